```python
import math
import jax, jax.numpy as jnp
from jax import lax
import numpy as np

D_MODEL = 1024
BATCH = 32
SEQ = 256
DEPTH = 4
DEC_BATCH = 8
DEC_SEQ = 4096
PAST_LEN = 512

GRID_W = 64
N_MIXERS = 3
N_SSD_LAYERS = (DEPTH + 2) // 3
N_ATTN_LAYERS = (DEPTH + 1) // 3
N_NAT_LAYERS = DEPTH // 3
EPS = 1e-6
NEG_INF = -1e30
SSD_D_INNER = 2 * D_MODEL
SSD_HEADDIM = 64
SSD_HEADS = SSD_D_INNER // SSD_HEADDIM
SSD_GROUPS = 4
SSD_HPG = SSD_HEADS // SSD_GROUPS
SSD_STATE = 128
SSD_CONV_W = 5
SSD_CHUNK = 128
SSD_CONV_DIM = SSD_D_INNER + 2 * SSD_GROUPS * SSD_STATE
SSD_IN_DIM = SSD_D_INNER + SSD_CONV_DIM + 2 * SSD_HEADS
ATTN_HEAD_DIM = 64
ATTN_HEADS = D_MODEL // ATTN_HEAD_DIM
ATTN_KV_HEADS = 4
ATTN_GROUP = ATTN_HEADS // ATTN_KV_HEADS
ATTN_QKV_DIM = (ATTN_HEADS + 2 * ATTN_KV_HEADS) * ATTN_HEAD_DIM
Q_BLOCK = 128
ROPE_BASE = 10000.0
NAT_HEAD_DIM = 64
NAT_HEADS = D_MODEL // NAT_HEAD_DIM
NAT_KH = 8
NAT_KW = 16
MLP_HIDDEN = 4 * D_MODEL

kernel_name = 'hybrid_diffusion_ssd_gqa_natten_step'


def rmsnorm(x, g):
    xf = x.astype(jnp.float32)
    y = xf * lax.rsqrt(jnp.mean(xf * xf, axis=-1, keepdims=True) + EPS)
    return (y * g.astype(jnp.float32)).astype(x.dtype)


def modulate(x, g, shift, scale):
    return rmsnorm(x, g) * (1 + scale) + shift


def ada_mods(cond, w, b):
    return jnp.split(jax.nn.silu(cond) @ w + b, 6, axis=-1)


def mlp(h, w1, w2):
    a = jax.nn.relu(h @ w1)
    return (a * a) @ w2


def axial_rope(x):
    L, hd = x.shape[1], x.shape[-1]
    n_pairs = hd // 4
    t = jnp.arange(L)
    row = (t // GRID_W).astype(jnp.float32)
    col = (t % GRID_W).astype(jnp.float32)
    freqs = ROPE_BASE ** (-jnp.arange(n_pairs, dtype=jnp.float32) / n_pairs)
    ang = jnp.concatenate([row[:, None] * freqs, col[:, None] * freqs], axis=-1)[:, None, :]
    cos, sin = jnp.cos(ang), jnp.sin(ang)
    xf = x.astype(jnp.float32).reshape(x.shape[:-1] + (hd // 2, 2))
    x0, x1 = xf[..., 0], xf[..., 1]
    out = jnp.stack([x0 * cos - x1 * sin, x0 * sin + x1 * cos], axis=-1)
    return out.reshape(x.shape).astype(x.dtype)


def blocked_attention(q, k, v):
    b, lq, kvh, g, hd = q.shape
    nb = lq // Q_BLOCK
    qb = q.reshape(b, nb, Q_BLOCK, kvh, g, hd).swapaxes(0, 1)
    scale = hd ** -0.5

    def one_block(qblk):
        s = jnp.einsum('bqkgd,bskd->bkgqs', qblk, k).astype(jnp.float32) * scale
        p = jax.nn.softmax(s, axis=-1).astype(v.dtype)
        return jnp.einsum('bkgqs,bskd->bqkgd', p, v)

    o = lax.map(one_block, qb)
    return o.swapaxes(0, 1).reshape(b, lq, kvh * g * hd)


def depthwise_conv(x, w, b):
    y = lax.conv_general_dilated(
        x, w[:, None, :], window_strides=(1,),
        padding=[(SSD_CONV_W // 2, SSD_CONV_W // 2)],
        dimension_numbers=('NWC', 'WIO', 'NWC'),
        feature_group_count=x.shape[-1])
    return y + b


def ssd_scan(x, dt, a, bmat, cmat, h0):
    b, L = x.shape[:2]
    nc = L // SSD_CHUNK
    tri = jnp.tril(jnp.ones((SSD_CHUNK, SSD_CHUNK), bool))[None, :, :, None, None]

    def chunkify(t):
        return t.reshape((b, nc, SSD_CHUNK) + t.shape[2:]).swapaxes(0, 1)

    def step(h, inp):
        xc, dtc, bc, cc = inp
        cs = jnp.cumsum(dtc * a, axis=1)
        seg = cs[:, :, None] - cs[:, None, :]
        decay = jnp.exp(jnp.where(tri, seg, -jnp.inf))
        cb = jnp.einsum('bign,bjgn->bijg', cc, bc)
        w = decay * cb[..., None] * dtc[:, None]
        y = (jnp.einsum('bijgr,bjgrp->bigrp', w, xc)
             + jnp.einsum('bign,bgrpn->bigrp', cc, h) * jnp.exp(cs)[..., None])
        to_end = jnp.exp(cs[:, -1:] - cs) * dtc
        h_new = (h * jnp.exp(cs[:, -1])[..., None, None]
                 + jnp.einsum('bjgr,bjgn,bjgrp->bgrpn', to_end, bc, xc))
        return h_new, y

    h_final, ys = lax.scan(step, h0.astype(jnp.float32),
                           (chunkify(x), chunkify(dt), chunkify(bmat), chunkify(cmat)))
    return ys.swapaxes(0, 1).reshape(x.shape), h_final


def ssd_mixer(h, w_in, conv_w, conv_b, dt_bias, a_log, d_skip, norm_g, w_out, h0_fwd, h0_bwd):
    b, L, _ = h.shape
    f32 = jnp.float32
    gn = SSD_GROUPS * SSD_STATE
    zxbcdt = h @ w_in
    z = zxbcdt[..., :SSD_D_INNER]
    xbc = jax.nn.silu(depthwise_conv(zxbcdt[..., SSD_D_INNER:SSD_D_INNER + SSD_CONV_DIM], conv_w, conv_b))
    dt_raw = zxbcdt[..., SSD_D_INNER + SSD_CONV_DIM:].reshape(b, L, 2, SSD_HEADS)
    xv = xbc[..., :SSD_D_INNER].reshape(b, L, SSD_GROUPS, SSD_HPG, SSD_HEADDIM).astype(f32)
    bm = xbc[..., SSD_D_INNER:SSD_D_INNER + gn].reshape(b, L, SSD_GROUPS, SSD_STATE).astype(f32)
    cm = xbc[..., SSD_D_INNER + gn:].reshape(b, L, SSD_GROUPS, SSD_STATE).astype(f32)
    dt = jax.nn.softplus((dt_raw + dt_bias).astype(f32)).reshape(b, L, 2, SSD_GROUPS, SSD_HPG)
    a = -jnp.exp(a_log.astype(f32)).reshape(2, SSD_GROUPS, SSD_HPG)
    y_f, h_f = ssd_scan(xv, dt[:, :, 0], a[0], bm, cm, h0_fwd)
    flip = lambda t: jnp.flip(t, axis=1)
    y_b, h_b = ssd_scan(flip(xv), flip(dt[:, :, 1]), a[1], flip(bm), flip(cm), h0_bwd)
    y = y_f + flip(y_b) + d_skip.astype(f32).reshape(SSD_GROUPS, SSD_HPG, 1) * xv
    y = y.reshape(b, L, SSD_D_INNER).astype(h.dtype)
    y = rmsnorm(y * jax.nn.silu(z), norm_g)
    return y @ w_out, h_f, h_b


def gqa_project(h, w_qkv, q_norm, k_norm):
    b, L, _ = h.shape
    qkv = h @ w_qkv
    nq = ATTN_HEADS * ATTN_HEAD_DIM
    nk = ATTN_KV_HEADS * ATTN_HEAD_DIM
    q = qkv[..., :nq].reshape(b, L, ATTN_HEADS, ATTN_HEAD_DIM)
    k = qkv[..., nq:nq + nk].reshape(b, L, ATTN_KV_HEADS, ATTN_HEAD_DIM)
    v = qkv[..., nq + nk:].reshape(b, L, ATTN_KV_HEADS, ATTN_HEAD_DIM)
    return rmsnorm(q, q_norm), rmsnorm(k, k_norm), v


def gqa_context(h, w_qkv, q_norm, k_norm, w_out):
    b, L, _ = h.shape
    q, k, v = gqa_project(h, w_qkv, q_norm, k_norm)
    o = blocked_attention(q.reshape(b, L, ATTN_KV_HEADS, ATTN_GROUP, ATTN_HEAD_DIM), k, v)
    return o @ w_out, k, v


def gqa_latent(h, w_qkv, q_norm, k_norm, w_out, ck, cv):
    b, L, _ = h.shape
    q, k, v = gqa_project(h, w_qkv, q_norm, k_norm)
    q, k = axial_rope(q), axial_rope(k)
    k_all = jnp.concatenate([k, ck], axis=1)
    v_all = jnp.concatenate([v, cv], axis=1)
    o = blocked_attention(q.reshape(b, L, ATTN_KV_HEADS, ATTN_GROUP, ATTN_HEAD_DIM), k_all, v_all)
    return o @ w_out


def nat_project(h, w_qkv):
    b, L, _ = h.shape
    qkv = (h @ w_qkv).reshape(b, L, 3, NAT_HEADS, NAT_HEAD_DIM)
    return qkv[:, :, 0], qkv[:, :, 1], qkv[:, :, 2]


def nat_context(h, w_qkv, rpb, w_out):
    del rpb
    q, k, v = nat_project(h, w_qkv)
    o = blocked_attention(q[:, :, :, None, :], k, v)
    return o @ w_out, k, v


def nat_latent(h, w_qkv, rpb, w_out, ck, cv):
    b, L, _ = h.shape
    f32 = jnp.float32
    rows = L // GRID_W
    kh = min(NAT_KH, rows)
    q, k, v = nat_project(h, w_qkv)
    scale = NAT_HEAD_DIM ** -0.5
    qg = q.reshape(b, rows, GRID_W, NAT_HEADS, NAT_HEAD_DIM).swapaxes(0, 1)
    kg = k.reshape(b, rows, GRID_W, NAT_HEADS, NAT_HEAD_DIM)
    vg = v.reshape(b, rows, GRID_W, NAT_HEADS, NAT_HEAD_DIM)
    col = jnp.arange(GRID_W)
    cstart = jnp.clip(col - NAT_KW // 2, 0, GRID_W - NAT_KW)
    col_ok = (col[None, :] >= cstart[:, None]) & (col[None, :] < cstart[:, None] + NAT_KW)
    dc_idx = jnp.clip(col[None, :] - col[:, None] + NAT_KW - 1, 0, 2 * NAT_KW - 2)
    rpb_c = rpb[:, :, dc_idx]

    def one_row(args):
        r, qr = args
        rs = jnp.clip(r - kh // 2, 0, rows - kh)
        kb = lax.dynamic_slice_in_dim(kg, rs, kh, axis=1)
        vb = lax.dynamic_slice_in_dim(vg, rs, kh, axis=1)
        bias = jnp.take(rpb_c, rs + jnp.arange(kh) - r + NAT_KH - 1, axis=1).transpose(0, 2, 1, 3)
        s_loc = jnp.einsum('bqhd,bakhd->bhqak', qr, kb).astype(f32) * scale + bias[None].astype(f32)
        s_loc = jnp.where(col_ok[None, None, :, None, :], s_loc, NEG_INF)
        s_ctx = jnp.einsum('bqhd,bshd->bhqs', qr, ck).astype(f32) * scale
        s = jnp.concatenate([s_loc.reshape(b, NAT_HEADS, GRID_W, kh * GRID_W), s_ctx], axis=-1)
        p = jax.nn.softmax(s, axis=-1).astype(v.dtype)
        p_loc = p[..., :kh * GRID_W].reshape(b, NAT_HEADS, GRID_W, kh, GRID_W)
        return (jnp.einsum('bhqak,bakhd->bqhd', p_loc, vb)
                + jnp.einsum('bhqs,bshd->bqhd', p[..., kh * GRID_W:], cv))

    o = lax.map(one_row, (jnp.arange(rows), qg))
    return o.swapaxes(0, 1).reshape(b, L, D_MODEL) @ w_out


def setup_inputs(seed: int = 0) -> dict:
    key = jax.random.key(seed)
    ks = list(jax.random.split(key, 32))
    f32 = jnp.float32

    def nrm(k, shape, scale=1.0):
        return scale * jax.random.normal(k, shape, f32)

    def gain(k, shape):
        return 1.0 + 0.02 * jax.random.normal(k, shape, f32)

    dt0 = jnp.exp(jax.random.uniform(ks[19], (N_SSD_LAYERS, 2, SSD_HEADS), f32, math.log(1e-3), math.log(1e-1)))
    a0 = jax.random.uniform(ks[20], (N_SSD_LAYERS, 2, SSD_HEADS), f32, 1.0, 16.0)
    return {
        'x_prompt': nrm(ks[0], (BATCH, SEQ, D_MODEL)),
        'x_sample': nrm(ks[1], (DEC_BATCH, DEC_SEQ, D_MODEL)),
        'c': nrm(ks[2], (DEC_BATCH, D_MODEL)),
        'state_ssd_fwd': nrm(ks[3], (DEC_BATCH, N_SSD_LAYERS, SSD_HEADS, SSD_HEADDIM, SSD_STATE), 0.5),
        'state_ssd_bwd': nrm(ks[4], (DEC_BATCH, N_SSD_LAYERS, SSD_HEADS, SSD_HEADDIM, SSD_STATE), 0.5),
        'cache_attn_k': nrm(ks[5], (DEC_BATCH, N_ATTN_LAYERS, PAST_LEN, ATTN_KV_HEADS, ATTN_HEAD_DIM)),
        'cache_attn_v': nrm(ks[6], (DEC_BATCH, N_ATTN_LAYERS, PAST_LEN, ATTN_KV_HEADS, ATTN_HEAD_DIM)),
        'cache_nat_k': nrm(ks[7], (DEC_BATCH, N_NAT_LAYERS, PAST_LEN, NAT_HEADS, NAT_HEAD_DIM)),
        'cache_nat_v': nrm(ks[8], (DEC_BATCH, N_NAT_LAYERS, PAST_LEN, NAT_HEADS, NAT_HEAD_DIM)),
        'c_ctx': nrm(ks[9], (D_MODEL,)),
        'ada_w': nrm(ks[10], (DEPTH, D_MODEL, 6 * D_MODEL), 0.5 * D_MODEL ** -0.5),
        'ada_b': nrm(ks[11], (DEPTH, 6 * D_MODEL), 0.02),
        'norm_mix': gain(ks[12], (DEPTH, D_MODEL)),
        'norm_mlp': gain(ks[13], (DEPTH, D_MODEL)),
        'mlp_w1': nrm(ks[14], (DEPTH, D_MODEL, MLP_HIDDEN), D_MODEL ** -0.5),
        'mlp_w2': nrm(ks[15], (DEPTH, MLP_HIDDEN, D_MODEL), MLP_HIDDEN ** -0.5),
        'ssd_w_in': nrm(ks[16], (N_SSD_LAYERS, D_MODEL, SSD_IN_DIM), D_MODEL ** -0.5),
        'ssd_conv_w': nrm(ks[17], (N_SSD_LAYERS, SSD_CONV_W, SSD_CONV_DIM), SSD_CONV_W ** -0.5),
        'ssd_conv_b': nrm(ks[18], (N_SSD_LAYERS, SSD_CONV_DIM), 0.02),
        'ssd_dt_bias': dt0 + jnp.log(-jnp.expm1(-dt0)),
        'ssd_a_log': jnp.log(a0),
        'ssd_d': gain(ks[21], (N_SSD_LAYERS, SSD_HEADS)),
        'ssd_norm': gain(ks[22], (N_SSD_LAYERS, SSD_D_INNER)),
        'ssd_w_out': nrm(ks[23], (N_SSD_LAYERS, SSD_D_INNER, D_MODEL), SSD_D_INNER ** -0.5),
        'attn_w_qkv': nrm(ks[24], (N_ATTN_LAYERS, D_MODEL, ATTN_QKV_DIM), D_MODEL ** -0.5),
        'attn_q_norm': gain(ks[25], (N_ATTN_LAYERS, ATTN_HEAD_DIM)),
        'attn_k_norm': gain(ks[26], (N_ATTN_LAYERS, ATTN_HEAD_DIM)),
        'attn_w_out': nrm(ks[27], (N_ATTN_LAYERS, D_MODEL, D_MODEL), D_MODEL ** -0.5),
        'nat_w_qkv': nrm(ks[28], (N_NAT_LAYERS, D_MODEL, 3 * D_MODEL), D_MODEL ** -0.5),
        'nat_rpb': nrm(ks[29], (N_NAT_LAYERS, NAT_HEADS, 2 * NAT_KH - 1, 2 * NAT_KW - 1), 0.1),
        'nat_w_out': nrm(ks[30], (N_NAT_LAYERS, D_MODEL, D_MODEL), D_MODEL ** -0.5),
        'norm_final': gain(ks[31], (D_MODEL,)),
    }


def reference(x_prompt, x_sample, c, state_ssd_fwd, state_ssd_bwd, cache_attn_k, cache_attn_v,
              cache_nat_k, cache_nat_v, c_ctx, ada_w, ada_b, norm_mix, norm_mlp, mlp_w1, mlp_w2,
              ssd_w_in, ssd_conv_w, ssd_conv_b, ssd_dt_bias, ssd_a_log, ssd_d, ssd_norm, ssd_w_out,
              attn_w_qkv, attn_q_norm, attn_k_norm, attn_w_out, nat_w_qkv, nat_rpb, nat_w_out, norm_final):
    n_p = x_prompt.shape[0]
    n_s = x_sample.shape[0]
    xp, xs = x_prompt, x_sample
    sf_out, sb_out, ak_out, av_out, nk_out, nv_out = [], [], [], [], [], []
    for i in range(DEPTH):
        kind, j = i % N_MIXERS, i // N_MIXERS
        sh_p, sc_p, g_p, sh2_p, sc2_p, g2_p = ada_mods(c_ctx, ada_w[i], ada_b[i])
        sh_s, sc_s, g_s, sh2_s, sc2_s, g2_s = ada_mods(c[:, None, :], ada_w[i], ada_b[i])
        hp = modulate(xp, norm_mix[i], sh_p, sc_p)
        hs = modulate(xs, norm_mix[i], sh_s, sc_s)
        if kind == 0:
            ssd_w = (ssd_w_in[j], ssd_conv_w[j], ssd_conv_b[j], ssd_dt_bias[j], ssd_a_log[j],
                     ssd_d[j], ssd_norm[j], ssd_w_out[j])
            zero = jnp.zeros((n_p, SSD_GROUPS, SSD_HPG, SSD_HEADDIM, SSD_STATE), jnp.float32)
            op, hf, hb = ssd_mixer(hp, *ssd_w, zero, zero)
            sshape = (n_s, SSD_GROUPS, SSD_HPG, SSD_HEADDIM, SSD_STATE)
            os_, _, _ = ssd_mixer(hs, *ssd_w, state_ssd_fwd[:, j].reshape(sshape),
                                  state_ssd_bwd[:, j].reshape(sshape))
            sf_out.append(hf.reshape(n_p, SSD_HEADS, SSD_HEADDIM, SSD_STATE).astype(xp.dtype))
            sb_out.append(hb.reshape(n_p, SSD_HEADS, SSD_HEADDIM, SSD_STATE).astype(xp.dtype))
        elif kind == 1:
            op, kp, vp = gqa_context(hp, attn_w_qkv[j], attn_q_norm[j], attn_k_norm[j], attn_w_out[j])
            os_ = gqa_latent(hs, attn_w_qkv[j], attn_q_norm[j], attn_k_norm[j], attn_w_out[j],
                             cache_attn_k[:, j], cache_attn_v[:, j])
            ak_out.append(kp)
            av_out.append(vp)
        else:
            op, kp, vp = nat_context(hp, nat_w_qkv[j], nat_rpb[j], nat_w_out[j])
            os_ = nat_latent(hs, nat_w_qkv[j], nat_rpb[j], nat_w_out[j], cache_nat_k[:, j], cache_nat_v[:, j])
            nk_out.append(kp)
            nv_out.append(vp)
        xp = xp + g_p * op
        xs = xs + g_s * os_
        xp = xp + g2_p * mlp(modulate(xp, norm_mlp[i], sh2_p, sc2_p), mlp_w1[i], mlp_w2[i])
        xs = xs + g2_s * mlp(modulate(xs, norm_mlp[i], sh2_s, sc2_s), mlp_w1[i], mlp_w2[i])
    y_prompt = rmsnorm(xp, norm_final)
    y_sample = rmsnorm(xs, norm_final)
    return (y_prompt, y_sample, jnp.stack(sf_out, axis=1), jnp.stack(sb_out, axis=1),
            jnp.stack(ak_out, axis=1), jnp.stack(av_out, axis=1),
            jnp.stack(nk_out, axis=1), jnp.stack(nv_out, axis=1))
```

```python
import functools
import math

import jax
import jax.numpy as jnp
from jax import lax
from jax.experimental import pallas as pl
from jax.experimental.pallas import tpu as pltpu

F32 = jnp.float32
BF16 = jnp.bfloat16

D_MODEL = 1024
DEPTH = 4
GRID_W = 64
N_MIXERS = 3
EPS = 1e-6
NEG_INF = -1e30
HEAD_DIM = 64
LANES = 128
SSD_D_INNER = 2 * D_MODEL
SSD_HEADS = SSD_D_INNER // HEAD_DIM
SSD_GROUPS = 4
SSD_HPG = SSD_HEADS // SSD_GROUPS
SSD_STATE = 128
SSD_CONV_W = 5
SSD_CHUNK = 128
SSD_CONV_DIM = SSD_D_INNER + 2 * SSD_GROUPS * SSD_STATE
SSD_ZX_DIM = SSD_D_INNER + SSD_CONV_DIM
ATTN_HEADS = D_MODEL // HEAD_DIM
ATTN_KV_HEADS = 4
ROPE_BASE = 10000.0
NAT_HEADS = D_MODEL // HEAD_DIM
NAT_KH = 8
NAT_KW = 16
MLP_HIDDEN = 4 * D_MODEL

VMEM_LIMIT_BYTES = 48 * 1024 * 1024
NAT_VMEM_LIMIT_BYTES = 56 * 1024 * 1024


def _params(semantics, vmem=VMEM_LIMIT_BYTES):
    return pltpu.CompilerParams(dimension_semantics=semantics, vmem_limit_bytes=vmem)


def _sigmoid(x):
    return 1.0 / (1.0 + jnp.exp(-x))


def _modulated_norm(x, gain, shift, scale):
    ms = jnp.mean(x * x, axis=-1, keepdims=True)
    return (x * lax.rsqrt(ms + EPS) * gain) * (1.0 + scale) + shift


def _ada_kernel(c_ref, w_ref, b_ref, o_ref):
    c = c_ref[...]
    a = (c * _sigmoid(c)).astype(BF16)
    o_ref[...] = jnp.dot(a, w_ref[...].astype(BF16), preferred_element_type=F32) + b_ref[...]


def _ada_mods(cond, ada_w, ada_b):
    depth, d, n = ada_w.shape
    r = cond.shape[0]
    tn = 1024
    return pl.pallas_call(
        _ada_kernel,
        out_shape=jax.ShapeDtypeStruct((depth, r, n), F32),
        grid=(depth, n // tn),
        in_specs=[pl.BlockSpec((r, d), lambda l, j: (0, 0)),
                  pl.BlockSpec((None, d, tn), lambda l, j: (l, 0, j)),
                  pl.BlockSpec((None, 1, tn), lambda l, j: (l, 0, j))],
        out_specs=pl.BlockSpec((None, r, tn), lambda l, j: (l, 0, j)),
        compiler_params=_params(("parallel", "parallel")),
        name="ada_mods",
    )(cond, ada_w, ada_b.reshape(depth, 1, n))


def _modmm_kernel(x_ref, mod_ref, g_ref, w_ref, o_ref, h_ref, *, shift_row, scale_row):
    @pl.when(pl.program_id(1) == 0)
    def _():
        h = _modulated_norm(x_ref[...], g_ref[...],
                            mod_ref[shift_row:shift_row + 1, :],
                            mod_ref[scale_row:scale_row + 1, :])
        h_ref[...] = h.astype(BF16)

    o_ref[...] = jnp.dot(h_ref[...], w_ref[...],
                         preferred_element_type=F32).astype(o_ref.dtype)


def _mod_matmul(x, mods, gain, w, *, rows_per_group, out_dtype, tm, tn, name):
    m, d = x.shape
    n = w.shape[1]
    tpg = rows_per_group // tm
    return pl.pallas_call(
        functools.partial(_modmm_kernel, shift_row=0, scale_row=1),
        out_shape=jax.ShapeDtypeStruct((m, n), out_dtype),
        grid=(m // tm, n // tn),
        in_specs=[pl.BlockSpec((tm, d), lambda i, j: (i, 0)),
                  pl.BlockSpec((None, 6, d), lambda i, j: (i // tpg, 0, 0)),
                  pl.BlockSpec((1, d), lambda i, j: (0, 0)),
                  pl.BlockSpec((d, tn), lambda i, j: (0, j))],
        out_specs=pl.BlockSpec((tm, tn), lambda i, j: (i, j)),
        scratch_shapes=[pltpu.VMEM((tm, d), BF16)],
        compiler_params=_params(("parallel", "arbitrary")),
        name=name,
    )(x, mods, gain.reshape(1, d), w)


def _proj_res_kernel(a_ref, x_ref, mod_ref, w_ref, o_ref, *, gate_row):
    y = jnp.dot(a_ref[...], w_ref[...], preferred_element_type=F32)
    o_ref[...] = x_ref[...] + mod_ref[gate_row:gate_row + 1, :] * y


def _proj_residual(a, x, mods, w, *, rows_per_group, tm, name):
    m, k = a.shape
    d = x.shape[1]
    tpg = rows_per_group // tm
    return pl.pallas_call(
        functools.partial(_proj_res_kernel, gate_row=2),
        out_shape=jax.ShapeDtypeStruct((m, d), F32),
        grid=(m // tm,),
        in_specs=[pl.BlockSpec((tm, k), lambda i: (i, 0)),
                  pl.BlockSpec((tm, d), lambda i: (i, 0)),
                  pl.BlockSpec((None, 6, d), lambda i: (i // tpg, 0, 0)),
                  pl.BlockSpec((k, d), lambda i: (0, 0))],
        out_specs=pl.BlockSpec((tm, d), lambda i: (i, 0)),
        compiler_params=_params(("parallel",)),
        name=name,
    )(a, x, mods, w)


def _mlp_kernel(x_ref, mod_ref, g_ref, w1_ref, w2_ref, gf_ref, o_ref, h_ref, acc_ref,
                *, final_norm):
    j = pl.program_id(1)

    @pl.when(j == 0)
    def _():
        h = _modulated_norm(x_ref[...], g_ref[...], mod_ref[3:4, :], mod_ref[4:5, :])
        h_ref[...] = h.astype(BF16)
        acc_ref[...] = jnp.zeros_like(acc_ref)

    a = jnp.dot(h_ref[...], w1_ref[...], preferred_element_type=F32)
    a = jnp.maximum(a, 0.0)
    acc_ref[...] += jnp.dot((a * a).astype(BF16), w2_ref[...], preferred_element_type=F32)

    @pl.when(j == pl.num_programs(1) - 1)
    def _():
        y = x_ref[...] + mod_ref[5:6, :] * acc_ref[...]
        if final_norm:
            ms = jnp.mean(y * y, axis=-1, keepdims=True)
            y = y * lax.rsqrt(ms + EPS) * gf_ref[...]
        o_ref[...] = y


def _mlp(x, mods, gain, w1, w2, gain_final, *, rows_per_group, final_norm, name):
    m, d = x.shape
    hdim = w1.shape[1]
    tm, th = 1024, 512
    tpg = rows_per_group // tm
    return pl.pallas_call(
        functools.partial(_mlp_kernel, final_norm=final_norm),
        out_shape=jax.ShapeDtypeStruct((m, d), F32),
        grid=(m // tm, hdim // th),
        in_specs=[pl.BlockSpec((tm, d), lambda i, j: (i, 0)),
                  pl.BlockSpec((None, 6, d), lambda i, j: (i // tpg, 0, 0)),
                  pl.BlockSpec((1, d), lambda i, j: (0, 0)),
                  pl.BlockSpec((d, th), lambda i, j: (0, j)),
                  pl.BlockSpec((th, d), lambda i, j: (j, 0)),
                  pl.BlockSpec((1, d), lambda i, j: (0, 0))],
        out_specs=pl.BlockSpec((tm, d), lambda i, j: (i, 0)),
        scratch_shapes=[pltpu.VMEM((tm, d), BF16), pltpu.VMEM((tm, d), F32)],
        compiler_params=_params(("parallel", "arbitrary")),
        name=name,
    )(x, mods, gain.reshape(1, d), w1, w2, gain_final.reshape(1, d))


CONV_HALO_ROWS = 16


def _conv_kernel(prev_ref, cur_ref, next_ref, w_ref, b_ref, o_ref, *, tiles_per_seq):
    pos = pl.program_id(0) % tiles_per_seq
    cur = cur_ref[...].astype(F32)
    tr = cur.shape[0]
    pv = jnp.where(pos == 0, 0.0, prev_ref[...].astype(F32))
    nx = jnp.where(pos == tiles_per_seq - 1, 0.0, next_ref[...].astype(F32))
    row = lax.broadcasted_iota(jnp.int32, cur.shape, 0)
    h = CONV_HALO_ROWS
    xm1 = jnp.where(row == 0, pv[h - 1:h], pltpu.roll(cur, 1, 0))
    xm2 = jnp.where(row == 0, pv[h - 2:h - 1],
                    jnp.where(row == 1, pv[h - 1:h], pltpu.roll(cur, 2, 0)))
    xp1 = jnp.where(row == tr - 1, nx[0:1], pltpu.roll(cur, tr - 1, 0))
    xp2 = jnp.where(row == tr - 2, nx[0:1],
                    jnp.where(row == tr - 1, nx[1:2], pltpu.roll(cur, tr - 2, 0)))
    w = w_ref[...]
    y = (w[0:1] * xm2 + w[1:2] * xm1 + w[2:3] * cur + w[3:4] * xp1 + w[4:5] * xp2
         + b_ref[...])
    o_ref[...] = (y * _sigmoid(y)).astype(o_ref.dtype)


def _conv_silu(zx, conv_w, conv_b, *, seq_len):
    m = zx.shape[0]
    tr, tc = 256, 512
    c0 = SSD_D_INNER // tc
    halo_per_tile = tr // CONV_HALO_ROWS
    n_halo = m // CONV_HALO_ROWS
    return pl.pallas_call(
        functools.partial(_conv_kernel, tiles_per_seq=seq_len // tr),
        out_shape=jax.ShapeDtypeStruct((m, SSD_CONV_DIM), BF16),
        grid=(m // tr, SSD_CONV_DIM // tc),
        in_specs=[
            pl.BlockSpec((CONV_HALO_ROWS, tc),
                         lambda i, j: (jnp.maximum(i * halo_per_tile - 1, 0), c0 + j)),
            pl.BlockSpec((tr, tc), lambda i, j: (i, c0 + j)),
            pl.BlockSpec((CONV_HALO_ROWS, tc),
                         lambda i, j: (jnp.minimum((i + 1) * halo_per_tile, n_halo - 1), c0 + j)),
            pl.BlockSpec((SSD_CONV_W, tc), lambda i, j: (0, j)),
            pl.BlockSpec((1, tc), lambda i, j: (0, j))],
        out_specs=pl.BlockSpec((tr, tc), lambda i, j: (i, j)),
        compiler_params=_params(("parallel", "parallel")),
        name="ssd_conv_silu",
    )(zx, zx, zx, conv_w, conv_b.reshape(1, SSD_CONV_DIM))


def _split3(a):
    a1 = a.astype(BF16)
    r1 = a - a1.astype(F32)
    a2 = r1.astype(BF16)
    a3 = (r1 - a2.astype(F32)).astype(BF16)
    return a1, a2, a3


def _dot_exact_left(m01, a):
    return sum(jnp.dot(m01, p, preferred_element_type=F32) for p in _split3(a))


def _dot_exact_right(a, m01):
    return sum(jnp.dot(p, m01, preferred_element_type=F32) for p in _split3(a))


def _ssd_kernel(*refs, reverse, has_h0, emit_state, combine):
    it = iter(refs)
    xbc_ref, dt_ref, dtb_ref, alog_ref, e_ref = (next(it) for _ in range(5))
    h0_ref = next(it) if has_h0 else None
    if combine:
        z_ref, yf_ref, ng_ref = next(it), next(it), next(it)
    else:
        drep_ref = next(it)
    y_ref = next(it)
    st_ref = next(it) if emit_state else None
    s_ref = next(it)
    ybuf_ref = next(it) if combine else y_ref

    q = SSD_CHUNK
    gw = SSD_HPG * HEAD_DIM
    c = pl.program_id(1)
    col0 = SSD_HEADS if reverse else 0

    @pl.when(c == 0)
    def _():
        if has_h0:
            for k in range(SSD_D_INNER // q):
                s_ref[:, k * q:(k + 1) * q] = h0_ref[k * q:(k + 1) * q, :].T
        else:
            s_ref[...] = jnp.zeros_like(s_ref)

    pre = dt_ref[...] + dtb_ref[...]
    dt = jnp.maximum(pre, 0.0) + jnp.log1p(jnp.exp(-jnp.abs(pre)))
    a_dt = dt * (-jnp.exp(alog_ref[...]))
    ri = lax.broadcasted_iota(jnp.int32, (q, q), 0)
    ci = lax.broadcasted_iota(jnp.int32, (q, q), 1)
    keep = (ri <= ci) if reverse else (ri >= ci)
    cs = _dot_exact_left(jnp.where(keep, 1.0, 0.0).astype(BF16), a_dt)
    cs_t = cs.T
    dt_t = dt.T
    e01 = e_ref[...]
    cs_rep = _dot_exact_right(cs, e01)
    dt_rep = _dot_exact_right(dt, e01)
    edge = cs_rep[0:1, :] if reverse else cs_rep[q - 1:q, :]
    exp_cs = jnp.exp(cs_rep)
    to_end = jnp.exp(edge - cs_rep) * dt_rep
    exp_edge = jnp.exp(edge)
    lane = lax.broadcasted_iota(jnp.int32, (q, LANES), 1)

    for g in range(SSD_GROUPS):
        gs = slice(g * gw, (g + 1) * gw)
        b_g = xbc_ref[:, SSD_D_INNER + g * SSD_STATE:SSD_D_INNER + (g + 1) * SSD_STATE]
        c_off = SSD_D_INNER + SSD_GROUPS * SSD_STATE
        c_g = xbc_ref[:, c_off + g * SSD_STATE:c_off + (g + 1) * SSD_STATE]
        cb = lax.dot_general(c_g, b_g, (((1,), (1,)), ((), ())), preferred_element_type=F32)
        s_g = s_ref[:, gs]
        y_off = jnp.dot(c_g, s_g.astype(BF16), preferred_element_type=F32) * exp_cs[:, gs]
        for mp in range(gw // LANES):
            m = g * (gw // LANES) + mp
            ls = slice(m * LANES, (m + 1) * LANES)
            x_pair = xbc_ref[:, ls]
            ys = []
            for h in (2 * m, 2 * m + 1):
                k = col0 + h
                seg = cs[:, k:k + 1] - cs_t[k:k + 1, :]
                dec = jnp.exp(jnp.where(keep, seg, -jnp.inf))
                w = (dec * cb * dt_t[k:k + 1, :]).astype(BF16)
                ys.append(jnp.dot(w, x_pair, preferred_element_type=F32))
            y_pair = jnp.where(lane < HEAD_DIM, ys[0], ys[1]) + y_off[:, mp * LANES:(mp + 1) * LANES]
            if not combine:
                y_pair = y_pair + drep_ref[:, ls] * x_pair.astype(F32)
            ybuf_ref[:, ls] = y_pair
        xs_g = (to_end[:, gs] * xbc_ref[:, gs].astype(F32)).astype(BF16)
        b_t = b_g.astype(F32).T.astype(BF16)
        s_ref[:, gs] = s_g * exp_edge[:, gs] + jnp.dot(b_t, xs_g, preferred_element_type=F32)

    if combine:
        z = z_ref[...].astype(F32)
        gated = (ybuf_ref[...] + yf_ref[...]) * (z * _sigmoid(z))
        ms = jnp.mean(gated * gated, axis=-1, keepdims=True)
        y_ref[...] = (gated * lax.rsqrt(ms + EPS) * ng_ref[...]).astype(y_ref.dtype)

    if emit_state:
        @pl.when(c == pl.num_programs(1) - 1)
        def _():
            for k in range(SSD_D_INNER // q):
                st_ref[k * q:(k + 1) * q, :] = s_ref[:, k * q:(k + 1) * q].T


def _ssd_scan(xbc, dt_raw, dt_bias, a_log, expand, *, nseq, seq_len, reverse, h0=None,
              layer=0, emit_state=False, zx=None, y_fwd=None, norm_g=None, d_rep=None, name):
    combine = zx is not None
    nchunk = seq_len // SSD_CHUNK
    q = SSD_CHUNK

    def row_blk(b, c):
        return b * nchunk + ((nchunk - 1 - c) if reverse else c)

    in_specs = [pl.BlockSpec((q, SSD_CONV_DIM), lambda b, c: (row_blk(b, c), 0)),
                pl.BlockSpec((q, LANES), lambda b, c: (row_blk(b, c), 0)),
                pl.BlockSpec((1, LANES), lambda b, c: (0, 0)),
                pl.BlockSpec((1, LANES), lambda b, c: (0, 0)),
                pl.BlockSpec((LANES, SSD_D_INNER), lambda b, c: (0, 0))]
    args = [xbc, dt_raw, dt_bias, a_log, expand]
    if h0 is not None:
        in_specs.append(pl.BlockSpec((None, None, SSD_D_INNER, SSD_STATE),
                                     lambda b, c: (b, layer, 0, 0)))
        args.append(h0)
    if combine:
        in_specs += [pl.BlockSpec((q, SSD_D_INNER), lambda b, c: (row_blk(b, c), 0)),
                     pl.BlockSpec((q, SSD_D_INNER), lambda b, c: (row_blk(b, c), 0)),
                     pl.BlockSpec((1, SSD_D_INNER), lambda b, c: (0, 0))]
        args += [zx, y_fwd, norm_g]
    else:
        in_specs.append(pl.BlockSpec((1, SSD_D_INNER), lambda b, c: (0, 0)))
        args.append(d_rep)
    m = nseq * seq_len
    out_shape = [jax.ShapeDtypeStruct((m, SSD_D_INNER), BF16 if combine else F32)]
    out_specs = [pl.BlockSpec((q, SSD_D_INNER), lambda b, c: (row_blk(b, c), 0))]
    if emit_state:
        out_shape.append(jax.ShapeDtypeStruct((nseq, SSD_D_INNER, SSD_STATE), F32))
        out_specs.append(pl.BlockSpec((None, SSD_D_INNER, SSD_STATE), lambda b, c: (b, 0, 0)))
    scratch = [pltpu.VMEM((SSD_STATE, SSD_D_INNER), F32)]
    if combine:
        scratch.append(pltpu.VMEM((q, SSD_D_INNER), F32))
    outs = pl.pallas_call(
        functools.partial(_ssd_kernel, reverse=reverse, has_h0=h0 is not None,
                          emit_state=emit_state, combine=combine),
        out_shape=out_shape,
        grid=(nseq, nchunk),
        in_specs=in_specs,
        out_specs=out_specs,
        scratch_shapes=scratch,
        compiler_params=_params(("parallel", "arbitrary")),
        name=name,
    )(*args)
    return outs


def _head_rms(x, gain_row):
    rows, width = x.shape
    lane = lax.broadcasted_iota(jnp.int32, (rows, LANES), 1)
    lo = lane < HEAD_DIM
    outs = []
    for j in range(width // LANES):
        xb = x[:, j * LANES:(j + 1) * LANES]
        sq = xb * xb
        s_lo = jnp.sum(jnp.where(lo, sq, 0.0), axis=-1, keepdims=True)
        s_hi = jnp.sum(jnp.where(lo, 0.0, sq), axis=-1, keepdims=True)
        ms = jnp.where(lo, s_lo, s_hi) * (1.0 / HEAD_DIM)
        outs.append(xb * lax.rsqrt(ms + EPS))
    return jnp.concatenate(outs, axis=1) * gain_row


def _rope(x, cos, sin_signed):
    rows, width = x.shape
    lane = lax.broadcasted_iota(jnp.int32, (rows, LANES), 1)
    even = (lane % 2) == 0
    outs = []
    for j in range(width // LANES):
        xb = x[:, j * LANES:(j + 1) * LANES]
        partner = jnp.where(even, pltpu.roll(xb, LANES - 1, 1), pltpu.roll(xb, 1, 1))
        outs.append(xb * cos + partner * sin_signed)
    return jnp.concatenate(outs, axis=1)


def _qknorm_kernel(*refs, rope, emit_cache):
    it = iter(refs)
    q_ref, k_ref, v_ref, qg_ref, kg_ref = (next(it) for _ in range(5))
    if rope:
        cos_ref, sin_ref = next(it), next(it)
    qo_ref, ko_ref, vo_ref = next(it), next(it), next(it)
    q = _head_rms(q_ref[...], qg_ref[...])
    k = _head_rms(k_ref[...], kg_ref[...])
    if emit_cache:
        kc_ref, vc_ref = next(it), next(it)
        kc_ref[...] = k
        vc_ref[...] = v_ref[...]
    if rope:
        q = _rope(q, cos_ref[...], sin_ref[...])
        k = _rope(k, cos_ref[...], sin_ref[...])
    qo_ref[...] = (q * (HEAD_DIM ** -0.5)).astype(BF16)
    ko_ref[...] = k.astype(BF16)
    vo_ref[...] = v_ref[...].astype(BF16)


def _rope_tables(seq_len):
    n_pairs = HEAD_DIM // 4
    t = jnp.arange(seq_len)
    row = (t // GRID_W).astype(F32)
    col = (t % GRID_W).astype(F32)
    freqs = ROPE_BASE ** (-jnp.arange(n_pairs, dtype=F32) / n_pairs)
    ang = jnp.concatenate([row[:, None] * freqs, col[:, None] * freqs], axis=-1)
    cos = jnp.repeat(jnp.cos(ang), 2, axis=-1)
    sin = jnp.repeat(jnp.sin(ang), 2, axis=-1)
    sign = jnp.where(jnp.arange(HEAD_DIM) % 2 == 0, -1.0, 1.0).astype(F32)
    return jnp.tile(cos, (1, 2)), jnp.tile(sin * sign, (1, 2))


def _qk_norm(qkv, q_gain, k_gain, *, seq_len, rope, emit_cache):
    m = qkv.shape[0]
    tm = 256
    nq = ATTN_HEADS * HEAD_DIM
    nk = ATTN_KV_HEADS * HEAD_DIM
    qg = jnp.tile(q_gain, ATTN_HEADS).reshape(1, nq)
    kg = jnp.tile(k_gain, ATTN_KV_HEADS).reshape(1, nk)
    in_specs = [pl.BlockSpec((tm, nq), lambda i: (i, 0)),
                pl.BlockSpec((tm, nk), lambda i: (i, nq // nk)),
                pl.BlockSpec((tm, nk), lambda i: (i, nq // nk + 1)),
                pl.BlockSpec((1, nq), lambda i: (0, 0)),
                pl.BlockSpec((1, nk), lambda i: (0, 0))]
    args = [qkv, qkv, qkv, qg, kg]
    if rope:
        cos, sin = _rope_tables(seq_len)
        tps = seq_len // tm
        in_specs += [pl.BlockSpec((tm, LANES), lambda i: (i % tps, 0)),
                     pl.BlockSpec((tm, LANES), lambda i: (i % tps, 0))]
        args += [cos, sin]
    out_shape = [jax.ShapeDtypeStruct((m, nq), BF16), jax.ShapeDtypeStruct((m, nk), BF16),
                 jax.ShapeDtypeStruct((m, nk), BF16)]
    out_specs = [pl.BlockSpec((tm, nq), lambda i: (i, 0)), pl.BlockSpec((tm, nk), lambda i: (i, 0)),
                 pl.BlockSpec((tm, nk), lambda i: (i, 0))]
    if emit_cache:
        out_shape += [jax.ShapeDtypeStruct((m, nk), F32)] * 2
        out_specs += [pl.BlockSpec((tm, nk), lambda i: (i, 0))] * 2
    return pl.pallas_call(
        functools.partial(_qknorm_kernel, rope=rope, emit_cache=emit_cache),
        out_shape=out_shape,
        grid=(m // tm,),
        in_specs=in_specs,
        out_specs=out_specs,
        compiler_params=_params(("parallel",)),
        name="gqa_qk_norm_rope" if rope else "gqa_qk_norm",
    )(*args)


def _place_half(x, lane, src_half, dst_half):
    if src_half != dst_half:
        x = pltpu.roll(x, HEAD_DIM, 1)
    keep = (lane >= HEAD_DIM) if dst_half else (lane < HEAD_DIM)
    return jnp.where(keep, x, 0.0)


def _attn_kernel(*refs, hq, hkv, tk, has_ctx, q_scale):
    if has_ctx:
        q_ref, k_ref, v_ref, ck_ref, cv_ref, o_ref = refs
    else:
        q_ref, k_ref, v_ref, o_ref = refs
    grp = hq // hkv
    tq = q_ref.shape[0]
    rows = grp * tq
    lane = lax.broadcasted_iota(jnp.int32, (tq, LANES), 1)
    head_out = [None] * hq

    for g in range(hkv):
        half = g % 2
        kc = slice((g // 2) * LANES, (g // 2 + 1) * LANES)
        parts = []
        for u in range(grp):
            h = g * grp + u
            qb = q_ref[:, (h // 2) * LANES:(h // 2 + 1) * LANES].astype(F32)
            if q_scale != 1.0:
                qb = qb * q_scale
            parts.append(_place_half(qb, lane, h % 2, half).astype(BF16))
        qst = parts[0] if grp == 1 else jnp.concatenate(parts, axis=0)

        def step(kblk, vblk, carry, qst=qst):
            m_i, l_i, acc = carry
            s = lax.dot_general(qst, kblk.astype(BF16), (((1,), (1,)), ((), ())),
                                preferred_element_type=F32)
            m_n = jnp.maximum(m_i, jnp.max(s, axis=-1, keepdims=True))
            alpha = jnp.exp(m_i - m_n)
            p = jnp.exp(s - m_n)
            l_n = alpha * l_i + jnp.sum(p, axis=-1, keepdims=True)
            acc = alpha * acc + jnp.dot(p.astype(BF16), vblk.astype(BF16),
                                        preferred_element_type=F32)
            return m_n, l_n, acc

        carry = (jnp.full((rows, 1), -jnp.inf, F32), jnp.zeros((rows, 1), F32),
                 jnp.zeros((rows, LANES), F32))
        n_self = k_ref.shape[0] // tk
        if n_self == 1:
            carry = step(k_ref[:, kc], v_ref[:, kc], carry)
        else:
            def body(ci, car, kc=kc, step=step):
                r0 = pl.multiple_of(ci * tk, tk)
                return step(k_ref[pl.ds(r0, tk), kc], v_ref[pl.ds(r0, tk), kc], car)
            carry = lax.fori_loop(0, n_self, body, carry)
        if has_ctx:
            for ci in range(ck_ref.shape[0] // tk):
                carry = step(ck_ref[ci * tk:(ci + 1) * tk, kc], cv_ref[ci * tk:(ci + 1) * tk, kc],
                             carry)
        _, l_f, acc = carry
        o = acc / l_f
        for u in range(grp):
            head_out[g * grp + u] = (o[u * tq:(u + 1) * tq], half)

    for mpair in range(hq // 2):
        (oa, ha), (ob, hb) = head_out[2 * mpair], head_out[2 * mpair + 1]
        if ha != 0:
            oa = pltpu.roll(oa, HEAD_DIM, 1)
        if hb != 1:
            ob = pltpu.roll(ob, HEAD_DIM, 1)
        o_ref[:, mpair * LANES:(mpair + 1) * LANES] = jnp.where(
            lane < HEAD_DIM, oa, ob).astype(o_ref.dtype)


def _attention(q_arr, q_col, k_arr, k_col, v_arr, v_col, ctx_k, ctx_v, *, nb, seq_len,
               hq, hkv, tq, tk, q_scale, name):
    wq, wk = hq * HEAD_DIM, hkv * HEAD_DIM
    qt = seq_len // tq
    in_specs = [pl.BlockSpec((tq, wq), lambda b, i: (b * qt + i, q_col)),
                pl.BlockSpec((seq_len, wk), lambda b, i: (b, k_col)),
                pl.BlockSpec((seq_len, wk), lambda b, i: (b, v_col))]
    args = [q_arr, k_arr, v_arr]
    has_ctx = ctx_k is not None
    if has_ctx:
        lc = ctx_k.shape[1]
        in_specs += [pl.BlockSpec((None, lc, wk), lambda b, i: (b, 0, 0))] * 2
        args += [ctx_k, ctx_v]
    return pl.pallas_call(
        functools.partial(_attn_kernel, hq=hq, hkv=hkv, tk=tk, has_ctx=has_ctx, q_scale=q_scale),
        out_shape=jax.ShapeDtypeStruct((nb * seq_len, wq), BF16),
        grid=(nb, qt),
        in_specs=in_specs,
        out_specs=pl.BlockSpec((tq, wq), lambda b, i: (b * qt + i, 0)),
        compiler_params=_params(("parallel", "arbitrary")),
        name=name,
    )(*args)


def _nat_kernel(q_ref, k_ref, v_ref, ck_ref, cv_ref, bias_ref, o_ref, *, rows):
    r = pl.program_id(1)
    kh = NAT_KH
    rs = jnp.clip(r - kh // 2, 0, rows - kh)
    k0 = pl.multiple_of(rs * GRID_W, GRID_W)
    nloc = kh * GRID_W
    lane = lax.broadcasted_iota(jnp.int32, (GRID_W, LANES), 1)
    nt = (((1,), (1,)), ((), ()))
    for mpair in range(NAT_HEADS // 2):
        ls = slice(mpair * LANES, (mpair + 1) * LANES)
        qb = q_ref[:, ls].astype(F32) * (HEAD_DIM ** -0.5)
        k_loc = k_ref[pl.ds(k0, nloc), ls]
        v_loc = v_ref[pl.ds(k0, nloc), ls]
        k_ctx = ck_ref[:, ls]
        v_ctx = cv_ref[:, ls]
        outs = []
        for half in (0, 1):
            h = 2 * mpair + half
            qh = _place_half(qb, lane, half, half).astype(BF16)
            s_loc = lax.dot_general(qh, k_loc, nt, preferred_element_type=F32) + bias_ref[h]
            s_ctx = lax.dot_general(qh, k_ctx, nt, preferred_element_type=F32)
            mx = jnp.maximum(jnp.max(s_loc, axis=-1, keepdims=True),
                             jnp.max(s_ctx, axis=-1, keepdims=True))
            p_loc = jnp.exp(s_loc - mx)
            p_ctx = jnp.exp(s_ctx - mx)
            denom = (jnp.sum(p_loc, axis=-1, keepdims=True)
                     + jnp.sum(p_ctx, axis=-1, keepdims=True))
            o = (jnp.dot(p_loc.astype(BF16), v_loc, preferred_element_type=F32)
                 + jnp.dot(p_ctx.astype(BF16), v_ctx, preferred_element_type=F32))
            outs.append(o / denom)
        o_ref[:, ls] = jnp.where(lane < HEAD_DIM, outs[0], outs[1]).astype(o_ref.dtype)


def _nat_bias(rpb):
    col = jnp.arange(GRID_W)
    cstart = jnp.clip(col - NAT_KW // 2, 0, GRID_W - NAT_KW)
    col_ok = (col[None, :] >= cstart[:, None]) & (col[None, :] < cstart[:, None] + NAT_KW)
    dc_idx = jnp.clip(col[None, :] - col[:, None] + NAT_KW - 1, 0, 2 * NAT_KW - 2)
    rpb_c = jnp.where(col_ok[None, None], rpb[:, :, dc_idx], NEG_INF)
    variants = []
    for d0 in range(NAT_KH):
        blk = rpb_c[:, d0:d0 + NAT_KH].transpose(0, 2, 1, 3)
        variants.append(blk.reshape(NAT_HEADS, GRID_W, NAT_KH * GRID_W))
    return jnp.stack(variants, axis=0)


def _nat_latent(qkv, ctx_k, ctx_v, rpb, *, nb, seq_len):
    rows = seq_len // GRID_W
    kh = NAT_KH
    bias = _nat_bias(rpb)
    lc = ctx_k.shape[1]
    wd = NAT_HEADS * HEAD_DIM

    def bias_idx(b, r):
        rs = jnp.clip(r - kh // 2, 0, rows - kh)
        return (rs - r + kh - 1, 0, 0, 0)

    return pl.pallas_call(
        functools.partial(_nat_kernel, rows=rows),
        out_shape=jax.ShapeDtypeStruct((nb * seq_len, wd), BF16),
        grid=(nb, rows),
        in_specs=[pl.BlockSpec((GRID_W, wd), lambda b, r: (b * rows + r, 0)),
                  pl.BlockSpec((seq_len, wd), lambda b, r: (b, 1)),
                  pl.BlockSpec((seq_len, wd), lambda b, r: (b, 2)),
                  pl.BlockSpec((None, lc, wd), lambda b, r: (b, 0, 0)),
                  pl.BlockSpec((None, lc, wd), lambda b, r: (b, 0, 0)),
                  pl.BlockSpec((None, NAT_HEADS, GRID_W, kh * GRID_W), bias_idx)],
        out_specs=pl.BlockSpec((GRID_W, wd), lambda b, r: (b * rows + r, 0)),
        compiler_params=_params(("parallel", "arbitrary"), NAT_VMEM_LIMIT_BYTES),
        name="nat_latent",
    )(qkv, qkv, qkv, ctx_k, ctx_v, bias)


def _ssd_layer(xp, xs, mods_p, mods_s, gain, j, np_, ns_, lp, ls, state_f, state_b,
               ssd_w_in, ssd_conv_w, ssd_conv_b, ssd_dt_bias, ssd_a_log, ssd_d, ssd_norm,
               ssd_w_out):
    w_zx = ssd_w_in[j, :, :SSD_ZX_DIM].astype(BF16)
    w_dt = jnp.pad(ssd_w_in[j, :, SSD_ZX_DIM:], ((0, 0), (0, LANES - 2 * SSD_HEADS))).astype(BF16)
    w_out = ssd_w_out[j].astype(BF16)
    pad = LANES - 2 * SSD_HEADS
    dt_bias = jnp.pad(ssd_dt_bias[j].reshape(-1), (0, pad)).reshape(1, LANES)
    a_log = jnp.pad(ssd_a_log[j].reshape(-1), (0, pad)).reshape(1, LANES)
    d_rep = jnp.repeat(ssd_d[j], HEAD_DIM).reshape(1, SSD_D_INNER)
    norm_g = ssd_norm[j].reshape(1, SSD_D_INNER)
    head_of_col = jnp.arange(SSD_D_INNER) // HEAD_DIM
    expand = [(jnp.arange(LANES)[:, None] == head_of_col[None, :] + off).astype(BF16)
              for off in (0, SSD_HEADS)]

    outs = []
    states = None
    for (x, mods, nseq, seq_len, is_prompt) in ((xp, mods_p, np_, lp, True),
                                                (xs, mods_s, ns_, ls, False)):
        tag = "p" if is_prompt else "s"
        zx = _mod_matmul(x, mods, gain, w_zx, rows_per_group=seq_len if not is_prompt else x.shape[0],
                         out_dtype=BF16, tm=1024, tn=1024, name="ssd_in_zx_" + tag)
        dt_raw = _mod_matmul(x, mods, gain, w_dt,
                             rows_per_group=seq_len if not is_prompt else x.shape[0],
                             out_dtype=F32, tm=1024, tn=LANES, name="ssd_in_dt_" + tag)
        xbc = _conv_silu(zx, ssd_conv_w[j], ssd_conv_b[j], seq_len=seq_len)
        common = dict(nseq=nseq, seq_len=seq_len, layer=j)
        fwd = _ssd_scan(xbc, dt_raw, dt_bias, a_log, expand[0], reverse=False,
                        h0=None if is_prompt else state_f, emit_state=is_prompt, d_rep=d_rep,
                        name="ssd_scan_fwd_" + tag, **common)
        bwd = _ssd_scan(xbc, dt_raw, dt_bias, a_log, expand[1], reverse=True,
                        h0=None if is_prompt else state_b, emit_state=is_prompt, zx=zx,
                        y_fwd=fwd[0], norm_g=norm_g, name="ssd_scan_bwd_" + tag, **common)
        if is_prompt:
            states = (fwd[1], bwd[1])
        outs.append(_proj_residual(bwd[0], x, mods,
                                   w_out, rows_per_group=seq_len if not is_prompt else x.shape[0],
                                   tm=512, name="ssd_out_" + tag))
    return outs[0], outs[1], states


def kernel(x_prompt, x_sample, c, state_ssd_fwd, state_ssd_bwd, cache_attn_k, cache_attn_v, cache_nat_k, cache_nat_v, c_ctx, ada_w, ada_b, norm_mix, norm_mlp, mlp_w1, mlp_w2, ssd_w_in, ssd_conv_w, ssd_conv_b, ssd_dt_bias, ssd_a_log, ssd_d, ssd_norm, ssd_w_out, attn_w_qkv, attn_q_norm, attn_k_norm, attn_w_out, nat_w_qkv, nat_rpb, nat_w_out, norm_final):
    np_, lp, d = x_prompt.shape
    ns_, ls, _ = x_sample.shape
    mp_rows, ms_rows = np_ * lp, ns_ * ls
    xp = x_prompt.reshape(mp_rows, d)
    xs = x_sample.reshape(ms_rows, d)

    cond_rows = 16
    cond = jnp.concatenate([c_ctx[None, :], c, jnp.zeros((cond_rows - 1 - ns_, d), F32)], axis=0)
    mods_all = _ada_mods(cond, ada_w, ada_b).reshape(DEPTH, cond_rows, 6, d)

    n_ssd = state_ssd_fwd.shape[1]
    st_f = state_ssd_fwd.reshape(ns_, n_ssd, SSD_D_INNER, SSD_STATE)
    st_b = state_ssd_bwd.reshape(ns_, n_ssd, SSD_D_INNER, SSD_STATE)

    sf_out, sb_out, ak_out, av_out, nk_out, nv_out = [], [], [], [], [], []
    for i in range(DEPTH):
        kind, j = i % N_MIXERS, i // N_MIXERS
        mods_p = mods_all[i, 0:1]
        mods_s = mods_all[i, 1:1 + ns_]
        rpg_p, rpg_s = mp_rows, ls
        if kind == 0:
            xp, xs, (hf, hb) = _ssd_layer(
                xp, xs, mods_p, mods_s, norm_mix[i], j, np_, ns_, lp, ls, st_f, st_b,
                ssd_w_in, ssd_conv_w, ssd_conv_b, ssd_dt_bias, ssd_a_log, ssd_d, ssd_norm,
                ssd_w_out)
            sf_out.append(hf.reshape(np_, SSD_HEADS, HEAD_DIM, SSD_STATE))
            sb_out.append(hb.reshape(np_, SSD_HEADS, HEAD_DIM, SSD_STATE))
        elif kind == 1:
            w_qkv = attn_w_qkv[j].astype(BF16)
            w_out = attn_w_out[j].astype(BF16)
            nk = ATTN_KV_HEADS * HEAD_DIM
            qkv = _mod_matmul(xp, mods_p, norm_mix[i], w_qkv, rows_per_group=rpg_p,
                              out_dtype=F32, tm=1024, tn=768, name="gqa_qkv_p")
            q, k, v, kc, vc = _qk_norm(qkv, attn_q_norm[j], attn_k_norm[j], seq_len=lp,
                                       rope=False, emit_cache=True)
            o = _attention(q, 0, k, 0, v, 0, None, None, nb=np_, seq_len=lp, hq=ATTN_HEADS,
                           hkv=ATTN_KV_HEADS, tq=128, tk=lp, q_scale=1.0, name="gqa_attn_p")
            xp = _proj_residual(o, xp, mods_p, w_out, rows_per_group=rpg_p, tm=512,
                                name="gqa_out_p")
            ak_out.append(kc.reshape(np_, lp, ATTN_KV_HEADS, HEAD_DIM))
            av_out.append(vc.reshape(np_, lp, ATTN_KV_HEADS, HEAD_DIM))
            qkv = _mod_matmul(xs, mods_s, norm_mix[i], w_qkv, rows_per_group=rpg_s,
                              out_dtype=F32, tm=1024, tn=768, name="gqa_qkv_s")
            q, k, v = _qk_norm(qkv, attn_q_norm[j], attn_k_norm[j], seq_len=ls, rope=True,
                               emit_cache=False)
            past = cache_attn_k.shape[2]
            ck = cache_attn_k[:, j].reshape(ns_, past, nk).astype(BF16)
            cv = cache_attn_v[:, j].reshape(ns_, past, nk).astype(BF16)
            o = _attention(q, 0, k, 0, v, 0, ck, cv, nb=ns_, seq_len=ls, hq=ATTN_HEADS,
                           hkv=ATTN_KV_HEADS, tq=128, tk=512, q_scale=1.0, name="gqa_attn_s")
            xs = _proj_residual(o, xs, mods_s, w_out, rows_per_group=rpg_s, tm=512,
                                name="gqa_out_s")
        else:
            w_qkv = nat_w_qkv[j].astype(BF16)
            w_out = nat_w_out[j].astype(BF16)
            wd = NAT_HEADS * HEAD_DIM
            scale = HEAD_DIM ** -0.5
            qkv = _mod_matmul(xp, mods_p, norm_mix[i], w_qkv, rows_per_group=rpg_p,
                              out_dtype=F32, tm=1024, tn=1024, name="nat_qkv_p")
            o = _attention(qkv, 0, qkv, 1, qkv, 2, None, None, nb=np_, seq_len=lp, hq=NAT_HEADS,
                           hkv=NAT_HEADS, tq=lp, tk=lp, q_scale=scale, name="nat_attn_p")
            xp = _proj_residual(o, xp, mods_p, w_out, rows_per_group=rpg_p, tm=512,
                                name="nat_out_p")
            nk_out.append(qkv[:, wd:2 * wd].reshape(np_, lp, NAT_HEADS, HEAD_DIM))
            nv_out.append(qkv[:, 2 * wd:].reshape(np_, lp, NAT_HEADS, HEAD_DIM))
            qkv = _mod_matmul(xs, mods_s, norm_mix[i], w_qkv, rows_per_group=rpg_s,
                              out_dtype=BF16, tm=1024, tn=1024, name="nat_qkv_s")
            past = cache_nat_k.shape[2]
            ck = cache_nat_k[:, j].reshape(ns_, past, wd).astype(BF16)
            cv = cache_nat_v[:, j].reshape(ns_, past, wd).astype(BF16)
            o = _nat_latent(qkv, ck, cv, nat_rpb[j], nb=ns_, seq_len=ls)
            xs = _proj_residual(o, xs, mods_s, w_out, rows_per_group=rpg_s, tm=512,
                                name="nat_out_s")
        last = i == DEPTH - 1
        w1 = mlp_w1[i].astype(BF16)
        w2 = mlp_w2[i].astype(BF16)
        xp = _mlp(xp, mods_p, norm_mlp[i], w1, w2, norm_final, rows_per_group=rpg_p,
                  final_norm=last, name="mlp_p")
        xs = _mlp(xs, mods_s, norm_mlp[i], w1, w2, norm_final, rows_per_group=rpg_s,
                  final_norm=last, name="mlp_s")

    return (xp.reshape(np_, lp, d), xs.reshape(ns_, ls, d),
            jnp.stack(sf_out, axis=1), jnp.stack(sb_out, axis=1),
            jnp.stack(ak_out, axis=1), jnp.stack(av_out, axis=1),
            jnp.stack(nk_out, axis=1), jnp.stack(nv_out, axis=1))
```

```python
import functools
import math

import jax
import jax.numpy as jnp
from jax import lax
from jax.experimental import pallas as pl
from jax.experimental.pallas import tpu as pltpu

F32 = jnp.float32
BF16 = jnp.bfloat16

D_MODEL = 1024
DEPTH = 4
GRID_W = 64
N_MIXERS = 3
EPS = 1e-6
NEG_INF = -1e30
HEAD_DIM = 64
LANES = 128
LOG2E = math.log2(math.e)
SSD_D_INNER = 2 * D_MODEL
SSD_HEADS = SSD_D_INNER // HEAD_DIM
SSD_GROUPS = 4
SSD_HPG = SSD_HEADS // SSD_GROUPS
SSD_STATE = 128
SSD_CONV_W = 5
SSD_CHUNK = 128
SSD_CONV_DIM = SSD_D_INNER + 2 * SSD_GROUPS * SSD_STATE
SSD_ZX_DIM = SSD_D_INNER + SSD_CONV_DIM
ATTN_HEADS = D_MODEL // HEAD_DIM
ATTN_KV_HEADS = 4
ROPE_BASE = 10000.0
NAT_HEADS = D_MODEL // HEAD_DIM
NAT_KH = 8
NAT_KW = 16
MLP_HIDDEN = 4 * D_MODEL

VMEM_LIMIT_BYTES = 48 * 1024 * 1024
NAT_VMEM_LIMIT_BYTES = 56 * 1024 * 1024


def _params(semantics, vmem=VMEM_LIMIT_BYTES):
    return pltpu.CompilerParams(dimension_semantics=semantics, vmem_limit_bytes=vmem)


def _sigmoid(x):
    return 1.0 / (1.0 + jnp.exp(-x))


def _modulated_norm(x, gain, shift, scale):
    ms = jnp.mean(x * x, axis=-1, keepdims=True)
    return (x * lax.rsqrt(ms + EPS) * gain) * (1.0 + scale) + shift


def _ada_kernel(c_ref, w_ref, b_ref, o_ref):
    c = c_ref[...]
    a = (c * _sigmoid(c)).astype(BF16)
    o_ref[...] = jnp.dot(a, w_ref[...].astype(BF16), preferred_element_type=F32) + b_ref[...]


def _ada_mods(cond, ada_w, ada_b):
    depth, d, n = ada_w.shape
    r = cond.shape[0]
    tn = 1024
    return pl.pallas_call(
        _ada_kernel,
        out_shape=jax.ShapeDtypeStruct((depth, r, n), F32),
        grid=(depth, n // tn),
        in_specs=[pl.BlockSpec((r, d), lambda l, j: (0, 0)),
                  pl.BlockSpec((None, d, tn), lambda l, j: (l, 0, j)),
                  pl.BlockSpec((None, 1, tn), lambda l, j: (l, 0, j))],
        out_specs=pl.BlockSpec((None, r, tn), lambda l, j: (l, 0, j)),
        compiler_params=_params(("parallel", "parallel")),
        name="ada_mods",
    )(cond, ada_w, ada_b.reshape(depth, 1, n))


def _modmm_kernel(x_ref, mod_ref, g_ref, w_ref, o_ref, h_ref, *, shift_row, scale_row):
    @pl.when(pl.program_id(1) == 0)
    def _():
        h = _modulated_norm(x_ref[...], g_ref[...],
                            mod_ref[shift_row:shift_row + 1, :],
                            mod_ref[scale_row:scale_row + 1, :])
        h_ref[...] = h.astype(BF16)

    o_ref[...] = jnp.dot(h_ref[...], w_ref[...],
                         preferred_element_type=F32).astype(o_ref.dtype)


def _mod_matmul(x, mods, gain, w, *, rows_per_group, out_dtype, tm, tn, name):
    m, d = x.shape
    n = w.shape[1]
    tpg = rows_per_group // tm
    return pl.pallas_call(
        functools.partial(_modmm_kernel, shift_row=0, scale_row=1),
        out_shape=jax.ShapeDtypeStruct((m, n), out_dtype),
        grid=(m // tm, n // tn),
        in_specs=[pl.BlockSpec((tm, d), lambda i, j: (i, 0)),
                  pl.BlockSpec((None, 6, d), lambda i, j: (i // tpg, 0, 0)),
                  pl.BlockSpec((1, d), lambda i, j: (0, 0)),
                  pl.BlockSpec((d, tn), lambda i, j: (0, j))],
        out_specs=pl.BlockSpec((tm, tn), lambda i, j: (i, j)),
        scratch_shapes=[pltpu.VMEM((tm, d), BF16)],
        compiler_params=_params(("parallel", "arbitrary")),
        name=name,
    )(x, mods, gain.reshape(1, d), w)


def _proj_res_kernel(a_ref, x_ref, mod_ref, w_ref, o_ref, *, gate_row):
    y = jnp.dot(a_ref[...], w_ref[...], preferred_element_type=F32)
    o_ref[...] = x_ref[...] + mod_ref[gate_row:gate_row + 1, :] * y


def _proj_residual(a, x, mods, w, *, rows_per_group, tm, name):
    m, k = a.shape
    d = x.shape[1]
    tpg = rows_per_group // tm
    return pl.pallas_call(
        functools.partial(_proj_res_kernel, gate_row=2),
        out_shape=jax.ShapeDtypeStruct((m, d), F32),
        grid=(m // tm,),
        in_specs=[pl.BlockSpec((tm, k), lambda i: (i, 0)),
                  pl.BlockSpec((tm, d), lambda i: (i, 0)),
                  pl.BlockSpec((None, 6, d), lambda i: (i // tpg, 0, 0)),
                  pl.BlockSpec((k, d), lambda i: (0, 0))],
        out_specs=pl.BlockSpec((tm, d), lambda i: (i, 0)),
        compiler_params=_params(("parallel",)),
        name=name,
    )(a, x, mods, w)


def _mlp_kernel(x_ref, mod_ref, g_ref, w1_ref, w2_ref, gf_ref, o_ref, h_ref, acc_ref,
                *, final_norm):
    j = pl.program_id(1)

    @pl.when(j == 0)
    def _():
        h = _modulated_norm(x_ref[...], g_ref[...], mod_ref[3:4, :], mod_ref[4:5, :])
        h_ref[...] = h.astype(BF16)
        acc_ref[...] = jnp.zeros_like(acc_ref)

    a = jnp.dot(h_ref[...], w1_ref[...], preferred_element_type=F32)
    a = jnp.maximum(a, 0.0)
    acc_ref[...] += jnp.dot((a * a).astype(BF16), w2_ref[...], preferred_element_type=F32)

    @pl.when(j == pl.num_programs(1) - 1)
    def _():
        y = x_ref[...] + mod_ref[5:6, :] * acc_ref[...]
        if final_norm:
            ms = jnp.mean(y * y, axis=-1, keepdims=True)
            y = y * lax.rsqrt(ms + EPS) * gf_ref[...]
        o_ref[...] = y


def _mlp(x, mods, gain, w1, w2, gain_final, *, rows_per_group, final_norm, name):
    m, d = x.shape
    hdim = w1.shape[1]
    tm, th = 1024, 512
    tpg = rows_per_group // tm
    return pl.pallas_call(
        functools.partial(_mlp_kernel, final_norm=final_norm),
        out_shape=jax.ShapeDtypeStruct((m, d), F32),
        grid=(m // tm, hdim // th),
        in_specs=[pl.BlockSpec((tm, d), lambda i, j: (i, 0)),
                  pl.BlockSpec((None, 6, d), lambda i, j: (i // tpg, 0, 0)),
                  pl.BlockSpec((1, d), lambda i, j: (0, 0)),
                  pl.BlockSpec((d, th), lambda i, j: (0, j)),
                  pl.BlockSpec((th, d), lambda i, j: (j, 0)),
                  pl.BlockSpec((1, d), lambda i, j: (0, 0))],
        out_specs=pl.BlockSpec((tm, d), lambda i, j: (i, 0)),
        scratch_shapes=[pltpu.VMEM((tm, d), BF16), pltpu.VMEM((tm, d), F32)],
        compiler_params=_params(("parallel", "arbitrary")),
        name=name,
    )(x, mods, gain.reshape(1, d), w1, w2, gain_final.reshape(1, d))


CONV_HALO_ROWS = 16


def _conv_kernel(prev_ref, cur_ref, next_ref, w_ref, b_ref, o_ref, *, tiles_per_seq):
    pos = pl.program_id(0) % tiles_per_seq
    cur = cur_ref[...].astype(F32)
    tr = cur.shape[0]
    pv = jnp.where(pos == 0, 0.0, prev_ref[...].astype(F32))
    nx = jnp.where(pos == tiles_per_seq - 1, 0.0, next_ref[...].astype(F32))
    row = lax.broadcasted_iota(jnp.int32, cur.shape, 0)
    h = CONV_HALO_ROWS
    xm1 = jnp.where(row == 0, pv[h - 1:h], pltpu.roll(cur, 1, 0))
    xm2 = jnp.where(row == 0, pv[h - 2:h - 1],
                    jnp.where(row == 1, pv[h - 1:h], pltpu.roll(cur, 2, 0)))
    xp1 = jnp.where(row == tr - 1, nx[0:1], pltpu.roll(cur, tr - 1, 0))
    xp2 = jnp.where(row == tr - 2, nx[0:1],
                    jnp.where(row == tr - 1, nx[1:2], pltpu.roll(cur, tr - 2, 0)))
    w = w_ref[...]
    y = (w[0:1] * xm2 + w[1:2] * xm1 + w[2:3] * cur + w[3:4] * xp1 + w[4:5] * xp2
         + b_ref[...])
    o_ref[...] = (y * _sigmoid(y)).astype(o_ref.dtype)


def _conv_silu(zx, conv_w, conv_b, *, seq_len):
    m = zx.shape[0]
    tr, tc = 256, 512
    c0 = SSD_D_INNER // tc
    halo_per_tile = tr // CONV_HALO_ROWS
    n_halo = m // CONV_HALO_ROWS
    return pl.pallas_call(
        functools.partial(_conv_kernel, tiles_per_seq=seq_len // tr),
        out_shape=jax.ShapeDtypeStruct((m, SSD_CONV_DIM), BF16),
        grid=(m // tr, SSD_CONV_DIM // tc),
        in_specs=[
            pl.BlockSpec((CONV_HALO_ROWS, tc),
                         lambda i, j: (jnp.maximum(i * halo_per_tile - 1, 0), c0 + j)),
            pl.BlockSpec((tr, tc), lambda i, j: (i, c0 + j)),
            pl.BlockSpec((CONV_HALO_ROWS, tc),
                         lambda i, j: (jnp.minimum((i + 1) * halo_per_tile, n_halo - 1), c0 + j)),
            pl.BlockSpec((SSD_CONV_W, tc), lambda i, j: (0, j)),
            pl.BlockSpec((1, tc), lambda i, j: (0, j))],
        out_specs=pl.BlockSpec((tr, tc), lambda i, j: (i, j)),
        compiler_params=_params(("parallel", "parallel")),
        name="ssd_conv_silu",
    )(zx, zx, zx, conv_w, conv_b.reshape(1, SSD_CONV_DIM))


def _split3(a):
    a1 = a.astype(BF16)
    r1 = a - a1.astype(F32)
    a2 = r1.astype(BF16)
    a3 = (r1 - a2.astype(F32)).astype(BF16)
    return a1, a2, a3


def _dot_exact_left(m01, a):
    return sum(jnp.dot(m01, p, preferred_element_type=F32) for p in _split3(a))


def _dot_exact_right(a, m01):
    return sum(jnp.dot(p, m01, preferred_element_type=F32) for p in _split3(a))


def _ssd_kernel(*refs, reverse, has_h0, emit_state, combine):
    it = iter(refs)
    xbc_ref, dt_ref, dtb_ref, alog_ref, e_ref = (next(it) for _ in range(5))
    h0_ref = next(it) if has_h0 else None
    if combine:
        z_ref, yf_ref, ng_ref = next(it), next(it), next(it)
    else:
        drep_ref = next(it)
    y_ref = next(it)
    st_ref = next(it) if emit_state else None
    s_ref = next(it)
    ybuf_ref = next(it) if combine else y_ref

    q = SSD_CHUNK
    gw = SSD_HPG * HEAD_DIM
    c = pl.program_id(1)
    col0 = SSD_HEADS if reverse else 0

    @pl.when(c == 0)
    def _():
        if has_h0:
            for k in range(SSD_D_INNER // q):
                s_ref[:, k * q:(k + 1) * q] = h0_ref[k * q:(k + 1) * q, :].T
        else:
            s_ref[...] = jnp.zeros_like(s_ref)

    pre = dt_ref[...] + dtb_ref[...]
    dt = jnp.maximum(pre, 0.0) + jnp.log1p(jnp.exp(-jnp.abs(pre)))
    a_dt = dt * (-jnp.exp(alog_ref[...]))
    ri = lax.broadcasted_iota(jnp.int32, (q, q), 0)
    ci = lax.broadcasted_iota(jnp.int32, (q, q), 1)
    keep = (ri <= ci) if reverse else (ri >= ci)
    cs = _dot_exact_left(jnp.where(keep, 1.0, 0.0).astype(BF16), a_dt)
    cs_t = cs.T
    dt_t = dt.T
    e01 = e_ref[...]
    cs_rep = _dot_exact_right(cs, e01)
    dt_rep = _dot_exact_right(dt, e01)
    edge = cs_rep[0:1, :] if reverse else cs_rep[q - 1:q, :]
    exp_cs = jnp.exp(cs_rep)
    to_end = jnp.exp(edge - cs_rep) * dt_rep
    exp_edge = jnp.exp(edge)
    lane = lax.broadcasted_iota(jnp.int32, (q, LANES), 1)

    for g in range(SSD_GROUPS):
        gs = slice(g * gw, (g + 1) * gw)
        b_g = xbc_ref[:, SSD_D_INNER + g * SSD_STATE:SSD_D_INNER + (g + 1) * SSD_STATE]
        c_off = SSD_D_INNER + SSD_GROUPS * SSD_STATE
        c_g = xbc_ref[:, c_off + g * SSD_STATE:c_off + (g + 1) * SSD_STATE]
        cb = lax.dot_general(c_g, b_g, (((1,), (1,)), ((), ())), preferred_element_type=F32)
        s_g = s_ref[:, gs]
        y_off = jnp.dot(c_g, s_g.astype(BF16), preferred_element_type=F32) * exp_cs[:, gs]
        for mp in range(gw // LANES):
            m = g * (gw // LANES) + mp
            ls = slice(m * LANES, (m + 1) * LANES)
            x_pair = xbc_ref[:, ls]
            ys = []
            for h in (2 * m, 2 * m + 1):
                k = col0 + h
                seg = cs[:, k:k + 1] - cs_t[k:k + 1, :]
                dec = jnp.exp(jnp.where(keep, seg, -jnp.inf))
                w = (dec * cb * dt_t[k:k + 1, :]).astype(BF16)
                ys.append(jnp.dot(w, x_pair, preferred_element_type=F32))
            y_pair = jnp.where(lane < HEAD_DIM, ys[0], ys[1]) + y_off[:, mp * LANES:(mp + 1) * LANES]
            if not combine:
                y_pair = y_pair + drep_ref[:, ls] * x_pair.astype(F32)
            ybuf_ref[:, ls] = y_pair
        xs_g = (to_end[:, gs] * xbc_ref[:, gs].astype(F32)).astype(BF16)
        b_t = b_g.astype(F32).T.astype(BF16)
        s_ref[:, gs] = s_g * exp_edge[:, gs] + jnp.dot(b_t, xs_g, preferred_element_type=F32)

    if combine:
        z = z_ref[...].astype(F32)
        gated = (ybuf_ref[...] + yf_ref[...]) * (z * _sigmoid(z))
        ms = jnp.mean(gated * gated, axis=-1, keepdims=True)
        y_ref[...] = (gated * lax.rsqrt(ms + EPS) * ng_ref[...]).astype(y_ref.dtype)

    if emit_state:
        @pl.when(c == pl.num_programs(1) - 1)
        def _():
            for k in range(SSD_D_INNER // q):
                st_ref[k * q:(k + 1) * q, :] = s_ref[:, k * q:(k + 1) * q].T


def _ssd_scan(xbc, dt_raw, dt_bias, a_log, expand, *, nseq, seq_len, reverse, h0=None,
              layer=0, emit_state=False, zx=None, y_fwd=None, norm_g=None, d_rep=None, name):
    combine = zx is not None
    nchunk = seq_len // SSD_CHUNK
    q = SSD_CHUNK

    def row_blk(b, c):
        return b * nchunk + ((nchunk - 1 - c) if reverse else c)

    in_specs = [pl.BlockSpec((q, SSD_CONV_DIM), lambda b, c: (row_blk(b, c), 0)),
                pl.BlockSpec((q, LANES), lambda b, c: (row_blk(b, c), 0)),
                pl.BlockSpec((1, LANES), lambda b, c: (0, 0)),
                pl.BlockSpec((1, LANES), lambda b, c: (0, 0)),
                pl.BlockSpec((LANES, SSD_D_INNER), lambda b, c: (0, 0))]
    args = [xbc, dt_raw, dt_bias, a_log, expand]
    if h0 is not None:
        in_specs.append(pl.BlockSpec((None, None, SSD_D_INNER, SSD_STATE),
                                     lambda b, c: (b, layer, 0, 0)))
        args.append(h0)
    if combine:
        in_specs += [pl.BlockSpec((q, SSD_D_INNER), lambda b, c: (row_blk(b, c), 0)),
                     pl.BlockSpec((q, SSD_D_INNER), lambda b, c: (row_blk(b, c), 0)),
                     pl.BlockSpec((1, SSD_D_INNER), lambda b, c: (0, 0))]
        args += [zx, y_fwd, norm_g]
    else:
        in_specs.append(pl.BlockSpec((1, SSD_D_INNER), lambda b, c: (0, 0)))
        args.append(d_rep)
    m = nseq * seq_len
    out_shape = [jax.ShapeDtypeStruct((m, SSD_D_INNER), BF16 if combine else F32)]
    out_specs = [pl.BlockSpec((q, SSD_D_INNER), lambda b, c: (row_blk(b, c), 0))]
    if emit_state:
        out_shape.append(jax.ShapeDtypeStruct((nseq, SSD_D_INNER, SSD_STATE), F32))
        out_specs.append(pl.BlockSpec((None, SSD_D_INNER, SSD_STATE), lambda b, c: (b, 0, 0)))
    scratch = [pltpu.VMEM((SSD_STATE, SSD_D_INNER), F32)]
    if combine:
        scratch.append(pltpu.VMEM((q, SSD_D_INNER), F32))
    outs = pl.pallas_call(
        functools.partial(_ssd_kernel, reverse=reverse, has_h0=h0 is not None,
                          emit_state=emit_state, combine=combine),
        out_shape=out_shape,
        grid=(nseq, nchunk),
        in_specs=in_specs,
        out_specs=out_specs,
        scratch_shapes=scratch,
        compiler_params=_params(("parallel", "arbitrary")),
        name=name,
    )(*args)
    return outs


def _head_rms(x, gain_row):
    rows, width = x.shape
    lane = lax.broadcasted_iota(jnp.int32, (rows, LANES), 1)
    lo = lane < HEAD_DIM
    outs = []
    for j in range(width // LANES):
        xb = x[:, j * LANES:(j + 1) * LANES]
        sq = xb * xb
        s_lo = jnp.sum(jnp.where(lo, sq, 0.0), axis=-1, keepdims=True)
        s_hi = jnp.sum(jnp.where(lo, 0.0, sq), axis=-1, keepdims=True)
        ms = jnp.where(lo, s_lo, s_hi) * (1.0 / HEAD_DIM)
        outs.append(xb * lax.rsqrt(ms + EPS))
    return jnp.concatenate(outs, axis=1) * gain_row


def _rope(x, cos, sin_signed):
    rows, width = x.shape
    lane = lax.broadcasted_iota(jnp.int32, (rows, LANES), 1)
    even = (lane % 2) == 0
    outs = []
    for j in range(width // LANES):
        xb = x[:, j * LANES:(j + 1) * LANES]
        partner = jnp.where(even, pltpu.roll(xb, LANES - 1, 1), pltpu.roll(xb, 1, 1))
        outs.append(xb * cos + partner * sin_signed)
    return jnp.concatenate(outs, axis=1)


def _qknorm_kernel(*refs, rope, emit_cache):
    it = iter(refs)
    q_ref, k_ref, v_ref, qg_ref, kg_ref = (next(it) for _ in range(5))
    if rope:
        cos_ref, sin_ref = next(it), next(it)
    qo_ref, ko_ref, vo_ref = next(it), next(it), next(it)
    q = _head_rms(q_ref[...], qg_ref[...])
    k = _head_rms(k_ref[...], kg_ref[...])
    if emit_cache:
        kc_ref, vc_ref = next(it), next(it)
        kc_ref[...] = k
        vc_ref[...] = v_ref[...]
    if rope:
        q = _rope(q, cos_ref[...], sin_ref[...])
        k = _rope(k, cos_ref[...], sin_ref[...])
    qo_ref[...] = (q * (HEAD_DIM ** -0.5 * LOG2E)).astype(BF16)
    ko_ref[...] = k.astype(BF16)
    vo_ref[...] = v_ref[...].astype(BF16)


def _rope_tables(seq_len):
    n_pairs = HEAD_DIM // 4
    t = jnp.arange(seq_len)
    row = (t // GRID_W).astype(F32)
    col = (t % GRID_W).astype(F32)
    freqs = ROPE_BASE ** (-jnp.arange(n_pairs, dtype=F32) / n_pairs)
    ang = jnp.concatenate([row[:, None] * freqs, col[:, None] * freqs], axis=-1)
    cos = jnp.repeat(jnp.cos(ang), 2, axis=-1)
    sin = jnp.repeat(jnp.sin(ang), 2, axis=-1)
    sign = jnp.where(jnp.arange(HEAD_DIM) % 2 == 0, -1.0, 1.0).astype(F32)
    return jnp.tile(cos, (1, 2)), jnp.tile(sin * sign, (1, 2))


def _qk_norm(qkv, q_gain, k_gain, *, seq_len, rope, emit_cache):
    m = qkv.shape[0]
    tm = 256
    nq = ATTN_HEADS * HEAD_DIM
    nk = ATTN_KV_HEADS * HEAD_DIM
    qg = jnp.tile(q_gain, ATTN_HEADS).reshape(1, nq)
    kg = jnp.tile(k_gain, ATTN_KV_HEADS).reshape(1, nk)
    in_specs = [pl.BlockSpec((tm, nq), lambda i: (i, 0)),
                pl.BlockSpec((tm, nk), lambda i: (i, nq // nk)),
                pl.BlockSpec((tm, nk), lambda i: (i, nq // nk + 1)),
                pl.BlockSpec((1, nq), lambda i: (0, 0)),
                pl.BlockSpec((1, nk), lambda i: (0, 0))]
    args = [qkv, qkv, qkv, qg, kg]
    if rope:
        cos, sin = _rope_tables(seq_len)
        tps = seq_len // tm
        in_specs += [pl.BlockSpec((tm, LANES), lambda i: (i % tps, 0)),
                     pl.BlockSpec((tm, LANES), lambda i: (i % tps, 0))]
        args += [cos, sin]
    out_shape = [jax.ShapeDtypeStruct((m, nq), BF16), jax.ShapeDtypeStruct((m, nk), BF16),
                 jax.ShapeDtypeStruct((m, nk), BF16)]
    out_specs = [pl.BlockSpec((tm, nq), lambda i: (i, 0)), pl.BlockSpec((tm, nk), lambda i: (i, 0)),
                 pl.BlockSpec((tm, nk), lambda i: (i, 0))]
    if emit_cache:
        out_shape += [jax.ShapeDtypeStruct((m, nk), F32)] * 2
        out_specs += [pl.BlockSpec((tm, nk), lambda i: (i, 0))] * 2
    return pl.pallas_call(
        functools.partial(_qknorm_kernel, rope=rope, emit_cache=emit_cache),
        out_shape=out_shape,
        grid=(m // tm,),
        in_specs=in_specs,
        out_specs=out_specs,
        compiler_params=_params(("parallel",)),
        name="gqa_qk_norm_rope" if rope else "gqa_qk_norm",
    )(*args)


def _place_half(x, lane, src_half, dst_half):
    if src_half != dst_half:
        x = pltpu.roll(x, HEAD_DIM, 1)
    keep = (lane >= HEAD_DIM) if dst_half else (lane < HEAD_DIM)
    return jnp.where(keep, x, 0.0)


def _attn_kernel(*refs, hq, hkv, tk, has_ctx, q_scale):
    if has_ctx:
        q_ref, k_ref, v_ref, ck_ref, cv_ref, o_ref = refs
    else:
        q_ref, k_ref, v_ref, o_ref = refs
    grp = hq // hkv
    tq = q_ref.shape[0]
    rows = grp * tq
    lane = lax.broadcasted_iota(jnp.int32, (tq, LANES), 1)
    nt = (((1,), (1,)), ((), ()))

    qst = []
    for g in range(hkv):
        parts = []
        for u in range(grp):
            h = g * grp + u
            qb = q_ref[:, (h // 2) * LANES:(h // 2 + 1) * LANES].astype(F32)
            if q_scale != 1.0:
                qb = qb * q_scale
            parts.append(_place_half(qb, lane, h % 2, g % 2).astype(BF16))
        qst.append(parts[0] if grp == 1 else jnp.concatenate(parts, axis=0))

    def step(g, kblk, vblk, carry):
        m_i, l_i, acc = carry
        s = lax.dot_general(qst[g], kblk.astype(BF16), nt, preferred_element_type=F32)
        m_n = jnp.maximum(m_i, jnp.max(s, axis=-1, keepdims=True))
        alpha = jnp.exp2(m_i - m_n)
        p = jnp.exp2(s - m_n)
        l_n = alpha * l_i + jnp.sum(p, axis=-1, keepdims=True)
        acc = alpha * acc + jnp.dot(p.astype(BF16), vblk.astype(BF16),
                                    preferred_element_type=F32)
        return m_n, l_n, acc

    def all_heads(kv_of_pair, carries):
        out = []
        for g in range(hkv):
            kblk, vblk = kv_of_pair(slice((g // 2) * LANES, (g // 2 + 1) * LANES))
            out.append(step(g, kblk, vblk, carries[g]))
        return tuple(out)

    carries = tuple((jnp.full((rows, 1), -jnp.inf, F32), jnp.zeros((rows, 1), F32),
                     jnp.zeros((rows, LANES), F32)) for _ in range(hkv))
    n_self = k_ref.shape[0] // tk
    if n_self == 1:
        carries = all_heads(lambda kc: (k_ref[:, kc], v_ref[:, kc]), carries)
    else:
        def body(ci, car):
            r0 = pl.multiple_of(ci * tk, tk)
            return all_heads(lambda kc: (k_ref[pl.ds(r0, tk), kc], v_ref[pl.ds(r0, tk), kc]), car)
        carries = lax.fori_loop(0, n_self, body, carries, unroll=2)
    if has_ctx:
        for ci in range(ck_ref.shape[0] // tk):
            rs = slice(ci * tk, (ci + 1) * tk)
            carries = all_heads(lambda kc: (ck_ref[rs, kc], cv_ref[rs, kc]), carries)

    head_out = [None] * hq
    for g in range(hkv):
        _, l_f, acc = carries[g]
        o = acc / l_f
        for u in range(grp):
            head_out[g * grp + u] = o[u * tq:(u + 1) * tq]
    for mpair in range(hq // 2):
        oa, ob = head_out[2 * mpair], head_out[2 * mpair + 1]
        if ((2 * mpair) // grp) % 2 != 0:
            oa = pltpu.roll(oa, HEAD_DIM, 1)
        if ((2 * mpair + 1) // grp) % 2 != 1:
            ob = pltpu.roll(ob, HEAD_DIM, 1)
        o_ref[:, mpair * LANES:(mpair + 1) * LANES] = jnp.where(
            lane < HEAD_DIM, oa, ob).astype(o_ref.dtype)


def _attention(q_arr, q_col, k_arr, k_col, v_arr, v_col, ctx_k, ctx_v, *, nb, seq_len,
               hq, hkv, tq, tk, q_scale, name):
    wq, wk = hq * HEAD_DIM, hkv * HEAD_DIM
    qt = seq_len // tq
    in_specs = [pl.BlockSpec((tq, wq), lambda b, i: (b * qt + i, q_col)),
                pl.BlockSpec((seq_len, wk), lambda b, i: (b, k_col)),
                pl.BlockSpec((seq_len, wk), lambda b, i: (b, v_col))]
    args = [q_arr, k_arr, v_arr]
    has_ctx = ctx_k is not None
    if has_ctx:
        lc = ctx_k.shape[1]
        in_specs += [pl.BlockSpec((None, lc, wk), lambda b, i: (b, 0, 0))] * 2
        args += [ctx_k, ctx_v]
    return pl.pallas_call(
        functools.partial(_attn_kernel, hq=hq, hkv=hkv, tk=tk, has_ctx=has_ctx, q_scale=q_scale),
        out_shape=jax.ShapeDtypeStruct((nb * seq_len, wq), BF16),
        grid=(nb, qt),
        in_specs=in_specs,
        out_specs=pl.BlockSpec((tq, wq), lambda b, i: (b * qt + i, 0)),
        compiler_params=_params(("parallel", "arbitrary")),
        name=name,
    )(*args)


NAT_ROWS_PER_STEP = 2


def _nat_kernel(q_ref, k_ref, v_ref, ck_ref, cv_ref, *rest, rows):
    bias_refs, o_ref = rest[:NAT_ROWS_PER_STEP], rest[NAT_ROWS_PER_STEP]
    kh, w = NAT_KH, GRID_W
    nloc = kh * w
    pair_rows = 2 * w
    lane = lax.broadcasted_iota(jnp.int32, (w, LANES), 1)
    nt = (((1,), (1,)), ((), ()))
    q_scale = (HEAD_DIM ** -0.5) * LOG2E
    for mpair in range(NAT_HEADS // 2):
        ls = slice(mpair * LANES, (mpair + 1) * LANES)
        k_ctx = ck_ref[:, ls]
        v_ctx = cv_ref[:, ls]
        qs = []
        for j in range(NAT_ROWS_PER_STEP):
            qb = q_ref[j * w:(j + 1) * w, ls].astype(F32) * q_scale
            qs.append(jnp.concatenate([_place_half(qb, lane, 0, 0), _place_half(qb, lane, 1, 1)],
                                      axis=0).astype(BF16))
        s_ctx_all = lax.dot_general(jnp.concatenate(qs, axis=0), k_ctx, nt,
                                    preferred_element_type=F32)
        p_ctx, o_loc, denom = [], [], []
        for j in range(NAT_ROWS_PER_STEP):
            r = pl.program_id(1) * NAT_ROWS_PER_STEP + j
            k0 = pl.multiple_of(jnp.clip(r - kh // 2, 0, rows - kh) * w, w)
            s_loc = (lax.dot_general(qs[j], k_ref[pl.ds(k0, nloc), ls], nt,
                                     preferred_element_type=F32) + bias_refs[j][mpair])
            s_ctx = s_ctx_all[j * pair_rows:(j + 1) * pair_rows]
            mx = jnp.maximum(jnp.max(s_loc, axis=-1, keepdims=True),
                             jnp.max(s_ctx, axis=-1, keepdims=True))
            p_loc = jnp.exp2(s_loc - mx)
            pc = jnp.exp2(s_ctx - mx)
            denom.append(jnp.sum(p_loc, axis=-1, keepdims=True)
                         + jnp.sum(pc, axis=-1, keepdims=True))
            o_loc.append(jnp.dot(p_loc.astype(BF16), v_ref[pl.ds(k0, nloc), ls],
                                 preferred_element_type=F32))
            p_ctx.append(pc.astype(BF16))
        o_ctx_all = jnp.dot(jnp.concatenate(p_ctx, axis=0), v_ctx, preferred_element_type=F32)
        for j in range(NAT_ROWS_PER_STEP):
            o = (o_loc[j] + o_ctx_all[j * pair_rows:(j + 1) * pair_rows]) / denom[j]
            o_ref[j * w:(j + 1) * w, ls] = jnp.where(lane < HEAD_DIM, o[:w], o[w:]).astype(o_ref.dtype)


def _nat_bias(rpb):
    col = jnp.arange(GRID_W)
    cstart = jnp.clip(col - NAT_KW // 2, 0, GRID_W - NAT_KW)
    col_ok = (col[None, :] >= cstart[:, None]) & (col[None, :] < cstart[:, None] + NAT_KW)
    dc_idx = jnp.clip(col[None, :] - col[:, None] + NAT_KW - 1, 0, 2 * NAT_KW - 2)
    rpb_c = jnp.where(col_ok[None, None], rpb[:, :, dc_idx] * LOG2E, NEG_INF)
    variants = []
    for d0 in range(NAT_KH):
        blk = rpb_c[:, d0:d0 + NAT_KH].transpose(0, 2, 1, 3)
        variants.append(blk.reshape(NAT_HEADS // 2, 2 * GRID_W, NAT_KH * GRID_W))
    return jnp.stack(variants, axis=0)


def _nat_latent(qkv, ctx_k, ctx_v, rpb, *, nb, seq_len):
    rows = seq_len // GRID_W
    kh = NAT_KH
    rb = NAT_ROWS_PER_STEP
    bias = _nat_bias(rpb)
    lc = ctx_k.shape[1]
    wd = NAT_HEADS * HEAD_DIM
    steps = rows // rb

    def bias_spec(j):
        def idx(b, i):
            r = i * rb + j
            return (jnp.clip(r - kh // 2, 0, rows - kh) - r + kh - 1, 0, 0, 0)
        return pl.BlockSpec((None, NAT_HEADS // 2, 2 * GRID_W, kh * GRID_W), idx)

    return pl.pallas_call(
        functools.partial(_nat_kernel, rows=rows),
        out_shape=jax.ShapeDtypeStruct((nb * seq_len, wd), BF16),
        grid=(nb, steps),
        in_specs=[pl.BlockSpec((rb * GRID_W, wd), lambda b, i: (b * steps + i, 0)),
                  pl.BlockSpec((seq_len, wd), lambda b, i: (b, 1)),
                  pl.BlockSpec((seq_len, wd), lambda b, i: (b, 2)),
                  pl.BlockSpec((None, lc, wd), lambda b, i: (b, 0, 0)),
                  pl.BlockSpec((None, lc, wd), lambda b, i: (b, 0, 0))]
                 + [bias_spec(j) for j in range(rb)],
        out_specs=pl.BlockSpec((rb * GRID_W, wd), lambda b, i: (b * steps + i, 0)),
        compiler_params=_params(("parallel", "arbitrary"), NAT_VMEM_LIMIT_BYTES),
        name="nat_latent",
    )(qkv, qkv, qkv, ctx_k, ctx_v, *([bias] * rb))


def _ssd_layer(xp, xs, mods_p, mods_s, gain, j, np_, ns_, lp, ls, state_f, state_b,
               ssd_w_in, ssd_conv_w, ssd_conv_b, ssd_dt_bias, ssd_a_log, ssd_d, ssd_norm,
               ssd_w_out):
    w_zx = ssd_w_in[j, :, :SSD_ZX_DIM].astype(BF16)
    w_dt = jnp.pad(ssd_w_in[j, :, SSD_ZX_DIM:], ((0, 0), (0, LANES - 2 * SSD_HEADS))).astype(BF16)
    w_out = ssd_w_out[j].astype(BF16)
    pad = LANES - 2 * SSD_HEADS
    dt_bias = jnp.pad(ssd_dt_bias[j].reshape(-1), (0, pad)).reshape(1, LANES)
    a_log = jnp.pad(ssd_a_log[j].reshape(-1), (0, pad)).reshape(1, LANES)
    d_rep = jnp.repeat(ssd_d[j], HEAD_DIM).reshape(1, SSD_D_INNER)
    norm_g = ssd_norm[j].reshape(1, SSD_D_INNER)
    head_of_col = jnp.arange(SSD_D_INNER) // HEAD_DIM
    expand = [(jnp.arange(LANES)[:, None] == head_of_col[None, :] + off).astype(BF16)
              for off in (0, SSD_HEADS)]

    outs = []
    states = None
    for (x, mods, nseq, seq_len, is_prompt) in ((xp, mods_p, np_, lp, True),
                                                (xs, mods_s, ns_, ls, False)):
        tag = "p" if is_prompt else "s"
        zx = _mod_matmul(x, mods, gain, w_zx, rows_per_group=seq_len if not is_prompt else x.shape[0],
                         out_dtype=BF16, tm=1024, tn=1024, name="ssd_in_zx_" + tag)
        dt_raw = _mod_matmul(x, mods, gain, w_dt,
                             rows_per_group=seq_len if not is_prompt else x.shape[0],
                             out_dtype=F32, tm=1024, tn=LANES, name="ssd_in_dt_" + tag)
        xbc = _conv_silu(zx, ssd_conv_w[j], ssd_conv_b[j], seq_len=seq_len)
        common = dict(nseq=nseq, seq_len=seq_len, layer=j)
        fwd = _ssd_scan(xbc, dt_raw, dt_bias, a_log, expand[0], reverse=False,
                        h0=None if is_prompt else state_f, emit_state=is_prompt, d_rep=d_rep,
                        name="ssd_scan_fwd_" + tag, **common)
        bwd = _ssd_scan(xbc, dt_raw, dt_bias, a_log, expand[1], reverse=True,
                        h0=None if is_prompt else state_b, emit_state=is_prompt, zx=zx,
                        y_fwd=fwd[0], norm_g=norm_g, name="ssd_scan_bwd_" + tag, **common)
        if is_prompt:
            states = (fwd[1], bwd[1])
        outs.append(_proj_residual(bwd[0], x, mods,
                                   w_out, rows_per_group=seq_len if not is_prompt else x.shape[0],
                                   tm=512, name="ssd_out_" + tag))
    return outs[0], outs[1], states


def kernel(x_prompt, x_sample, c, state_ssd_fwd, state_ssd_bwd, cache_attn_k, cache_attn_v, cache_nat_k, cache_nat_v, c_ctx, ada_w, ada_b, norm_mix, norm_mlp, mlp_w1, mlp_w2, ssd_w_in, ssd_conv_w, ssd_conv_b, ssd_dt_bias, ssd_a_log, ssd_d, ssd_norm, ssd_w_out, attn_w_qkv, attn_q_norm, attn_k_norm, attn_w_out, nat_w_qkv, nat_rpb, nat_w_out, norm_final):
    np_, lp, d = x_prompt.shape
    ns_, ls, _ = x_sample.shape
    mp_rows, ms_rows = np_ * lp, ns_ * ls
    xp = x_prompt.reshape(mp_rows, d)
    xs = x_sample.reshape(ms_rows, d)

    cond_rows = 16
    cond = jnp.concatenate([c_ctx[None, :], c, jnp.zeros((cond_rows - 1 - ns_, d), F32)], axis=0)
    mods_all = _ada_mods(cond, ada_w, ada_b).reshape(DEPTH, cond_rows, 6, d)

    n_ssd = state_ssd_fwd.shape[1]
    st_f = state_ssd_fwd.reshape(ns_, n_ssd, SSD_D_INNER, SSD_STATE)
    st_b = state_ssd_bwd.reshape(ns_, n_ssd, SSD_D_INNER, SSD_STATE)

    sf_out, sb_out, ak_out, av_out, nk_out, nv_out = [], [], [], [], [], []
    for i in range(DEPTH):
        kind, j = i % N_MIXERS, i // N_MIXERS
        mods_p = mods_all[i, 0:1]
        mods_s = mods_all[i, 1:1 + ns_]
        rpg_p, rpg_s = mp_rows, ls
        if kind == 0:
            xp, xs, (hf, hb) = _ssd_layer(
                xp, xs, mods_p, mods_s, norm_mix[i], j, np_, ns_, lp, ls, st_f, st_b,
                ssd_w_in, ssd_conv_w, ssd_conv_b, ssd_dt_bias, ssd_a_log, ssd_d, ssd_norm,
                ssd_w_out)
            sf_out.append(hf.reshape(np_, SSD_HEADS, HEAD_DIM, SSD_STATE))
            sb_out.append(hb.reshape(np_, SSD_HEADS, HEAD_DIM, SSD_STATE))
        elif kind == 1:
            w_qkv = attn_w_qkv[j].astype(BF16)
            w_out = attn_w_out[j].astype(BF16)
            nk = ATTN_KV_HEADS * HEAD_DIM
            qkv = _mod_matmul(xp, mods_p, norm_mix[i], w_qkv, rows_per_group=rpg_p,
                              out_dtype=F32, tm=1024, tn=768, name="gqa_qkv_p")
            q, k, v, kc, vc = _qk_norm(qkv, attn_q_norm[j], attn_k_norm[j], seq_len=lp,
                                       rope=False, emit_cache=True)
            o = _attention(q, 0, k, 0, v, 0, None, None, nb=np_, seq_len=lp, hq=ATTN_HEADS,
                           hkv=ATTN_KV_HEADS, tq=128, tk=lp, q_scale=1.0, name="gqa_attn_p")
            xp = _proj_residual(o, xp, mods_p, w_out, rows_per_group=rpg_p, tm=512,
                                name="gqa_out_p")
            ak_out.append(kc.reshape(np_, lp, ATTN_KV_HEADS, HEAD_DIM))
            av_out.append(vc.reshape(np_, lp, ATTN_KV_HEADS, HEAD_DIM))
            qkv = _mod_matmul(xs, mods_s, norm_mix[i], w_qkv, rows_per_group=rpg_s,
                              out_dtype=F32, tm=1024, tn=768, name="gqa_qkv_s")
            q, k, v = _qk_norm(qkv, attn_q_norm[j], attn_k_norm[j], seq_len=ls, rope=True,
                               emit_cache=False)
            past = cache_attn_k.shape[2]
            ck = cache_attn_k[:, j].reshape(ns_, past, nk).astype(BF16)
            cv = cache_attn_v[:, j].reshape(ns_, past, nk).astype(BF16)
            o = _attention(q, 0, k, 0, v, 0, ck, cv, nb=ns_, seq_len=ls, hq=ATTN_HEADS,
                           hkv=ATTN_KV_HEADS, tq=128, tk=512, q_scale=1.0, name="gqa_attn_s")
            xs = _proj_residual(o, xs, mods_s, w_out, rows_per_group=rpg_s, tm=512,
                                name="gqa_out_s")
        else:
            w_qkv = nat_w_qkv[j].astype(BF16)
            w_out = nat_w_out[j].astype(BF16)
            wd = NAT_HEADS * HEAD_DIM
            scale = HEAD_DIM ** -0.5 * LOG2E
            qkv = _mod_matmul(xp, mods_p, norm_mix[i], w_qkv, rows_per_group=rpg_p,
                              out_dtype=F32, tm=1024, tn=1024, name="nat_qkv_p")
            o = _attention(qkv, 0, qkv, 1, qkv, 2, None, None, nb=np_, seq_len=lp, hq=NAT_HEADS,
                           hkv=NAT_HEADS, tq=lp, tk=lp, q_scale=scale, name="nat_attn_p")
            xp = _proj_residual(o, xp, mods_p, w_out, rows_per_group=rpg_p, tm=512,
                                name="nat_out_p")
            nk_out.append(qkv[:, wd:2 * wd].reshape(np_, lp, NAT_HEADS, HEAD_DIM))
            nv_out.append(qkv[:, 2 * wd:].reshape(np_, lp, NAT_HEADS, HEAD_DIM))
            qkv = _mod_matmul(xs, mods_s, norm_mix[i], w_qkv, rows_per_group=rpg_s,
                              out_dtype=BF16, tm=1024, tn=1024, name="nat_qkv_s")
            past = cache_nat_k.shape[2]
            ck = cache_nat_k[:, j].reshape(ns_, past, wd).astype(BF16)
            cv = cache_nat_v[:, j].reshape(ns_, past, wd).astype(BF16)
            o = _nat_latent(qkv, ck, cv, nat_rpb[j], nb=ns_, seq_len=ls)
            xs = _proj_residual(o, xs, mods_s, w_out, rows_per_group=rpg_s, tm=512,
                                name="nat_out_s")
        last = i == DEPTH - 1
        w1 = mlp_w1[i].astype(BF16)
        w2 = mlp_w2[i].astype(BF16)
        xp = _mlp(xp, mods_p, norm_mlp[i], w1, w2, norm_final, rows_per_group=rpg_p,
                  final_norm=last, name="mlp_p")
        xs = _mlp(xs, mods_s, norm_mlp[i], w1, w2, norm_final, rows_per_group=rpg_s,
                  final_norm=last, name="mlp_s")

    return (xp.reshape(np_, lp, d), xs.reshape(ns_, ls, d),
            jnp.stack(sf_out, axis=1), jnp.stack(sb_out, axis=1),
            jnp.stack(ak_out, axis=1), jnp.stack(av_out, axis=1),
            jnp.stack(nk_out, axis=1), jnp.stack(nv_out, axis=1))
```

```python
import functools
import math

import jax
import jax.numpy as jnp
from jax import lax
from jax.experimental import pallas as pl
from jax.experimental.pallas import tpu as pltpu

F32 = jnp.float32
BF16 = jnp.bfloat16

D_MODEL = 1024
DEPTH = 4
GRID_W = 64
N_MIXERS = 3
EPS = 1e-6
NEG_INF = -1e30
HEAD_DIM = 64
LANES = 128
LOG2E = math.log2(math.e)
SSD_D_INNER = 2 * D_MODEL
SSD_HEADS = SSD_D_INNER // HEAD_DIM
SSD_GROUPS = 4
SSD_HPG = SSD_HEADS // SSD_GROUPS
SSD_STATE = 128
SSD_CONV_W = 5
SSD_CHUNK = 128
SSD_CONV_DIM = SSD_D_INNER + 2 * SSD_GROUPS * SSD_STATE
SSD_ZX_DIM = SSD_D_INNER + SSD_CONV_DIM
ATTN_HEADS = D_MODEL // HEAD_DIM
ATTN_KV_HEADS = 4
ROPE_BASE = 10000.0
NAT_HEADS = D_MODEL // HEAD_DIM
NAT_KH = 8
NAT_KW = 16
MLP_HIDDEN = 4 * D_MODEL

VMEM_LIMIT_BYTES = 48 * 1024 * 1024
NAT_VMEM_LIMIT_BYTES = 56 * 1024 * 1024


def _params(semantics, vmem=VMEM_LIMIT_BYTES):
    return pltpu.CompilerParams(dimension_semantics=semantics, vmem_limit_bytes=vmem)


def _sigmoid(x):
    return 1.0 / (1.0 + jnp.exp(-x))


def _modulated_norm(x, gain, shift, scale):
    ms = jnp.mean(x * x, axis=-1, keepdims=True)
    return (x * lax.rsqrt(ms + EPS) * gain) * (1.0 + scale) + shift


def _ada_kernel(c_ref, w_ref, b_ref, o_ref):
    c = c_ref[...]
    a = (c * _sigmoid(c)).astype(BF16)
    o_ref[...] = jnp.dot(a, w_ref[...].astype(BF16), preferred_element_type=F32) + b_ref[...]


def _ada_mods(cond, ada_w, ada_b):
    depth, d, n = ada_w.shape
    r = cond.shape[0]
    tn = 1024
    return pl.pallas_call(
        _ada_kernel,
        out_shape=jax.ShapeDtypeStruct((depth, r, n), F32),
        grid=(depth, n // tn),
        in_specs=[pl.BlockSpec((r, d), lambda l, j: (0, 0)),
                  pl.BlockSpec((None, d, tn), lambda l, j: (l, 0, j)),
                  pl.BlockSpec((None, 1, tn), lambda l, j: (l, 0, j))],
        out_specs=pl.BlockSpec((None, r, tn), lambda l, j: (l, 0, j)),
        compiler_params=_params(("parallel", "parallel")),
        name="ada_mods",
    )(cond, ada_w, ada_b.reshape(depth, 1, n))


def _modmm_kernel(x_ref, mod_ref, g_ref, w_ref, o_ref, h_ref, *, shift_row, scale_row):
    @pl.when(pl.program_id(1) == 0)
    def _():
        h = _modulated_norm(x_ref[...], g_ref[...],
                            mod_ref[shift_row:shift_row + 1, :],
                            mod_ref[scale_row:scale_row + 1, :])
        h_ref[...] = h.astype(BF16)

    o_ref[...] = jnp.dot(h_ref[...], w_ref[...],
                         preferred_element_type=F32).astype(o_ref.dtype)


def _mod_matmul(x, mods, gain, w, *, rows_per_group, out_dtype, tm, tn, name):
    m, d = x.shape
    n = w.shape[1]
    tpg = rows_per_group // tm
    return pl.pallas_call(
        functools.partial(_modmm_kernel, shift_row=0, scale_row=1),
        out_shape=jax.ShapeDtypeStruct((m, n), out_dtype),
        grid=(m // tm, n // tn),
        in_specs=[pl.BlockSpec((tm, d), lambda i, j: (i, 0)),
                  pl.BlockSpec((None, 6, d), lambda i, j: (i // tpg, 0, 0)),
                  pl.BlockSpec((1, d), lambda i, j: (0, 0)),
                  pl.BlockSpec((d, tn), lambda i, j: (0, j))],
        out_specs=pl.BlockSpec((tm, tn), lambda i, j: (i, j)),
        scratch_shapes=[pltpu.VMEM((tm, d), BF16)],
        compiler_params=_params(("parallel", "arbitrary")),
        name=name,
    )(x, mods, gain.reshape(1, d), w)


def _proj_res_kernel(a_ref, x_ref, mod_ref, w_ref, o_ref, *, gate_row):
    y = jnp.dot(a_ref[...], w_ref[...], preferred_element_type=F32)
    o_ref[...] = x_ref[...] + mod_ref[gate_row:gate_row + 1, :] * y


def _proj_residual(a, x, mods, w, *, rows_per_group, tm, name):
    m, k = a.shape
    d = x.shape[1]
    tpg = rows_per_group // tm
    return pl.pallas_call(
        functools.partial(_proj_res_kernel, gate_row=2),
        out_shape=jax.ShapeDtypeStruct((m, d), F32),
        grid=(m // tm,),
        in_specs=[pl.BlockSpec((tm, k), lambda i: (i, 0)),
                  pl.BlockSpec((tm, d), lambda i: (i, 0)),
                  pl.BlockSpec((None, 6, d), lambda i: (i // tpg, 0, 0)),
                  pl.BlockSpec((k, d), lambda i: (0, 0))],
        out_specs=pl.BlockSpec((tm, d), lambda i: (i, 0)),
        compiler_params=_params(("parallel",)),
        name=name,
    )(a, x, mods, w)


def _mlp_kernel(x_ref, mod_ref, g_ref, w1_ref, w2_ref, gf_ref, o_ref, h_ref, acc_ref,
                *, final_norm):
    j = pl.program_id(1)

    @pl.when(j == 0)
    def _():
        h = _modulated_norm(x_ref[...], g_ref[...], mod_ref[3:4, :], mod_ref[4:5, :])
        h_ref[...] = h.astype(BF16)
        acc_ref[...] = jnp.zeros_like(acc_ref)

    a = jnp.dot(h_ref[...], w1_ref[...], preferred_element_type=F32)
    a = jnp.maximum(a, 0.0)
    acc_ref[...] += jnp.dot((a * a).astype(BF16), w2_ref[...], preferred_element_type=F32)

    @pl.when(j == pl.num_programs(1) - 1)
    def _():
        y = x_ref[...] + mod_ref[5:6, :] * acc_ref[...]
        if final_norm:
            ms = jnp.mean(y * y, axis=-1, keepdims=True)
            y = y * lax.rsqrt(ms + EPS) * gf_ref[...]
        o_ref[...] = y


def _mlp(x, mods, gain, w1, w2, gain_final, *, rows_per_group, final_norm, name):
    m, d = x.shape
    hdim = w1.shape[1]
    tm, th = 1024, 512
    tpg = rows_per_group // tm
    return pl.pallas_call(
        functools.partial(_mlp_kernel, final_norm=final_norm),
        out_shape=jax.ShapeDtypeStruct((m, d), F32),
        grid=(m // tm, hdim // th),
        in_specs=[pl.BlockSpec((tm, d), lambda i, j: (i, 0)),
                  pl.BlockSpec((None, 6, d), lambda i, j: (i // tpg, 0, 0)),
                  pl.BlockSpec((1, d), lambda i, j: (0, 0)),
                  pl.BlockSpec((d, th), lambda i, j: (0, j)),
                  pl.BlockSpec((th, d), lambda i, j: (j, 0)),
                  pl.BlockSpec((1, d), lambda i, j: (0, 0))],
        out_specs=pl.BlockSpec((tm, d), lambda i, j: (i, 0)),
        scratch_shapes=[pltpu.VMEM((tm, d), BF16), pltpu.VMEM((tm, d), F32)],
        compiler_params=_params(("parallel", "arbitrary")),
        name=name,
    )(x, mods, gain.reshape(1, d), w1, w2, gain_final.reshape(1, d))


CONV_HALO_ROWS = 16


CONV_TAPS_SHIFTED = (0, 1, 3, 4)


def _conv_kernel(prev_ref, cur_ref, next_ref, sh_ref, w_ref, b_ref, o_ref, *, tiles_per_seq):
    pos = pl.program_id(0) % tiles_per_seq
    cur_b = cur_ref[...]
    cur = cur_b.astype(F32)
    tr = cur.shape[0]
    h = CONV_HALO_ROWS
    w = w_ref[...]
    bias = b_ref[...]

    def silu(y):
        return (y / (1.0 + jnp.exp2(y * (-LOG2E)))).astype(o_ref.dtype)

    y = bias + w[2:3] * cur
    for idx, k in enumerate(CONV_TAPS_SHIFTED):
        y = y + w[k:k + 1] * jnp.dot(sh_ref[idx], cur_b, preferred_element_type=F32)
    o_ref[...] = silu(y)

    def edge(x):
        ye = bias + w[2:3] * x
        for k in CONV_TAPS_SHIFTED:
            ye = ye + w[k:k + 1] * pltpu.roll(x, (SSD_CONV_W // 2 - k) % x.shape[0], 0)
        return silu(ye)[h:2 * h]

    pv = jnp.where(pos == 0, 0.0, prev_ref[...].astype(F32))
    nx = jnp.where(pos == tiles_per_seq - 1, 0.0, next_ref[...].astype(F32))
    o_ref[0:h, :] = edge(jnp.concatenate([pv, cur[0:2 * h]], axis=0))
    o_ref[tr - h:tr, :] = edge(jnp.concatenate([cur[tr - 2 * h:tr], nx], axis=0))


def _conv_silu(zx, conv_w, conv_b, *, seq_len):
    m = zx.shape[0]
    tr, tc = 256, 1024
    c0 = SSD_D_INNER // tc
    halo_per_tile = tr // CONV_HALO_ROWS
    n_halo = m // CONV_HALO_ROWS
    t = jnp.arange(tr)
    shifts = jnp.stack([(t[None, :] == t[:, None] + (k - SSD_CONV_W // 2)).astype(BF16)
                        for k in CONV_TAPS_SHIFTED])
    return pl.pallas_call(
        functools.partial(_conv_kernel, tiles_per_seq=seq_len // tr),
        out_shape=jax.ShapeDtypeStruct((m, SSD_CONV_DIM), BF16),
        grid=(m // tr, SSD_CONV_DIM // tc),
        in_specs=[
            pl.BlockSpec((CONV_HALO_ROWS, tc),
                         lambda i, j: (jnp.maximum(i * halo_per_tile - 1, 0), c0 + j)),
            pl.BlockSpec((tr, tc), lambda i, j: (i, c0 + j)),
            pl.BlockSpec((CONV_HALO_ROWS, tc),
                         lambda i, j: (jnp.minimum((i + 1) * halo_per_tile, n_halo - 1), c0 + j)),
            pl.BlockSpec((len(CONV_TAPS_SHIFTED), tr, tr), lambda i, j: (0, 0, 0)),
            pl.BlockSpec((SSD_CONV_W, tc), lambda i, j: (0, j)),
            pl.BlockSpec((1, tc), lambda i, j: (0, j))],
        out_specs=pl.BlockSpec((tr, tc), lambda i, j: (i, j)),
        compiler_params=_params(("parallel", "parallel")),
        name="ssd_conv_silu",
    )(zx, zx, zx, shifts, conv_w, conv_b.reshape(1, SSD_CONV_DIM))


def _split3(a):
    a1 = a.astype(BF16)
    r1 = a - a1.astype(F32)
    a2 = r1.astype(BF16)
    a3 = (r1 - a2.astype(F32)).astype(BF16)
    return a1, a2, a3


def _dot_exact_left(m01, a):
    return sum(jnp.dot(m01, p, preferred_element_type=F32) for p in _split3(a))


def _dot_exact_right(a, m01):
    return sum(jnp.dot(p, m01, preferred_element_type=F32) for p in _split3(a))


def _spread_heads(a, e2_ref):
    hi = a.astype(BF16)
    lo = (a - hi.astype(F32)).astype(BF16)
    return jnp.dot(jnp.concatenate([hi, lo], axis=1), e2_ref[...], preferred_element_type=F32)


def _ssd_kernel(*refs, reverse, has_h0, emit_state, combine):
    it = iter(refs)
    xbc_ref, dt_ref, dtb_ref, alog_ref, e_ref = (next(it) for _ in range(5))
    h0_ref = next(it) if has_h0 else None
    if combine:
        z_ref, yf_ref, ng_ref = next(it), next(it), next(it)
    else:
        drep_ref = next(it)
    y_ref = next(it)
    st_ref = next(it) if emit_state else None
    s_ref = next(it)
    ybuf_ref = next(it) if combine else y_ref

    q = SSD_CHUNK
    gw = SSD_HPG * HEAD_DIM
    c = pl.program_id(1)
    col0 = SSD_HEADS if reverse else 0

    @pl.when(c == 0)
    def _():
        if has_h0:
            for k in range(SSD_D_INNER // q):
                s_ref[:, k * q:(k + 1) * q] = h0_ref[k * q:(k + 1) * q, :].T
        else:
            s_ref[...] = jnp.zeros_like(s_ref)

    pre = dt_ref[...] + dtb_ref[...]
    dt = jnp.maximum(pre, 0.0) + jnp.log1p(jnp.exp(-jnp.abs(pre)))
    a_dt = dt * (-jnp.exp(alog_ref[...]) * LOG2E)
    ri = lax.broadcasted_iota(jnp.int32, (q, q), 0)
    ci = lax.broadcasted_iota(jnp.int32, (q, q), 1)
    keep = (ri <= ci) if reverse else (ri >= ci)
    cs = _dot_exact_left(jnp.where(keep, 1.0, 0.0).astype(BF16), a_dt)
    cs_t = cs.T
    dt_t = dt.T
    edge = cs[0:1, :] if reverse else cs[q - 1:q, :]
    exp_cs = _spread_heads(jnp.exp2(cs), e_ref)
    to_end = _spread_heads(jnp.exp2(edge - cs) * dt, e_ref)
    edge_rep = _dot_exact_right(jnp.broadcast_to(edge, (8, LANES)), e_ref[0:LANES, :])[0:1, :]
    exp_edge = jnp.exp2(edge_rep)
    lane = lax.broadcasted_iota(jnp.int32, (q, LANES), 1)
    lo = lane < HEAD_DIM

    for g in range(SSD_GROUPS):
        gs = slice(g * gw, (g + 1) * gw)
        b_g = xbc_ref[:, SSD_D_INNER + g * SSD_STATE:SSD_D_INNER + (g + 1) * SSD_STATE]
        c_off = SSD_D_INNER + SSD_GROUPS * SSD_STATE
        c_g = xbc_ref[:, c_off + g * SSD_STATE:c_off + (g + 1) * SSD_STATE]
        cb = lax.dot_general(c_g, b_g, (((1,), (1,)), ((), ())), preferred_element_type=F32)
        cb = jnp.where(keep, cb, 0.0)
        s_g = s_ref[:, gs]
        y_off = jnp.dot(c_g, s_g.astype(BF16), preferred_element_type=F32)
        for mp in range(gw // LANES):
            m = g * (gw // LANES) + mp
            ls = slice(m * LANES, (m + 1) * LANES)
            x_pair = xbc_ref[:, ls]
            k0 = col0 + 2 * m
            ys = []
            for k in (k0, k0 + 1):
                seg = jnp.minimum(cs[:, k:k + 1] - cs_t[k:k + 1, :], 0.0)
                w = (jnp.exp2(seg) * cb * dt_t[k:k + 1, :]).astype(BF16)
                ys.append(jnp.dot(w, x_pair, preferred_element_type=F32))
            y_pair = (jnp.where(lo, ys[0], ys[1])
                      + y_off[:, mp * LANES:(mp + 1) * LANES] * exp_cs[:, ls])
            if not combine:
                y_pair = y_pair + drep_ref[:, ls] * x_pair.astype(F32)
            ybuf_ref[:, ls] = y_pair
        xs_g = (to_end[:, gs] * xbc_ref[:, gs].astype(F32)).astype(BF16)
        b_t = b_g.astype(F32).T.astype(BF16)
        s_ref[:, gs] = s_g * exp_edge[:, gs] + jnp.dot(b_t, xs_g, preferred_element_type=F32)

    if combine:
        z = z_ref[...].astype(F32)
        gated = (ybuf_ref[...] + yf_ref[...]) * (z * _sigmoid(z))
        ms = jnp.mean(gated * gated, axis=-1, keepdims=True)
        y_ref[...] = (gated * lax.rsqrt(ms + EPS) * ng_ref[...]).astype(y_ref.dtype)

    if emit_state:
        @pl.when(c == pl.num_programs(1) - 1)
        def _():
            for k in range(SSD_D_INNER // q):
                st_ref[k * q:(k + 1) * q, :] = s_ref[:, k * q:(k + 1) * q].T


def _ssd_scan(xbc, dt_raw, dt_bias, a_log, expand, *, nseq, seq_len, reverse, h0=None,
              layer=0, emit_state=False, zx=None, y_fwd=None, norm_g=None, d_rep=None, name):
    combine = zx is not None
    nchunk = seq_len // SSD_CHUNK
    q = SSD_CHUNK

    def row_blk(b, c):
        return b * nchunk + ((nchunk - 1 - c) if reverse else c)

    in_specs = [pl.BlockSpec((q, SSD_CONV_DIM), lambda b, c: (row_blk(b, c), 0)),
                pl.BlockSpec((q, LANES), lambda b, c: (row_blk(b, c), 0)),
                pl.BlockSpec((1, LANES), lambda b, c: (0, 0)),
                pl.BlockSpec((1, LANES), lambda b, c: (0, 0)),
                pl.BlockSpec((2 * LANES, SSD_D_INNER), lambda b, c: (0, 0))]
    args = [xbc, dt_raw, dt_bias, a_log, expand]
    if h0 is not None:
        in_specs.append(pl.BlockSpec((None, None, SSD_D_INNER, SSD_STATE),
                                     lambda b, c: (b, layer, 0, 0)))
        args.append(h0)
    if combine:
        in_specs += [pl.BlockSpec((q, SSD_D_INNER), lambda b, c: (row_blk(b, c), 0)),
                     pl.BlockSpec((q, SSD_D_INNER), lambda b, c: (row_blk(b, c), 0)),
                     pl.BlockSpec((1, SSD_D_INNER), lambda b, c: (0, 0))]
        args += [zx, y_fwd, norm_g]
    else:
        in_specs.append(pl.BlockSpec((1, SSD_D_INNER), lambda b, c: (0, 0)))
        args.append(d_rep)
    m = nseq * seq_len
    out_shape = [jax.ShapeDtypeStruct((m, SSD_D_INNER), BF16 if combine else F32)]
    out_specs = [pl.BlockSpec((q, SSD_D_INNER), lambda b, c: (row_blk(b, c), 0))]
    if emit_state:
        out_shape.append(jax.ShapeDtypeStruct((nseq, SSD_D_INNER, SSD_STATE), F32))
        out_specs.append(pl.BlockSpec((None, SSD_D_INNER, SSD_STATE), lambda b, c: (b, 0, 0)))
    scratch = [pltpu.VMEM((SSD_STATE, SSD_D_INNER), F32)]
    if combine:
        scratch.append(pltpu.VMEM((q, SSD_D_INNER), F32))
    outs = pl.pallas_call(
        functools.partial(_ssd_kernel, reverse=reverse, has_h0=h0 is not None,
                          emit_state=emit_state, combine=combine),
        out_shape=out_shape,
        grid=(nseq, nchunk),
        in_specs=in_specs,
        out_specs=out_specs,
        scratch_shapes=scratch,
        compiler_params=_params(("parallel", "arbitrary")),
        name=name,
    )(*args)
    return outs


def _head_rms(x, gain_row):
    rows, width = x.shape
    lane = lax.broadcasted_iota(jnp.int32, (rows, LANES), 1)
    lo = lane < HEAD_DIM
    outs = []
    for j in range(width // LANES):
        xb = x[:, j * LANES:(j + 1) * LANES]
        sq = xb * xb
        s_lo = jnp.sum(jnp.where(lo, sq, 0.0), axis=-1, keepdims=True)
        s_hi = jnp.sum(jnp.where(lo, 0.0, sq), axis=-1, keepdims=True)
        ms = jnp.where(lo, s_lo, s_hi) * (1.0 / HEAD_DIM)
        outs.append(xb * lax.rsqrt(ms + EPS))
    return jnp.concatenate(outs, axis=1) * gain_row


def _rope(x, cos, sin_signed):
    rows, width = x.shape
    lane = lax.broadcasted_iota(jnp.int32, (rows, LANES), 1)
    even = (lane % 2) == 0
    outs = []
    for j in range(width // LANES):
        xb = x[:, j * LANES:(j + 1) * LANES]
        partner = jnp.where(even, pltpu.roll(xb, LANES - 1, 1), pltpu.roll(xb, 1, 1))
        outs.append(xb * cos + partner * sin_signed)
    return jnp.concatenate(outs, axis=1)


def _qknorm_kernel(*refs, rope, emit_cache):
    it = iter(refs)
    q_ref, k_ref, v_ref, qg_ref, kg_ref = (next(it) for _ in range(5))
    if rope:
        cos_ref, sin_ref = next(it), next(it)
    qo_ref, ko_ref, vo_ref = next(it), next(it), next(it)
    q = _head_rms(q_ref[...], qg_ref[...])
    k = _head_rms(k_ref[...], kg_ref[...])
    if emit_cache:
        kc_ref, vc_ref = next(it), next(it)
        kc_ref[...] = k
        vc_ref[...] = v_ref[...]
    if rope:
        q = _rope(q, cos_ref[...], sin_ref[...])
        k = _rope(k, cos_ref[...], sin_ref[...])
    qo_ref[...] = (q * (HEAD_DIM ** -0.5 * LOG2E)).astype(BF16)
    ko_ref[...] = k.astype(BF16)
    vo_ref[...] = v_ref[...].astype(BF16)


def _rope_tables(seq_len):
    n_pairs = HEAD_DIM // 4
    t = jnp.arange(seq_len)
    row = (t // GRID_W).astype(F32)
    col = (t % GRID_W).astype(F32)
    freqs = ROPE_BASE ** (-jnp.arange(n_pairs, dtype=F32) / n_pairs)
    ang = jnp.concatenate([row[:, None] * freqs, col[:, None] * freqs], axis=-1)
    cos = jnp.repeat(jnp.cos(ang), 2, axis=-1)
    sin = jnp.repeat(jnp.sin(ang), 2, axis=-1)
    sign = jnp.where(jnp.arange(HEAD_DIM) % 2 == 0, -1.0, 1.0).astype(F32)
    return jnp.tile(cos, (1, 2)), jnp.tile(sin * sign, (1, 2))


def _qk_norm(qkv, q_gain, k_gain, *, seq_len, rope, emit_cache):
    m = qkv.shape[0]
    tm = 256
    nq = ATTN_HEADS * HEAD_DIM
    nk = ATTN_KV_HEADS * HEAD_DIM
    qg = jnp.tile(q_gain, ATTN_HEADS).reshape(1, nq)
    kg = jnp.tile(k_gain, ATTN_KV_HEADS).reshape(1, nk)
    in_specs = [pl.BlockSpec((tm, nq), lambda i: (i, 0)),
                pl.BlockSpec((tm, nk), lambda i: (i, nq // nk)),
                pl.BlockSpec((tm, nk), lambda i: (i, nq // nk + 1)),
                pl.BlockSpec((1, nq), lambda i: (0, 0)),
                pl.BlockSpec((1, nk), lambda i: (0, 0))]
    args = [qkv, qkv, qkv, qg, kg]
    if rope:
        cos, sin = _rope_tables(seq_len)
        tps = seq_len // tm
        in_specs += [pl.BlockSpec((tm, LANES), lambda i: (i % tps, 0)),
                     pl.BlockSpec((tm, LANES), lambda i: (i % tps, 0))]
        args += [cos, sin]
    out_shape = [jax.ShapeDtypeStruct((m, nq), BF16), jax.ShapeDtypeStruct((m, nk), BF16),
                 jax.ShapeDtypeStruct((m, nk), BF16)]
    out_specs = [pl.BlockSpec((tm, nq), lambda i: (i, 0)), pl.BlockSpec((tm, nk), lambda i: (i, 0)),
                 pl.BlockSpec((tm, nk), lambda i: (i, 0))]
    if emit_cache:
        out_shape += [jax.ShapeDtypeStruct((m, nk), F32)] * 2
        out_specs += [pl.BlockSpec((tm, nk), lambda i: (i, 0))] * 2
    return pl.pallas_call(
        functools.partial(_qknorm_kernel, rope=rope, emit_cache=emit_cache),
        out_shape=out_shape,
        grid=(m // tm,),
        in_specs=in_specs,
        out_specs=out_specs,
        compiler_params=_params(("parallel",)),
        name="gqa_qk_norm_rope" if rope else "gqa_qk_norm",
    )(*args)


def _place_half(x, lane, src_half, dst_half):
    if src_half != dst_half:
        x = pltpu.roll(x, HEAD_DIM, 1)
    keep = (lane >= HEAD_DIM) if dst_half else (lane < HEAD_DIM)
    return jnp.where(keep, x, 0.0)


def _attn_kernel(*refs, hq, hkv, tk, has_ctx, q_scale):
    if has_ctx:
        q_ref, k_ref, v_ref, ck_ref, cv_ref, o_ref = refs
    else:
        q_ref, k_ref, v_ref, o_ref = refs
    grp = hq // hkv
    tq = q_ref.shape[0]
    rows = grp * tq
    lane = lax.broadcasted_iota(jnp.int32, (tq, LANES), 1)
    nt = (((1,), (1,)), ((), ()))

    qst = []
    for g in range(hkv):
        parts = []
        for u in range(grp):
            h = g * grp + u
            qb = q_ref[:, (h // 2) * LANES:(h // 2 + 1) * LANES].astype(F32)
            if q_scale != 1.0:
                qb = qb * q_scale
            parts.append(_place_half(qb, lane, h % 2, g % 2).astype(BF16))
        qst.append(parts[0] if grp == 1 else jnp.concatenate(parts, axis=0))

    def step(g, kblk, vblk, carry):
        m_i, l_i, acc = carry
        s = lax.dot_general(qst[g], kblk.astype(BF16), nt, preferred_element_type=F32)
        m_n = jnp.maximum(m_i, jnp.max(s, axis=-1, keepdims=True))
        alpha = jnp.exp2(m_i - m_n)
        p = jnp.exp2(s - m_n)
        l_n = alpha * l_i + jnp.sum(p, axis=-1, keepdims=True)
        acc = alpha * acc + jnp.dot(p.astype(BF16), vblk.astype(BF16),
                                    preferred_element_type=F32)
        return m_n, l_n, acc

    def all_heads(kv_of_pair, carries):
        out = []
        for g in range(hkv):
            kblk, vblk = kv_of_pair(slice((g // 2) * LANES, (g // 2 + 1) * LANES))
            out.append(step(g, kblk, vblk, carries[g]))
        return tuple(out)

    carries = tuple((jnp.full((rows, 1), -jnp.inf, F32), jnp.zeros((rows, 1), F32),
                     jnp.zeros((rows, LANES), F32)) for _ in range(hkv))
    n_self = k_ref.shape[0] // tk
    if n_self == 1:
        carries = all_heads(lambda kc: (k_ref[:, kc], v_ref[:, kc]), carries)
    else:
        def body(ci, car):
            r0 = pl.multiple_of(ci * tk, tk)
            return all_heads(lambda kc: (k_ref[pl.ds(r0, tk), kc], v_ref[pl.ds(r0, tk), kc]), car)
        carries = lax.fori_loop(0, n_self, body, carries, unroll=2)
    if has_ctx:
        carries = all_heads(lambda kc: (ck_ref[:, kc], cv_ref[:, kc]), carries)

    head_out = [None] * hq
    for g in range(hkv):
        _, l_f, acc = carries[g]
        o = acc / l_f
        for u in range(grp):
            head_out[g * grp + u] = o[u * tq:(u + 1) * tq]
    for mpair in range(hq // 2):
        oa, ob = head_out[2 * mpair], head_out[2 * mpair + 1]
        if ((2 * mpair) // grp) % 2 != 0:
            oa = pltpu.roll(oa, HEAD_DIM, 1)
        if ((2 * mpair + 1) // grp) % 2 != 1:
            ob = pltpu.roll(ob, HEAD_DIM, 1)
        o_ref[:, mpair * LANES:(mpair + 1) * LANES] = jnp.where(
            lane < HEAD_DIM, oa, ob).astype(o_ref.dtype)


def _attention(q_arr, q_col, k_arr, k_col, v_arr, v_col, ctx_k, ctx_v, *, nb, seq_len,
               hq, hkv, tq, tk, q_scale, name):
    wq, wk = hq * HEAD_DIM, hkv * HEAD_DIM
    qt = seq_len // tq
    in_specs = [pl.BlockSpec((tq, wq), lambda b, i: (b * qt + i, q_col)),
                pl.BlockSpec((seq_len, wk), lambda b, i: (b, k_col)),
                pl.BlockSpec((seq_len, wk), lambda b, i: (b, v_col))]
    args = [q_arr, k_arr, v_arr]
    has_ctx = ctx_k is not None
    if has_ctx:
        lc = ctx_k.shape[1]
        in_specs += [pl.BlockSpec((None, lc, wk), lambda b, i: (b, 0, 0))] * 2
        args += [ctx_k, ctx_v]
    return pl.pallas_call(
        functools.partial(_attn_kernel, hq=hq, hkv=hkv, tk=tk, has_ctx=has_ctx, q_scale=q_scale),
        out_shape=jax.ShapeDtypeStruct((nb * seq_len, wq), BF16),
        grid=(nb, qt),
        in_specs=in_specs,
        out_specs=pl.BlockSpec((tq, wq), lambda b, i: (b * qt + i, 0)),
        compiler_params=_params(("parallel", "arbitrary")),
        name=name,
    )(*args)


NAT_ROWS_PER_STEP = 2


def _nat_kernel(q_ref, k_ref, v_ref, ck_ref, cv_ref, *rest, rows):
    bias_refs, o_ref = rest[:NAT_ROWS_PER_STEP], rest[NAT_ROWS_PER_STEP]
    kh, w = NAT_KH, GRID_W
    nloc = kh * w
    pair_rows = 2 * w
    lane = lax.broadcasted_iota(jnp.int32, (w, LANES), 1)
    nt = (((1,), (1,)), ((), ()))
    q_scale = (HEAD_DIM ** -0.5) * LOG2E
    for mpair in range(NAT_HEADS // 2):
        ls = slice(mpair * LANES, (mpair + 1) * LANES)
        k_ctx = ck_ref[:, ls]
        v_ctx = cv_ref[:, ls]
        qs = []
        for j in range(NAT_ROWS_PER_STEP):
            qb = q_ref[j * w:(j + 1) * w, ls].astype(F32) * q_scale
            qs.append(jnp.concatenate([_place_half(qb, lane, 0, 0), _place_half(qb, lane, 1, 1)],
                                      axis=0).astype(BF16))
        s_ctx_all = lax.dot_general(jnp.concatenate(qs, axis=0), k_ctx, nt,
                                    preferred_element_type=F32)
        p_ctx, o_loc, denom = [], [], []
        for j in range(NAT_ROWS_PER_STEP):
            r = pl.program_id(1) * NAT_ROWS_PER_STEP + j
            k0 = pl.multiple_of(jnp.clip(r - kh // 2, 0, rows - kh) * w, w)
            s_loc = (lax.dot_general(qs[j], k_ref[pl.ds(k0, nloc), ls], nt,
                                     preferred_element_type=F32) + bias_refs[j][mpair])
            s_ctx = s_ctx_all[j * pair_rows:(j + 1) * pair_rows]
            mx = jnp.maximum(jnp.max(s_loc, axis=-1, keepdims=True),
                             jnp.max(s_ctx, axis=-1, keepdims=True))
            p_loc = jnp.exp2(s_loc - mx)
            pc = jnp.exp2(s_ctx - mx)
            denom.append(jnp.sum(p_loc, axis=-1, keepdims=True)
                         + jnp.sum(pc, axis=-1, keepdims=True))
            o_loc.append(jnp.dot(p_loc.astype(BF16), v_ref[pl.ds(k0, nloc), ls],
                                 preferred_element_type=F32))
            p_ctx.append(pc.astype(BF16))
        o_ctx_all = jnp.dot(jnp.concatenate(p_ctx, axis=0), v_ctx, preferred_element_type=F32)
        for j in range(NAT_ROWS_PER_STEP):
            o = (o_loc[j] + o_ctx_all[j * pair_rows:(j + 1) * pair_rows]) / denom[j]
            o_ref[j * w:(j + 1) * w, ls] = jnp.where(lane < HEAD_DIM, o[:w], o[w:]).astype(o_ref.dtype)


def _nat_bias(rpb):
    col = jnp.arange(GRID_W)
    cstart = jnp.clip(col - NAT_KW // 2, 0, GRID_W - NAT_KW)
    col_ok = (col[None, :] >= cstart[:, None]) & (col[None, :] < cstart[:, None] + NAT_KW)
    dc_idx = jnp.clip(col[None, :] - col[:, None] + NAT_KW - 1, 0, 2 * NAT_KW - 2)
    rpb_c = jnp.where(col_ok[None, None], rpb[:, :, dc_idx] * LOG2E, NEG_INF)
    variants = []
    for d0 in range(NAT_KH):
        blk = rpb_c[:, d0:d0 + NAT_KH].transpose(0, 2, 1, 3)
        variants.append(blk.reshape(NAT_HEADS // 2, 2 * GRID_W, NAT_KH * GRID_W))
    return jnp.stack(variants, axis=0)


def _nat_latent(qkv, ctx_k, ctx_v, rpb, *, nb, seq_len):
    rows = seq_len // GRID_W
    kh = NAT_KH
    rb = NAT_ROWS_PER_STEP
    bias = _nat_bias(rpb)
    lc = ctx_k.shape[1]
    wd = NAT_HEADS * HEAD_DIM
    steps = rows // rb

    def bias_spec(j):
        def idx(b, i):
            r = i * rb + j
            return (jnp.clip(r - kh // 2, 0, rows - kh) - r + kh - 1, 0, 0, 0)
        return pl.BlockSpec((None, NAT_HEADS // 2, 2 * GRID_W, kh * GRID_W), idx)

    return pl.pallas_call(
        functools.partial(_nat_kernel, rows=rows),
        out_shape=jax.ShapeDtypeStruct((nb * seq_len, wd), BF16),
        grid=(nb, steps),
        in_specs=[pl.BlockSpec((rb * GRID_W, wd), lambda b, i: (b * steps + i, 0)),
                  pl.BlockSpec((seq_len, wd), lambda b, i: (b, 1)),
                  pl.BlockSpec((seq_len, wd), lambda b, i: (b, 2)),
                  pl.BlockSpec((None, lc, wd), lambda b, i: (b, 0, 0)),
                  pl.BlockSpec((None, lc, wd), lambda b, i: (b, 0, 0))]
                 + [bias_spec(j) for j in range(rb)],
        out_specs=pl.BlockSpec((rb * GRID_W, wd), lambda b, i: (b * steps + i, 0)),
        compiler_params=_params(("parallel", "arbitrary"), NAT_VMEM_LIMIT_BYTES),
        name="nat_latent",
    )(qkv, qkv, qkv, ctx_k, ctx_v, *([bias] * rb))


def _ssd_layer(xp, xs, mods_p, mods_s, gain, j, np_, ns_, lp, ls, state_f, state_b,
               ssd_w_in, ssd_conv_w, ssd_conv_b, ssd_dt_bias, ssd_a_log, ssd_d, ssd_norm,
               ssd_w_out):
    w_zx = ssd_w_in[j, :, :SSD_ZX_DIM].astype(BF16)
    w_dt = jnp.pad(ssd_w_in[j, :, SSD_ZX_DIM:], ((0, 0), (0, LANES - 2 * SSD_HEADS))).astype(BF16)
    w_out = ssd_w_out[j].astype(BF16)
    pad = LANES - 2 * SSD_HEADS
    dt_bias = jnp.pad(ssd_dt_bias[j].reshape(-1), (0, pad)).reshape(1, LANES)
    a_log = jnp.pad(ssd_a_log[j].reshape(-1), (0, pad)).reshape(1, LANES)
    d_rep = jnp.repeat(ssd_d[j], HEAD_DIM).reshape(1, SSD_D_INNER)
    norm_g = ssd_norm[j].reshape(1, SSD_D_INNER)
    head_of_col = jnp.arange(SSD_D_INNER) // HEAD_DIM
    expand = [jnp.tile((jnp.arange(LANES)[:, None] == head_of_col[None, :] + off).astype(BF16), (2, 1))
              for off in (0, SSD_HEADS)]

    outs = []
    states = None
    for (x, mods, nseq, seq_len, is_prompt) in ((xp, mods_p, np_, lp, True),
                                                (xs, mods_s, ns_, ls, False)):
        tag = "p" if is_prompt else "s"
        zx = _mod_matmul(x, mods, gain, w_zx, rows_per_group=seq_len if not is_prompt else x.shape[0],
                         out_dtype=BF16, tm=1024, tn=1024, name="ssd_in_zx_" + tag)
        dt_raw = _mod_matmul(x, mods, gain, w_dt,
                             rows_per_group=seq_len if not is_prompt else x.shape[0],
                             out_dtype=F32, tm=1024, tn=LANES, name="ssd_in_dt_" + tag)
        xbc = _conv_silu(zx, ssd_conv_w[j], ssd_conv_b[j], seq_len=seq_len)
        common = dict(nseq=nseq, seq_len=seq_len, layer=j)
        fwd = _ssd_scan(xbc, dt_raw, dt_bias, a_log, expand[0], reverse=False,
                        h0=None if is_prompt else state_f, emit_state=is_prompt, d_rep=d_rep,
                        name="ssd_scan_fwd_" + tag, **common)
        bwd = _ssd_scan(xbc, dt_raw, dt_bias, a_log, expand[1], reverse=True,
                        h0=None if is_prompt else state_b, emit_state=is_prompt, zx=zx,
                        y_fwd=fwd[0], norm_g=norm_g, name="ssd_scan_bwd_" + tag, **common)
        if is_prompt:
            states = (fwd[1], bwd[1])
        outs.append(_proj_residual(bwd[0], x, mods,
                                   w_out, rows_per_group=seq_len if not is_prompt else x.shape[0],
                                   tm=512, name="ssd_out_" + tag))
    return outs[0], outs[1], states


def kernel(x_prompt, x_sample, c, state_ssd_fwd, state_ssd_bwd, cache_attn_k, cache_attn_v, cache_nat_k, cache_nat_v, c_ctx, ada_w, ada_b, norm_mix, norm_mlp, mlp_w1, mlp_w2, ssd_w_in, ssd_conv_w, ssd_conv_b, ssd_dt_bias, ssd_a_log, ssd_d, ssd_norm, ssd_w_out, attn_w_qkv, attn_q_norm, attn_k_norm, attn_w_out, nat_w_qkv, nat_rpb, nat_w_out, norm_final):
    np_, lp, d = x_prompt.shape
    ns_, ls, _ = x_sample.shape
    mp_rows, ms_rows = np_ * lp, ns_ * ls
    xp = x_prompt.reshape(mp_rows, d)
    xs = x_sample.reshape(ms_rows, d)

    cond_rows = 16
    cond = jnp.concatenate([c_ctx[None, :], c, jnp.zeros((cond_rows - 1 - ns_, d), F32)], axis=0)
    mods_all = _ada_mods(cond, ada_w, ada_b).reshape(DEPTH, cond_rows, 6, d)

    n_ssd = state_ssd_fwd.shape[1]
    st_f = state_ssd_fwd.reshape(ns_, n_ssd, SSD_D_INNER, SSD_STATE)
    st_b = state_ssd_bwd.reshape(ns_, n_ssd, SSD_D_INNER, SSD_STATE)

    sf_out, sb_out, ak_out, av_out, nk_out, nv_out = [], [], [], [], [], []
    for i in range(DEPTH):
        kind, j = i % N_MIXERS, i // N_MIXERS
        mods_p = mods_all[i, 0:1]
        mods_s = mods_all[i, 1:1 + ns_]
        rpg_p, rpg_s = mp_rows, ls
        if kind == 0:
            xp, xs, (hf, hb) = _ssd_layer(
                xp, xs, mods_p, mods_s, norm_mix[i], j, np_, ns_, lp, ls, st_f, st_b,
                ssd_w_in, ssd_conv_w, ssd_conv_b, ssd_dt_bias, ssd_a_log, ssd_d, ssd_norm,
                ssd_w_out)
            sf_out.append(hf.reshape(np_, SSD_HEADS, HEAD_DIM, SSD_STATE))
            sb_out.append(hb.reshape(np_, SSD_HEADS, HEAD_DIM, SSD_STATE))
        elif kind == 1:
            w_qkv = attn_w_qkv[j].astype(BF16)
            w_out = attn_w_out[j].astype(BF16)
            nk = ATTN_KV_HEADS * HEAD_DIM
            qkv = _mod_matmul(xp, mods_p, norm_mix[i], w_qkv, rows_per_group=rpg_p,
                              out_dtype=F32, tm=1024, tn=768, name="gqa_qkv_p")
            q, k, v, kc, vc = _qk_norm(qkv, attn_q_norm[j], attn_k_norm[j], seq_len=lp,
                                       rope=False, emit_cache=True)
            o = _attention(q, 0, k, 0, v, 0, None, None, nb=np_, seq_len=lp, hq=ATTN_HEADS,
                           hkv=ATTN_KV_HEADS, tq=128, tk=lp, q_scale=1.0, name="gqa_attn_p")
            xp = _proj_residual(o, xp, mods_p, w_out, rows_per_group=rpg_p, tm=512,
                                name="gqa_out_p")
            ak_out.append(kc.reshape(np_, lp, ATTN_KV_HEADS, HEAD_DIM))
            av_out.append(vc.reshape(np_, lp, ATTN_KV_HEADS, HEAD_DIM))
            qkv = _mod_matmul(xs, mods_s, norm_mix[i], w_qkv, rows_per_group=rpg_s,
                              out_dtype=F32, tm=1024, tn=768, name="gqa_qkv_s")
            q, k, v = _qk_norm(qkv, attn_q_norm[j], attn_k_norm[j], seq_len=ls, rope=True,
                               emit_cache=False)
            past = cache_attn_k.shape[2]
            ck = cache_attn_k[:, j].reshape(ns_, past, nk).astype(BF16)
            cv = cache_attn_v[:, j].reshape(ns_, past, nk).astype(BF16)
            o = _attention(q, 0, k, 0, v, 0, ck, cv, nb=ns_, seq_len=ls, hq=ATTN_HEADS,
                           hkv=ATTN_KV_HEADS, tq=128, tk=1024, q_scale=1.0, name="gqa_attn_s")
            xs = _proj_residual(o, xs, mods_s, w_out, rows_per_group=rpg_s, tm=512,
                                name="gqa_out_s")
        else:
            w_qkv = nat_w_qkv[j].astype(BF16)
            w_out = nat_w_out[j].astype(BF16)
            wd = NAT_HEADS * HEAD_DIM
            scale = HEAD_DIM ** -0.5 * LOG2E
            qkv = _mod_matmul(xp, mods_p, norm_mix[i], w_qkv, rows_per_group=rpg_p,
                              out_dtype=F32, tm=1024, tn=1024, name="nat_qkv_p")
            o = _attention(qkv, 0, qkv, 1, qkv, 2, None, None, nb=np_, seq_len=lp, hq=NAT_HEADS,
                           hkv=NAT_HEADS, tq=lp, tk=lp, q_scale=scale, name="nat_attn_p")
            xp = _proj_residual(o, xp, mods_p, w_out, rows_per_group=rpg_p, tm=512,
                                name="nat_out_p")
            nk_out.append(qkv[:, wd:2 * wd].reshape(np_, lp, NAT_HEADS, HEAD_DIM))
            nv_out.append(qkv[:, 2 * wd:].reshape(np_, lp, NAT_HEADS, HEAD_DIM))
            qkv = _mod_matmul(xs, mods_s, norm_mix[i], w_qkv, rows_per_group=rpg_s,
                              out_dtype=BF16, tm=1024, tn=1024, name="nat_qkv_s")
            past = cache_nat_k.shape[2]
            ck = cache_nat_k[:, j].reshape(ns_, past, wd).astype(BF16)
            cv = cache_nat_v[:, j].reshape(ns_, past, wd).astype(BF16)
            o = _nat_latent(qkv, ck, cv, nat_rpb[j], nb=ns_, seq_len=ls)
            xs = _proj_residual(o, xs, mods_s, w_out, rows_per_group=rpg_s, tm=512,
                                name="nat_out_s")
        last = i == DEPTH - 1
        w1 = mlp_w1[i].astype(BF16)
        w2 = mlp_w2[i].astype(BF16)
        xp = _mlp(xp, mods_p, norm_mlp[i], w1, w2, norm_final, rows_per_group=rpg_p,
                  final_norm=last, name="mlp_p")
        xs = _mlp(xs, mods_s, norm_mlp[i], w1, w2, norm_final, rows_per_group=rpg_s,
                  final_norm=last, name="mlp_s")

    return (xp.reshape(np_, lp, d), xs.reshape(ns_, ls, d),
            jnp.stack(sf_out, axis=1), jnp.stack(sb_out, axis=1),
            jnp.stack(ak_out, axis=1), jnp.stack(av_out, axis=1),
            jnp.stack(nk_out, axis=1), jnp.stack(nv_out, axis=1))
```

```python
import functools
import math

import jax
import jax.numpy as jnp
from jax import lax
from jax.experimental import pallas as pl
from jax.experimental.pallas import tpu as pltpu

F32 = jnp.float32
BF16 = jnp.bfloat16

D_MODEL = 1024
DEPTH = 4
GRID_W = 64
N_MIXERS = 3
EPS = 1e-6
NEG_INF = -1e30
HEAD_DIM = 64
LANES = 128
LOG2E = math.log2(math.e)
SSD_D_INNER = 2 * D_MODEL
SSD_HEADS = SSD_D_INNER // HEAD_DIM
SSD_GROUPS = 4
SSD_HPG = SSD_HEADS // SSD_GROUPS
SSD_STATE = 128
SSD_CONV_W = 5
SSD_CHUNK = 128
SSD_CONV_DIM = SSD_D_INNER + 2 * SSD_GROUPS * SSD_STATE
SSD_ZX_DIM = SSD_D_INNER + SSD_CONV_DIM
ATTN_HEADS = D_MODEL // HEAD_DIM
ATTN_KV_HEADS = 4
ROPE_BASE = 10000.0
NAT_HEADS = D_MODEL // HEAD_DIM
NAT_KH = 8
NAT_KW = 16
MLP_HIDDEN = 4 * D_MODEL

VMEM_LIMIT_BYTES = 48 * 1024 * 1024
NAT_VMEM_LIMIT_BYTES = 56 * 1024 * 1024


def _params(semantics, vmem=VMEM_LIMIT_BYTES):
    return pltpu.CompilerParams(dimension_semantics=semantics, vmem_limit_bytes=vmem)


def _sigmoid(x):
    return 1.0 / (1.0 + jnp.exp(-x))


def _modulated_norm(x, gain, shift, scale):
    ms = jnp.mean(x * x, axis=-1, keepdims=True)
    return (x * lax.rsqrt(ms + EPS) * gain) * (1.0 + scale) + shift


def _ada_kernel(c_ref, w_ref, b_ref, o_ref):
    c = c_ref[...]
    a = (c * _sigmoid(c)).astype(BF16)
    o_ref[...] = jnp.dot(a, w_ref[...].astype(BF16), preferred_element_type=F32) + b_ref[...]


def _ada_mods(cond, ada_w, ada_b):
    depth, d, n = ada_w.shape
    r = cond.shape[0]
    tn = 1024
    return pl.pallas_call(
        _ada_kernel,
        out_shape=jax.ShapeDtypeStruct((depth, r, n), F32),
        grid=(depth, n // tn),
        in_specs=[pl.BlockSpec((r, d), lambda l, j: (0, 0)),
                  pl.BlockSpec((None, d, tn), lambda l, j: (l, 0, j)),
                  pl.BlockSpec((None, 1, tn), lambda l, j: (l, 0, j))],
        out_specs=pl.BlockSpec((None, r, tn), lambda l, j: (l, 0, j)),
        compiler_params=_params(("parallel", "parallel")),
        name="ada_mods",
    )(cond, ada_w, ada_b.reshape(depth, 1, n))


def _modmm_kernel(x_ref, mod_ref, g_ref, w_ref, o_ref, h_ref, *, shift_row, scale_row):
    @pl.when(pl.program_id(1) == 0)
    def _():
        h = _modulated_norm(x_ref[...], g_ref[...],
                            mod_ref[shift_row:shift_row + 1, :],
                            mod_ref[scale_row:scale_row + 1, :])
        h_ref[...] = h.astype(BF16)

    o_ref[...] = jnp.dot(h_ref[...], w_ref[...],
                         preferred_element_type=F32).astype(o_ref.dtype)


def _mod_matmul(x, mods, gain, w, *, rows_per_group, out_dtype, tm, tn, name):
    m, d = x.shape
    n = w.shape[1]
    tpg = rows_per_group // tm
    return pl.pallas_call(
        functools.partial(_modmm_kernel, shift_row=0, scale_row=1),
        out_shape=jax.ShapeDtypeStruct((m, n), out_dtype),
        grid=(m // tm, n // tn),
        in_specs=[pl.BlockSpec((tm, d), lambda i, j: (i, 0)),
                  pl.BlockSpec((None, 6, d), lambda i, j: (i // tpg, 0, 0)),
                  pl.BlockSpec((1, d), lambda i, j: (0, 0)),
                  pl.BlockSpec((d, tn), lambda i, j: (0, j))],
        out_specs=pl.BlockSpec((tm, tn), lambda i, j: (i, j)),
        scratch_shapes=[pltpu.VMEM((tm, d), BF16)],
        compiler_params=_params(("parallel", "arbitrary")),
        name=name,
    )(x, mods, gain.reshape(1, d), w)


def _proj_res_kernel(a_ref, x_ref, mod_ref, w_ref, o_ref, *, gate_row):
    y = jnp.dot(a_ref[...], w_ref[...], preferred_element_type=F32)
    o_ref[...] = x_ref[...] + mod_ref[gate_row:gate_row + 1, :] * y


def _proj_residual(a, x, mods, w, *, rows_per_group, tm, name):
    m, k = a.shape
    d = x.shape[1]
    tpg = rows_per_group // tm
    return pl.pallas_call(
        functools.partial(_proj_res_kernel, gate_row=2),
        out_shape=jax.ShapeDtypeStruct((m, d), F32),
        grid=(m // tm,),
        in_specs=[pl.BlockSpec((tm, k), lambda i: (i, 0)),
                  pl.BlockSpec((tm, d), lambda i: (i, 0)),
                  pl.BlockSpec((None, 6, d), lambda i: (i // tpg, 0, 0)),
                  pl.BlockSpec((k, d), lambda i: (0, 0))],
        out_specs=pl.BlockSpec((tm, d), lambda i: (i, 0)),
        compiler_params=_params(("parallel",)),
        name=name,
    )(a, x, mods, w)


def _mlp_kernel(x_ref, mod_ref, g_ref, w1_ref, w2_ref, gf_ref, o_ref, h_ref, acc_ref,
                *, final_norm):
    j = pl.program_id(1)

    @pl.when(j == 0)
    def _():
        h = _modulated_norm(x_ref[...], g_ref[...], mod_ref[3:4, :], mod_ref[4:5, :])
        h_ref[...] = h.astype(BF16)
        acc_ref[...] = jnp.zeros_like(acc_ref)

    a = jnp.dot(h_ref[...], w1_ref[...], preferred_element_type=F32)
    a = jnp.maximum(a, 0.0)
    acc_ref[...] += jnp.dot((a * a).astype(BF16), w2_ref[...], preferred_element_type=F32)

    @pl.when(j == pl.num_programs(1) - 1)
    def _():
        y = x_ref[...] + mod_ref[5:6, :] * acc_ref[...]
        if final_norm:
            ms = jnp.mean(y * y, axis=-1, keepdims=True)
            y = y * lax.rsqrt(ms + EPS) * gf_ref[...]
        o_ref[...] = y


def _mlp(x, mods, gain, w1, w2, gain_final, *, rows_per_group, final_norm, name):
    m, d = x.shape
    hdim = w1.shape[1]
    tm, th = 1024, 1024
    tpg = rows_per_group // tm
    return pl.pallas_call(
        functools.partial(_mlp_kernel, final_norm=final_norm),
        out_shape=jax.ShapeDtypeStruct((m, d), F32),
        grid=(m // tm, hdim // th),
        in_specs=[pl.BlockSpec((tm, d), lambda i, j: (i, 0)),
                  pl.BlockSpec((None, 6, d), lambda i, j: (i // tpg, 0, 0)),
                  pl.BlockSpec((1, d), lambda i, j: (0, 0)),
                  pl.BlockSpec((d, th), lambda i, j: (0, j)),
                  pl.BlockSpec((th, d), lambda i, j: (j, 0)),
                  pl.BlockSpec((1, d), lambda i, j: (0, 0))],
        out_specs=pl.BlockSpec((tm, d), lambda i, j: (i, 0)),
        scratch_shapes=[pltpu.VMEM((tm, d), BF16), pltpu.VMEM((tm, d), F32)],
        compiler_params=_params(("parallel", "arbitrary")),
        name=name,
    )(x, mods, gain.reshape(1, d), w1, w2, gain_final.reshape(1, d))


CONV_HALO_ROWS = 16


CONV_TAPS_SHIFTED = (0, 1, 3, 4)


def _conv_kernel(prev_ref, cur_ref, next_ref, sh_ref, w_ref, b_ref, o_ref, *, tiles_per_seq):
    pos = pl.program_id(0) % tiles_per_seq
    cur_b = cur_ref[...]
    cur = cur_b.astype(F32)
    tr = cur.shape[0]
    h = CONV_HALO_ROWS
    w = w_ref[...]
    bias = b_ref[...]

    def silu(y):
        return (y / (1.0 + jnp.exp2(y * (-LOG2E)))).astype(o_ref.dtype)

    y = bias + w[2:3] * cur
    for idx, k in enumerate(CONV_TAPS_SHIFTED):
        y = y + w[k:k + 1] * jnp.dot(sh_ref[idx], cur_b, preferred_element_type=F32)
    o_ref[...] = silu(y)

    def edge(x):
        ye = bias + w[2:3] * x
        for k in CONV_TAPS_SHIFTED:
            ye = ye + w[k:k + 1] * pltpu.roll(x, (SSD_CONV_W // 2 - k) % x.shape[0], 0)
        return silu(ye)[h:2 * h]

    pv = jnp.where(pos == 0, 0.0, prev_ref[...].astype(F32))
    nx = jnp.where(pos == tiles_per_seq - 1, 0.0, next_ref[...].astype(F32))
    o_ref[0:h, :] = edge(jnp.concatenate([pv, cur[0:2 * h]], axis=0))
    o_ref[tr - h:tr, :] = edge(jnp.concatenate([cur[tr - 2 * h:tr], nx], axis=0))


def _conv_silu(zx, conv_w, conv_b, *, seq_len):
    m = zx.shape[0]
    tr, tc = 256, 1024
    c0 = SSD_D_INNER // tc
    halo_per_tile = tr // CONV_HALO_ROWS
    n_halo = m // CONV_HALO_ROWS
    t = jnp.arange(tr)
    shifts = jnp.stack([(t[None, :] == t[:, None] + (k - SSD_CONV_W // 2)).astype(BF16)
                        for k in CONV_TAPS_SHIFTED])
    return pl.pallas_call(
        functools.partial(_conv_kernel, tiles_per_seq=seq_len // tr),
        out_shape=jax.ShapeDtypeStruct((m, SSD_CONV_DIM), BF16),
        grid=(m // tr, SSD_CONV_DIM // tc),
        in_specs=[
            pl.BlockSpec((CONV_HALO_ROWS, tc),
                         lambda i, j: (jnp.maximum(i * halo_per_tile - 1, 0), c0 + j)),
            pl.BlockSpec((tr, tc), lambda i, j: (i, c0 + j)),
            pl.BlockSpec((CONV_HALO_ROWS, tc),
                         lambda i, j: (jnp.minimum((i + 1) * halo_per_tile, n_halo - 1), c0 + j)),
            pl.BlockSpec((len(CONV_TAPS_SHIFTED), tr, tr), lambda i, j: (0, 0, 0)),
            pl.BlockSpec((SSD_CONV_W, tc), lambda i, j: (0, j)),
            pl.BlockSpec((1, tc), lambda i, j: (0, j))],
        out_specs=pl.BlockSpec((tr, tc), lambda i, j: (i, j)),
        compiler_params=_params(("parallel", "parallel")),
        name="ssd_conv_silu",
    )(zx, zx, zx, shifts, conv_w, conv_b.reshape(1, SSD_CONV_DIM))


def _split3(a):
    a1 = a.astype(BF16)
    r1 = a - a1.astype(F32)
    a2 = r1.astype(BF16)
    a3 = (r1 - a2.astype(F32)).astype(BF16)
    return a1, a2, a3


def _dot_exact_left(m01, a):
    return sum(jnp.dot(m01, p, preferred_element_type=F32) for p in _split3(a))


def _dot_exact_right(a, m01):
    return sum(jnp.dot(p, m01, preferred_element_type=F32) for p in _split3(a))


def _spread_heads(a, e2_ref):
    hi = a.astype(BF16)
    lo = (a - hi.astype(F32)).astype(BF16)
    return jnp.dot(jnp.concatenate([hi, lo], axis=1), e2_ref[...], preferred_element_type=F32)


SSD_SEQS_PER_STEP = 2


def _ssd_chunk(xbc_ref, dt_ref, dtb_ref, alog_ref, e_ref, z_ref, yf_ref, ng_ref, drep_ref,
               y_ref, s_ref, ybuf_ref, *, reverse, combine):
    q = SSD_CHUNK
    gw = SSD_HPG * HEAD_DIM
    col0 = SSD_HEADS if reverse else 0

    pre = dt_ref[...] + dtb_ref[...]
    dt = jnp.maximum(pre, 0.0) + jnp.log1p(jnp.exp(-jnp.abs(pre)))
    a_dt = dt * (-jnp.exp(alog_ref[...]) * LOG2E)
    ri = lax.broadcasted_iota(jnp.int32, (q, q), 0)
    ci = lax.broadcasted_iota(jnp.int32, (q, q), 1)
    keep = (ri <= ci) if reverse else (ri >= ci)
    cs = _dot_exact_left(jnp.where(keep, 1.0, 0.0).astype(BF16), a_dt)
    cs_t = cs.T
    dt_t = dt.T
    edge = cs[0:1, :] if reverse else cs[q - 1:q, :]
    exp_cs = _spread_heads(jnp.exp2(cs), e_ref)
    to_end = _spread_heads(jnp.exp2(edge - cs) * dt, e_ref)
    edge_rep = _dot_exact_right(jnp.broadcast_to(edge, (8, LANES)), e_ref[0:LANES, :])[0:1, :]
    exp_edge = jnp.exp2(edge_rep)
    lane = lax.broadcasted_iota(jnp.int32, (1, LANES), 1)
    keep_lo = jnp.where(lane < HEAD_DIM, 1.0, 0.0).astype(BF16)
    keep_hi = jnp.where(lane < HEAD_DIM, 0.0, 1.0).astype(BF16)

    for g in range(SSD_GROUPS):
        gs = slice(g * gw, (g + 1) * gw)
        b_g = xbc_ref[:, SSD_D_INNER + g * SSD_STATE:SSD_D_INNER + (g + 1) * SSD_STATE]
        c_off = SSD_D_INNER + SSD_GROUPS * SSD_STATE
        c_g = xbc_ref[:, c_off + g * SSD_STATE:c_off + (g + 1) * SSD_STATE]
        cb = lax.dot_general(c_g, b_g, (((1,), (1,)), ((), ())), preferred_element_type=F32)
        cb = jnp.where(keep, cb, 0.0)
        s_g = s_ref[:, gs]
        y_off = jnp.dot(c_g, s_g.astype(BF16), preferred_element_type=F32)
        for mp in range(gw // LANES):
            m = g * (gw // LANES) + mp
            ls = slice(m * LANES, (m + 1) * LANES)
            x_pair = xbc_ref[:, ls]
            k0 = col0 + 2 * m
            ws = []
            for k in (k0, k0 + 1):
                seg = jnp.minimum(cs[:, k:k + 1] - cs_t[k:k + 1, :], 0.0)
                ws.append((jnp.exp2(seg) * cb * dt_t[k:k + 1, :]).astype(BF16))
            y_pair = (jnp.dot(jnp.concatenate(ws, axis=1),
                              jnp.concatenate([x_pair * keep_lo, x_pair * keep_hi], axis=0),
                              preferred_element_type=F32)
                      + y_off[:, mp * LANES:(mp + 1) * LANES] * exp_cs[:, ls])
            if not combine:
                y_pair = y_pair + drep_ref[:, ls] * x_pair.astype(F32)
            ybuf_ref[:, ls] = y_pair
        xs_g = (to_end[:, gs] * xbc_ref[:, gs].astype(F32)).astype(BF16)
        b_t = b_g.astype(F32).T.astype(BF16)
        s_ref[:, gs] = s_g * exp_edge[:, gs] + jnp.dot(b_t, xs_g, preferred_element_type=F32)

    if combine:
        z = z_ref[...].astype(F32)
        gated = (ybuf_ref[...] + yf_ref[...]) * (z * _sigmoid(z))
        ms = jnp.mean(gated * gated, axis=-1, keepdims=True)
        y_ref[...] = (gated * lax.rsqrt(ms + EPS) * ng_ref[...]).astype(y_ref.dtype)


def _ssd_kernel(*refs, reverse, has_h0, emit_state, combine):
    it = iter(refs)
    xbc_ref, dt_ref, dtb_ref, alog_ref, e_ref = (next(it) for _ in range(5))
    h0_ref = next(it) if has_h0 else None
    z_ref = yf_ref = ng_ref = drep_ref = None
    if combine:
        z_ref, yf_ref, ng_ref = next(it), next(it), next(it)
    else:
        drep_ref = next(it)
    y_ref = next(it)
    st_ref = next(it) if emit_state else None
    s_ref = next(it)
    ybuf_ref = next(it) if combine else y_ref
    q = SSD_CHUNK
    c = pl.program_id(1)
    seqs = range(SSD_SEQS_PER_STEP)

    @pl.when(c == 0)
    def _():
        if has_h0:
            for sq in seqs:
                for k in range(SSD_D_INNER // q):
                    s_ref[sq, :, k * q:(k + 1) * q] = h0_ref[sq, k * q:(k + 1) * q, :].T
        else:
            s_ref[...] = jnp.zeros_like(s_ref)

    for sq in seqs:
        _ssd_chunk(xbc_ref.at[sq], dt_ref.at[sq], dtb_ref, alog_ref, e_ref,
                   z_ref.at[sq] if combine else None, yf_ref.at[sq] if combine else None,
                   ng_ref, drep_ref, y_ref.at[sq], s_ref.at[sq], ybuf_ref.at[sq],
                   reverse=reverse, combine=combine)

    if emit_state:
        @pl.when(c == pl.num_programs(1) - 1)
        def _():
            for sq in seqs:
                for k in range(SSD_D_INNER // q):
                    st_ref[sq, k * q:(k + 1) * q, :] = s_ref[sq, :, k * q:(k + 1) * q].T


def _ssd_scan(xbc, dt_raw, dt_bias, a_log, expand, *, nseq, seq_len, reverse, h0=None,
              layer=0, emit_state=False, zx=None, y_fwd=None, norm_g=None, d_rep=None, name):
    combine = zx is not None
    nchunk = seq_len // SSD_CHUNK
    q = SSD_CHUNK
    sp = SSD_SEQS_PER_STEP
    nb = nseq // sp

    def per_seq(a):
        return a.reshape(nb, sp, seq_len, a.shape[-1])

    def chunk_spec(width):
        return pl.BlockSpec((None, sp, q, width),
                            lambda b, c: (b, 0, (nchunk - 1 - c) if reverse else c, 0))

    def const_spec(shape):
        return pl.BlockSpec(shape, lambda b, c: (0,) * len(shape))

    in_specs = [chunk_spec(SSD_CONV_DIM), chunk_spec(LANES), const_spec((1, LANES)),
                const_spec((1, LANES)), const_spec((2 * LANES, SSD_D_INNER))]
    args = [per_seq(xbc), per_seq(dt_raw), dt_bias, a_log, expand]
    if h0 is not None:
        in_specs.append(pl.BlockSpec((None, sp, None, SSD_D_INNER, SSD_STATE),
                                     lambda b, c: (b, 0, layer, 0, 0)))
        args.append(h0.reshape((nb, sp) + h0.shape[1:]))
    if combine:
        in_specs += [chunk_spec(SSD_D_INNER), chunk_spec(SSD_D_INNER),
                     const_spec((1, SSD_D_INNER))]
        args += [per_seq(zx), per_seq(y_fwd), norm_g]
    else:
        in_specs.append(const_spec((1, SSD_D_INNER)))
        args.append(d_rep)
    out_shape = [jax.ShapeDtypeStruct((nb, sp, seq_len, SSD_D_INNER), BF16 if combine else F32)]
    out_specs = [chunk_spec(SSD_D_INNER)]
    if emit_state:
        out_shape.append(jax.ShapeDtypeStruct((nb, sp, SSD_D_INNER, SSD_STATE), F32))
        out_specs.append(pl.BlockSpec((None, sp, SSD_D_INNER, SSD_STATE),
                                      lambda b, c: (b, 0, 0, 0)))
    scratch = [pltpu.VMEM((sp, SSD_STATE, SSD_D_INNER), F32)]
    if combine:
        scratch.append(pltpu.VMEM((sp, q, SSD_D_INNER), F32))
    outs = list(pl.pallas_call(
        functools.partial(_ssd_kernel, reverse=reverse, has_h0=h0 is not None,
                          emit_state=emit_state, combine=combine),
        out_shape=out_shape,
        grid=(nb, nchunk),
        in_specs=in_specs,
        out_specs=out_specs,
        scratch_shapes=scratch,
        compiler_params=_params(("parallel", "arbitrary")),
        name=name,
    )(*args))
    outs[0] = outs[0].reshape(nseq * seq_len, SSD_D_INNER)
    if emit_state:
        outs[1] = outs[1].reshape(nseq, SSD_D_INNER, SSD_STATE)
    return outs


def _head_rms(x, gain_row):
    rows, width = x.shape
    lane = lax.broadcasted_iota(jnp.int32, (rows, LANES), 1)
    lo = lane < HEAD_DIM
    outs = []
    for j in range(width // LANES):
        xb = x[:, j * LANES:(j + 1) * LANES]
        sq = xb * xb
        s_lo = jnp.sum(jnp.where(lo, sq, 0.0), axis=-1, keepdims=True)
        s_hi = jnp.sum(jnp.where(lo, 0.0, sq), axis=-1, keepdims=True)
        ms = jnp.where(lo, s_lo, s_hi) * (1.0 / HEAD_DIM)
        outs.append(xb * lax.rsqrt(ms + EPS))
    return jnp.concatenate(outs, axis=1) * gain_row


def _rope(x, cos, sin_signed):
    rows, width = x.shape
    lane = lax.broadcasted_iota(jnp.int32, (rows, LANES), 1)
    even = (lane % 2) == 0
    outs = []
    for j in range(width // LANES):
        xb = x[:, j * LANES:(j + 1) * LANES]
        partner = jnp.where(even, pltpu.roll(xb, LANES - 1, 1), pltpu.roll(xb, 1, 1))
        outs.append(xb * cos + partner * sin_signed)
    return jnp.concatenate(outs, axis=1)


def _qknorm_kernel(*refs, rope, emit_cache):
    it = iter(refs)
    q_ref, k_ref, v_ref, qg_ref, kg_ref = (next(it) for _ in range(5))
    if rope:
        cos_ref, sin_ref = next(it), next(it)
    qo_ref, ko_ref, vo_ref = next(it), next(it), next(it)
    q = _head_rms(q_ref[...], qg_ref[...])
    k = _head_rms(k_ref[...], kg_ref[...])
    if emit_cache:
        kc_ref, vc_ref = next(it), next(it)
        kc_ref[...] = k
        vc_ref[...] = v_ref[...]
    if rope:
        q = _rope(q, cos_ref[...], sin_ref[...])
        k = _rope(k, cos_ref[...], sin_ref[...])
    qo_ref[...] = (q * (HEAD_DIM ** -0.5 * LOG2E)).astype(BF16)
    ko_ref[...] = k.astype(BF16)
    v = v_ref[...]
    vo_ref[...] = (_v_with_ones(v) if rope else v).astype(BF16)


def _rope_tables(seq_len):
    n_pairs = HEAD_DIM // 4
    t = jnp.arange(seq_len)
    row = (t // GRID_W).astype(F32)
    col = (t % GRID_W).astype(F32)
    freqs = ROPE_BASE ** (-jnp.arange(n_pairs, dtype=F32) / n_pairs)
    ang = jnp.concatenate([row[:, None] * freqs, col[:, None] * freqs], axis=-1)
    cos = jnp.repeat(jnp.cos(ang), 2, axis=-1)
    sin = jnp.repeat(jnp.sin(ang), 2, axis=-1)
    sign = jnp.where(jnp.arange(HEAD_DIM) % 2 == 0, -1.0, 1.0).astype(F32)
    return jnp.tile(cos, (1, 2)), jnp.tile(sin * sign, (1, 2))


def _qk_norm(qkv, q_gain, k_gain, *, seq_len, rope, emit_cache):
    m = qkv.shape[0]
    tm = 256
    nq = ATTN_HEADS * HEAD_DIM
    nk = ATTN_KV_HEADS * HEAD_DIM
    qg = jnp.tile(q_gain, ATTN_HEADS).reshape(1, nq)
    kg = jnp.tile(k_gain, ATTN_KV_HEADS).reshape(1, nk)
    in_specs = [pl.BlockSpec((tm, nq), lambda i: (i, 0)),
                pl.BlockSpec((tm, nk), lambda i: (i, nq // nk)),
                pl.BlockSpec((tm, nk), lambda i: (i, nq // nk + 1)),
                pl.BlockSpec((1, nq), lambda i: (0, 0)),
                pl.BlockSpec((1, nk), lambda i: (0, 0))]
    args = [qkv, qkv, qkv, qg, kg]
    if rope:
        cos, sin = _rope_tables(seq_len)
        tps = seq_len // tm
        in_specs += [pl.BlockSpec((tm, LANES), lambda i: (i % tps, 0)),
                     pl.BlockSpec((tm, LANES), lambda i: (i % tps, 0))]
        args += [cos, sin]
    out_shape = [jax.ShapeDtypeStruct((m, nq), BF16), jax.ShapeDtypeStruct((m, nk), BF16),
                 jax.ShapeDtypeStruct((m, nk), BF16)]
    out_specs = [pl.BlockSpec((tm, nq), lambda i: (i, 0)), pl.BlockSpec((tm, nk), lambda i: (i, 0)),
                 pl.BlockSpec((tm, nk), lambda i: (i, 0))]
    if rope:
        out_shape[2] = jax.ShapeDtypeStruct((m, ATTN_KV_HEADS * LANES), BF16)
        out_specs[2] = pl.BlockSpec((tm, ATTN_KV_HEADS * LANES), lambda i: (i, 0))
    if emit_cache:
        out_shape += [jax.ShapeDtypeStruct((m, nk), F32)] * 2
        out_specs += [pl.BlockSpec((tm, nk), lambda i: (i, 0))] * 2
    return pl.pallas_call(
        functools.partial(_qknorm_kernel, rope=rope, emit_cache=emit_cache),
        out_shape=out_shape,
        grid=(m // tm,),
        in_specs=in_specs,
        out_specs=out_specs,
        compiler_params=_params(("parallel",)),
        name="gqa_qk_norm_rope" if rope else "gqa_qk_norm",
    )(*args)


def _place_half(x, lane, src_half, dst_half):
    if src_half != dst_half:
        x = pltpu.roll(x, HEAD_DIM, 1)
    keep = (lane >= HEAD_DIM) if dst_half else (lane < HEAD_DIM)
    return jnp.where(keep, x, 0.0)


def _v_with_ones(v):
    lane = lax.broadcasted_iota(jnp.int32, v.shape[:-1] + (LANES,), v.ndim - 1)
    tiles = []
    for g in range(v.shape[-1] // HEAD_DIM):
        pair = v[..., (g // 2) * LANES:(g // 2 + 1) * LANES]
        own = (lane >= HEAD_DIM) if g % 2 else (lane < HEAD_DIM)
        tiles.append(jnp.where(own, pair, 1.0))
    return jnp.concatenate(tiles, axis=-1)


def _attn_kernel(*refs, hq, hkv, tk, has_ctx, q_scale, v_ones):
    if has_ctx:
        q_ref, k_ref, v_ref, ck_ref, cv_ref, o_ref = refs
    else:
        q_ref, k_ref, v_ref, o_ref = refs
    grp = hq // hkv
    tq = q_ref.shape[0]
    rows = grp * tq
    lane = lax.broadcasted_iota(jnp.int32, (tq, LANES), 1)
    nt = (((1,), (1,)), ((), ()))

    qst = []
    for g in range(hkv):
        parts = []
        for u in range(grp):
            h = g * grp + u
            qb = q_ref[:, (h // 2) * LANES:(h // 2 + 1) * LANES].astype(F32)
            if q_scale != 1.0:
                qb = qb * q_scale
            parts.append(_place_half(qb, lane, h % 2, g % 2).astype(BF16))
        qst.append(parts[0] if grp == 1 else jnp.concatenate(parts, axis=0))

    def step(g, kblk, vblk, carry):
        m_i, l_i, acc = carry
        s = lax.dot_general(qst[g], kblk.astype(BF16), nt, preferred_element_type=F32)
        m_n = jnp.maximum(m_i, jnp.max(s, axis=-1, keepdims=True))
        alpha = jnp.exp2(m_i - m_n)
        p = jnp.exp2(s - m_n)
        acc = alpha * acc + jnp.dot(p.astype(BF16), vblk.astype(BF16),
                                    preferred_element_type=F32)
        if v_ones:
            return m_n, l_i, acc
        return m_n, alpha * l_i + jnp.sum(p, axis=-1, keepdims=True), acc

    def all_heads(kv_at, carries):
        out = []
        for g in range(hkv):
            kc = slice((g // 2) * LANES, (g // 2 + 1) * LANES)
            vc = slice(g * LANES, (g + 1) * LANES) if v_ones else kc
            out.append(step(g, *kv_at(kc, vc), carries[g]))
        return tuple(out)

    carries = tuple((jnp.full((rows, 1), -jnp.inf, F32), jnp.zeros((rows, 1), F32),
                     jnp.zeros((rows, LANES), F32)) for _ in range(hkv))
    n_self = k_ref.shape[0] // tk
    if n_self == 1:
        carries = all_heads(lambda kc, vc: (k_ref[:, kc], v_ref[:, vc]), carries)
    else:
        def body(ci, car):
            r0 = pl.multiple_of(ci * tk, tk)
            return all_heads(lambda kc, vc: (k_ref[pl.ds(r0, tk), kc], v_ref[pl.ds(r0, tk), vc]),
                             car)
        carries = lax.fori_loop(0, n_self, body, carries, unroll=2)
    if has_ctx:
        carries = all_heads(lambda kc, vc: (ck_ref[:, kc], cv_ref[:, vc]), carries)

    head_out = [None] * hq
    for g in range(hkv):
        _, l_f, acc = carries[g]
        if v_ones:
            ones_lane = HEAD_DIM * (1 - g % 2)
            l_f = acc[:, ones_lane:ones_lane + 1]
        o = acc / l_f
        for u in range(grp):
            head_out[g * grp + u] = o[u * tq:(u + 1) * tq]
    for mpair in range(hq // 2):
        oa, ob = head_out[2 * mpair], head_out[2 * mpair + 1]
        if ((2 * mpair) // grp) % 2 != 0:
            oa = pltpu.roll(oa, HEAD_DIM, 1)
        if ((2 * mpair + 1) // grp) % 2 != 1:
            ob = pltpu.roll(ob, HEAD_DIM, 1)
        o_ref[:, mpair * LANES:(mpair + 1) * LANES] = jnp.where(
            lane < HEAD_DIM, oa, ob).astype(o_ref.dtype)


def _attention(q_arr, q_col, k_arr, k_col, v_arr, v_col, ctx_k, ctx_v, *, nb, seq_len,
               hq, hkv, tq, tk, q_scale, name, v_ones=False):
    wq, wk = hq * HEAD_DIM, hkv * HEAD_DIM
    wv = hkv * LANES if v_ones else wk
    qt = seq_len // tq
    in_specs = [pl.BlockSpec((tq, wq), lambda b, i: (b * qt + i, q_col)),
                pl.BlockSpec((seq_len, wk), lambda b, i: (b, k_col)),
                pl.BlockSpec((seq_len, wv), lambda b, i: (b, v_col))]
    args = [q_arr, k_arr, v_arr]
    has_ctx = ctx_k is not None
    if has_ctx:
        lc = ctx_k.shape[1]
        in_specs += [pl.BlockSpec((None, lc, wk), lambda b, i: (b, 0, 0)),
                     pl.BlockSpec((None, lc, wv), lambda b, i: (b, 0, 0))]
        args += [ctx_k, ctx_v]
    return pl.pallas_call(
        functools.partial(_attn_kernel, hq=hq, hkv=hkv, tk=tk, has_ctx=has_ctx, q_scale=q_scale,
                          v_ones=v_ones),
        out_shape=jax.ShapeDtypeStruct((nb * seq_len, wq), BF16),
        grid=(nb, qt),
        in_specs=in_specs,
        out_specs=pl.BlockSpec((tq, wq), lambda b, i: (b * qt + i, 0)),
        compiler_params=_params(("parallel", "arbitrary")),
        name=name,
    )(*args)


NAT_ROWS_PER_STEP = 2


def _nat_kernel(q_ref, k_ref, v_ref, ck_ref, cv_ref, *rest, rows):
    bias_refs, o_ref = rest[:NAT_ROWS_PER_STEP], rest[NAT_ROWS_PER_STEP]
    kh, w = NAT_KH, GRID_W
    nloc = kh * w
    pair_rows = 2 * w
    lane = lax.broadcasted_iota(jnp.int32, (w, LANES), 1)
    nt = (((1,), (1,)), ((), ()))
    q_scale = (HEAD_DIM ** -0.5) * LOG2E
    for mpair in range(NAT_HEADS // 2):
        ls = slice(mpair * LANES, (mpair + 1) * LANES)
        k_ctx = ck_ref[:, ls]
        v_ctx = cv_ref[:, ls]
        qs = []
        for j in range(NAT_ROWS_PER_STEP):
            qb = q_ref[j * w:(j + 1) * w, ls].astype(F32) * q_scale
            qs.append(jnp.concatenate([_place_half(qb, lane, 0, 0), _place_half(qb, lane, 1, 1)],
                                      axis=0).astype(BF16))
        s_ctx_all = lax.dot_general(jnp.concatenate(qs, axis=0), k_ctx, nt,
                                    preferred_element_type=F32)
        p_ctx, o_loc, denom = [], [], []
        for j in range(NAT_ROWS_PER_STEP):
            r = pl.program_id(1) * NAT_ROWS_PER_STEP + j
            k0 = pl.multiple_of(jnp.clip(r - kh // 2, 0, rows - kh) * w, w)
            s_loc = (lax.dot_general(qs[j], k_ref[pl.ds(k0, nloc), ls], nt,
                                     preferred_element_type=F32) + bias_refs[j][mpair])
            s_ctx = s_ctx_all[j * pair_rows:(j + 1) * pair_rows]
            mx = jnp.maximum(jnp.max(s_loc, axis=-1, keepdims=True),
                             jnp.max(s_ctx, axis=-1, keepdims=True))
            p_loc = jnp.exp2(s_loc - mx)
            pc = jnp.exp2(s_ctx - mx)
            denom.append(jnp.sum(p_loc, axis=-1, keepdims=True)
                         + jnp.sum(pc, axis=-1, keepdims=True))
            o_loc.append(jnp.dot(p_loc.astype(BF16), v_ref[pl.ds(k0, nloc), ls],
                                 preferred_element_type=F32))
            p_ctx.append(pc.astype(BF16))
        o_ctx_all = jnp.dot(jnp.concatenate(p_ctx, axis=0), v_ctx, preferred_element_type=F32)
        for j in range(NAT_ROWS_PER_STEP):
            o = (o_loc[j] + o_ctx_all[j * pair_rows:(j + 1) * pair_rows]) / denom[j]
            o_ref[j * w:(j + 1) * w, ls] = jnp.where(lane < HEAD_DIM, o[:w], o[w:]).astype(o_ref.dtype)


def _nat_bias(rpb):
    col = jnp.arange(GRID_W)
    cstart = jnp.clip(col - NAT_KW // 2, 0, GRID_W - NAT_KW)
    col_ok = (col[None, :] >= cstart[:, None]) & (col[None, :] < cstart[:, None] + NAT_KW)
    dc_idx = jnp.clip(col[None, :] - col[:, None] + NAT_KW - 1, 0, 2 * NAT_KW - 2)
    rpb_c = jnp.where(col_ok[None, None], rpb[:, :, dc_idx] * LOG2E, NEG_INF)
    variants = []
    for d0 in range(NAT_KH):
        blk = rpb_c[:, d0:d0 + NAT_KH].transpose(0, 2, 1, 3)
        variants.append(blk.reshape(NAT_HEADS // 2, 2 * GRID_W, NAT_KH * GRID_W))
    return jnp.stack(variants, axis=0)


def _nat_latent(qkv, ctx_k, ctx_v, rpb, *, nb, seq_len):
    rows = seq_len // GRID_W
    kh = NAT_KH
    rb = NAT_ROWS_PER_STEP
    bias = _nat_bias(rpb)
    lc = ctx_k.shape[1]
    wd = NAT_HEADS * HEAD_DIM
    steps = rows // rb

    def bias_spec(j):
        def idx(b, i):
            r = i * rb + j
            return (jnp.clip(r - kh // 2, 0, rows - kh) - r + kh - 1, 0, 0, 0)
        return pl.BlockSpec((None, NAT_HEADS // 2, 2 * GRID_W, kh * GRID_W), idx)

    return pl.pallas_call(
        functools.partial(_nat_kernel, rows=rows),
        out_shape=jax.ShapeDtypeStruct((nb * seq_len, wd), BF16),
        grid=(nb, steps),
        in_specs=[pl.BlockSpec((rb * GRID_W, wd), lambda b, i: (b * steps + i, 0)),
                  pl.BlockSpec((seq_len, wd), lambda b, i: (b, 1)),
                  pl.BlockSpec((seq_len, wd), lambda b, i: (b, 2)),
                  pl.BlockSpec((None, lc, wd), lambda b, i: (b, 0, 0)),
                  pl.BlockSpec((None, lc, wd), lambda b, i: (b, 0, 0))]
                 + [bias_spec(j) for j in range(rb)],
        out_specs=pl.BlockSpec((rb * GRID_W, wd), lambda b, i: (b * steps + i, 0)),
        compiler_params=_params(("parallel", "arbitrary"), NAT_VMEM_LIMIT_BYTES),
        name="nat_latent",
    )(qkv, qkv, qkv, ctx_k, ctx_v, *([bias] * rb))


def _ssd_layer(xp, xs, mods_p, mods_s, gain, j, np_, ns_, lp, ls, state_f, state_b,
               ssd_w_in, ssd_conv_w, ssd_conv_b, ssd_dt_bias, ssd_a_log, ssd_d, ssd_norm,
               ssd_w_out):
    w_zx = ssd_w_in[j, :, :SSD_ZX_DIM].astype(BF16)
    w_dt = jnp.pad(ssd_w_in[j, :, SSD_ZX_DIM:], ((0, 0), (0, LANES - 2 * SSD_HEADS))).astype(BF16)
    w_out = ssd_w_out[j].astype(BF16)
    pad = LANES - 2 * SSD_HEADS
    dt_bias = jnp.pad(ssd_dt_bias[j].reshape(-1), (0, pad)).reshape(1, LANES)
    a_log = jnp.pad(ssd_a_log[j].reshape(-1), (0, pad)).reshape(1, LANES)
    d_rep = jnp.repeat(ssd_d[j], HEAD_DIM).reshape(1, SSD_D_INNER)
    norm_g = ssd_norm[j].reshape(1, SSD_D_INNER)
    head_of_col = jnp.arange(SSD_D_INNER) // HEAD_DIM
    expand = [jnp.tile((jnp.arange(LANES)[:, None] == head_of_col[None, :] + off).astype(BF16), (2, 1))
              for off in (0, SSD_HEADS)]

    outs = []
    states = None
    for (x, mods, nseq, seq_len, is_prompt) in ((xp, mods_p, np_, lp, True),
                                                (xs, mods_s, ns_, ls, False)):
        tag = "p" if is_prompt else "s"
        zx = _mod_matmul(x, mods, gain, w_zx, rows_per_group=seq_len if not is_prompt else x.shape[0],
                         out_dtype=BF16, tm=1024, tn=1024, name="ssd_in_zx_" + tag)
        dt_raw = _mod_matmul(x, mods, gain, w_dt,
                             rows_per_group=seq_len if not is_prompt else x.shape[0],
                             out_dtype=F32, tm=1024, tn=LANES, name="ssd_in_dt_" + tag)
        xbc = _conv_silu(zx, ssd_conv_w[j], ssd_conv_b[j], seq_len=seq_len)
        common = dict(nseq=nseq, seq_len=seq_len, layer=j)
        fwd = _ssd_scan(xbc, dt_raw, dt_bias, a_log, expand[0], reverse=False,
                        h0=None if is_prompt else state_f, emit_state=is_prompt, d_rep=d_rep,
                        name="ssd_scan_fwd_" + tag, **common)
        bwd = _ssd_scan(xbc, dt_raw, dt_bias, a_log, expand[1], reverse=True,
                        h0=None if is_prompt else state_b, emit_state=is_prompt, zx=zx,
                        y_fwd=fwd[0], norm_g=norm_g, name="ssd_scan_bwd_" + tag, **common)
        if is_prompt:
            states = (fwd[1], bwd[1])
        outs.append(_proj_residual(bwd[0], x, mods,
                                   w_out, rows_per_group=seq_len if not is_prompt else x.shape[0],
                                   tm=512, name="ssd_out_" + tag))
    return outs[0], outs[1], states


def kernel(x_prompt, x_sample, c, state_ssd_fwd, state_ssd_bwd, cache_attn_k, cache_attn_v, cache_nat_k, cache_nat_v, c_ctx, ada_w, ada_b, norm_mix, norm_mlp, mlp_w1, mlp_w2, ssd_w_in, ssd_conv_w, ssd_conv_b, ssd_dt_bias, ssd_a_log, ssd_d, ssd_norm, ssd_w_out, attn_w_qkv, attn_q_norm, attn_k_norm, attn_w_out, nat_w_qkv, nat_rpb, nat_w_out, norm_final):
    np_, lp, d = x_prompt.shape
    ns_, ls, _ = x_sample.shape
    mp_rows, ms_rows = np_ * lp, ns_ * ls
    xp = x_prompt.reshape(mp_rows, d)
    xs = x_sample.reshape(ms_rows, d)

    cond_rows = 16
    cond = jnp.concatenate([c_ctx[None, :], c, jnp.zeros((cond_rows - 1 - ns_, d), F32)], axis=0)
    mods_all = _ada_mods(cond, ada_w, ada_b).reshape(DEPTH, cond_rows, 6, d)

    n_ssd = state_ssd_fwd.shape[1]
    st_f = state_ssd_fwd.reshape(ns_, n_ssd, SSD_D_INNER, SSD_STATE)
    st_b = state_ssd_bwd.reshape(ns_, n_ssd, SSD_D_INNER, SSD_STATE)

    sf_out, sb_out, ak_out, av_out, nk_out, nv_out = [], [], [], [], [], []
    for i in range(DEPTH):
        kind, j = i % N_MIXERS, i // N_MIXERS
        mods_p = mods_all[i, 0:1]
        mods_s = mods_all[i, 1:1 + ns_]
        rpg_p, rpg_s = mp_rows, ls
        if kind == 0:
            xp, xs, (hf, hb) = _ssd_layer(
                xp, xs, mods_p, mods_s, norm_mix[i], j, np_, ns_, lp, ls, st_f, st_b,
                ssd_w_in, ssd_conv_w, ssd_conv_b, ssd_dt_bias, ssd_a_log, ssd_d, ssd_norm,
                ssd_w_out)
            sf_out.append(hf.reshape(np_, SSD_HEADS, HEAD_DIM, SSD_STATE))
            sb_out.append(hb.reshape(np_, SSD_HEADS, HEAD_DIM, SSD_STATE))
        elif kind == 1:
            w_qkv = attn_w_qkv[j].astype(BF16)
            w_out = attn_w_out[j].astype(BF16)
            nk = ATTN_KV_HEADS * HEAD_DIM
            qkv = _mod_matmul(xp, mods_p, norm_mix[i], w_qkv, rows_per_group=rpg_p,
                              out_dtype=F32, tm=1024, tn=768, name="gqa_qkv_p")
            q, k, v, kc, vc = _qk_norm(qkv, attn_q_norm[j], attn_k_norm[j], seq_len=lp,
                                       rope=False, emit_cache=True)
            o = _attention(q, 0, k, 0, v, 0, None, None, nb=np_, seq_len=lp, hq=ATTN_HEADS,
                           hkv=ATTN_KV_HEADS, tq=128, tk=lp, q_scale=1.0, name="gqa_attn_p")
            xp = _proj_residual(o, xp, mods_p, w_out, rows_per_group=rpg_p, tm=512,
                                name="gqa_out_p")
            ak_out.append(kc.reshape(np_, lp, ATTN_KV_HEADS, HEAD_DIM))
            av_out.append(vc.reshape(np_, lp, ATTN_KV_HEADS, HEAD_DIM))
            qkv = _mod_matmul(xs, mods_s, norm_mix[i], w_qkv, rows_per_group=rpg_s,
                              out_dtype=F32, tm=1024, tn=768, name="gqa_qkv_s")
            q, k, v = _qk_norm(qkv, attn_q_norm[j], attn_k_norm[j], seq_len=ls, rope=True,
                               emit_cache=False)
            past = cache_attn_k.shape[2]
            ck = cache_attn_k[:, j].reshape(ns_, past, nk).astype(BF16)
            cv = _v_with_ones(cache_attn_v[:, j].reshape(ns_, past, nk)).astype(BF16)
            o = _attention(q, 0, k, 0, v, 0, ck, cv, nb=ns_, seq_len=ls, hq=ATTN_HEADS,
                           hkv=ATTN_KV_HEADS, tq=128, tk=1024, q_scale=1.0, name="gqa_attn_s",
                           v_ones=True)
            xs = _proj_residual(o, xs, mods_s, w_out, rows_per_group=rpg_s, tm=512,
                                name="gqa_out_s")
        else:
            w_qkv = nat_w_qkv[j].astype(BF16)
            w_out = nat_w_out[j].astype(BF16)
            wd = NAT_HEADS * HEAD_DIM
            scale = HEAD_DIM ** -0.5 * LOG2E
            qkv = _mod_matmul(xp, mods_p, norm_mix[i], w_qkv, rows_per_group=rpg_p,
                              out_dtype=F32, tm=1024, tn=1024, name="nat_qkv_p")
            o = _attention(qkv, 0, qkv, 1, qkv, 2, None, None, nb=np_, seq_len=lp, hq=NAT_HEADS,
                           hkv=NAT_HEADS, tq=lp, tk=lp, q_scale=scale, name="nat_attn_p")
            xp = _proj_residual(o, xp, mods_p, w_out, rows_per_group=rpg_p, tm=512,
                                name="nat_out_p")
            nk_out.append(qkv[:, wd:2 * wd].reshape(np_, lp, NAT_HEADS, HEAD_DIM))
            nv_out.append(qkv[:, 2 * wd:].reshape(np_, lp, NAT_HEADS, HEAD_DIM))
            qkv = _mod_matmul(xs, mods_s, norm_mix[i], w_qkv, rows_per_group=rpg_s,
                              out_dtype=BF16, tm=1024, tn=1024, name="nat_qkv_s")
            past = cache_nat_k.shape[2]
            ck = cache_nat_k[:, j].reshape(ns_, past, wd).astype(BF16)
            cv = cache_nat_v[:, j].reshape(ns_, past, wd).astype(BF16)
            o = _nat_latent(qkv, ck, cv, nat_rpb[j], nb=ns_, seq_len=ls)
            xs = _proj_residual(o, xs, mods_s, w_out, rows_per_group=rpg_s, tm=512,
                                name="nat_out_s")
        last = i == DEPTH - 1
        w1 = mlp_w1[i].astype(BF16)
        w2 = mlp_w2[i].astype(BF16)
        xp = _mlp(xp, mods_p, norm_mlp[i], w1, w2, norm_final, rows_per_group=rpg_p,
                  final_norm=last, name="mlp_p")
        xs = _mlp(xs, mods_s, norm_mlp[i], w1, w2, norm_final, rows_per_group=rpg_s,
                  final_norm=last, name="mlp_s")

    return (xp.reshape(np_, lp, d), xs.reshape(ns_, ls, d),
            jnp.stack(sf_out, axis=1), jnp.stack(sb_out, axis=1),
            jnp.stack(ak_out, axis=1), jnp.stack(av_out, axis=1),
            jnp.stack(nk_out, axis=1), jnp.stack(nv_out, axis=1))
```

```python
import functools
import math

import jax
import jax.numpy as jnp
from jax import lax
from jax.experimental import pallas as pl
from jax.experimental.pallas import tpu as pltpu

F32 = jnp.float32
BF16 = jnp.bfloat16

D_MODEL = 1024
DEPTH = 4
GRID_W = 64
N_MIXERS = 3
EPS = 1e-6
NEG_INF = -1e30
HEAD_DIM = 64
LANES = 128
LOG2E = math.log2(math.e)
SSD_D_INNER = 2 * D_MODEL
SSD_HEADS = SSD_D_INNER // HEAD_DIM
SSD_GROUPS = 4
SSD_HPG = SSD_HEADS // SSD_GROUPS
SSD_STATE = 128
SSD_CONV_W = 5
SSD_CHUNK = 128
SSD_CONV_DIM = SSD_D_INNER + 2 * SSD_GROUPS * SSD_STATE
SSD_ZX_DIM = SSD_D_INNER + SSD_CONV_DIM
ATTN_HEADS = D_MODEL // HEAD_DIM
ATTN_KV_HEADS = 4
ROPE_BASE = 10000.0
NAT_HEADS = D_MODEL // HEAD_DIM
NAT_KH = 8
NAT_KW = 16
MLP_HIDDEN = 4 * D_MODEL

VMEM_LIMIT_BYTES = 48 * 1024 * 1024
NAT_VMEM_LIMIT_BYTES = 56 * 1024 * 1024


def _params(semantics, vmem=VMEM_LIMIT_BYTES):
    return pltpu.CompilerParams(dimension_semantics=semantics, vmem_limit_bytes=vmem)


def _sigmoid(x):
    return 1.0 / (1.0 + jnp.exp(-x))


def _modulated_norm(x, gain, shift, scale):
    ms = jnp.mean(x * x, axis=-1, keepdims=True)
    return (x * lax.rsqrt(ms + EPS) * gain) * (1.0 + scale) + shift


def _ada_kernel(c_ref, w_ref, b_ref, o_ref):
    c = c_ref[...]
    a = (c * _sigmoid(c)).astype(BF16)
    o_ref[...] = jnp.dot(a, w_ref[...].astype(BF16), preferred_element_type=F32) + b_ref[...]


def _ada_mods(cond, ada_w, ada_b):
    depth, d, n = ada_w.shape
    r = cond.shape[0]
    tn = 1024
    return pl.pallas_call(
        _ada_kernel,
        out_shape=jax.ShapeDtypeStruct((depth, r, n), F32),
        grid=(depth, n // tn),
        in_specs=[pl.BlockSpec((r, d), lambda l, j: (0, 0)),
                  pl.BlockSpec((None, d, tn), lambda l, j: (l, 0, j)),
                  pl.BlockSpec((None, 1, tn), lambda l, j: (l, 0, j))],
        out_specs=pl.BlockSpec((None, r, tn), lambda l, j: (l, 0, j)),
        compiler_params=_params(("parallel", "parallel")),
        name="ada_mods",
    )(cond, ada_w, ada_b.reshape(depth, 1, n))


def _modmm_kernel(x_ref, mod_ref, g_ref, w_ref, o_ref, h_ref, *, shift_row, scale_row):
    @pl.when(pl.program_id(1) == 0)
    def _():
        h = _modulated_norm(x_ref[...], g_ref[...],
                            mod_ref[shift_row:shift_row + 1, :],
                            mod_ref[scale_row:scale_row + 1, :])
        h_ref[...] = h.astype(BF16)

    o_ref[...] = jnp.dot(h_ref[...], w_ref[...],
                         preferred_element_type=F32).astype(o_ref.dtype)


def _mod_matmul(x, mods, gain, w, *, rows_per_group, out_dtype, tm, tn, name):
    m, d = x.shape
    n = w.shape[1]
    tpg = rows_per_group // tm
    return pl.pallas_call(
        functools.partial(_modmm_kernel, shift_row=0, scale_row=1),
        out_shape=jax.ShapeDtypeStruct((m, n), out_dtype),
        grid=(m // tm, n // tn),
        in_specs=[pl.BlockSpec((tm, d), lambda i, j: (i, 0)),
                  pl.BlockSpec((None, 6, d), lambda i, j: (i // tpg, 0, 0)),
                  pl.BlockSpec((1, d), lambda i, j: (0, 0)),
                  pl.BlockSpec((d, tn), lambda i, j: (0, j))],
        out_specs=pl.BlockSpec((tm, tn), lambda i, j: (i, j)),
        scratch_shapes=[pltpu.VMEM((tm, d), BF16)],
        compiler_params=_params(("parallel", "arbitrary")),
        name=name,
    )(x, mods, gain.reshape(1, d), w)


def _proj_res_kernel(a_ref, x_ref, mod_ref, w_ref, o_ref, *, gate_row):
    y = jnp.dot(a_ref[...], w_ref[...], preferred_element_type=F32)
    o_ref[...] = x_ref[...] + mod_ref[gate_row:gate_row + 1, :] * y


def _proj_residual(a, x, mods, w, *, rows_per_group, tm, name):
    m, k = a.shape
    d = x.shape[1]
    tpg = rows_per_group // tm
    return pl.pallas_call(
        functools.partial(_proj_res_kernel, gate_row=2),
        out_shape=jax.ShapeDtypeStruct((m, d), F32),
        grid=(m // tm,),
        in_specs=[pl.BlockSpec((tm, k), lambda i: (i, 0)),
                  pl.BlockSpec((tm, d), lambda i: (i, 0)),
                  pl.BlockSpec((None, 6, d), lambda i: (i // tpg, 0, 0)),
                  pl.BlockSpec((k, d), lambda i: (0, 0))],
        out_specs=pl.BlockSpec((tm, d), lambda i: (i, 0)),
        compiler_params=_params(("parallel",)),
        name=name,
    )(a, x, mods, w)


def _mlp_kernel(x_ref, mod_ref, g_ref, w1_ref, w2_ref, gf_ref, o_ref, h_ref, acc_ref,
                *, final_norm):
    j = pl.program_id(1)

    @pl.when(j == 0)
    def _():
        h = _modulated_norm(x_ref[...], g_ref[...], mod_ref[3:4, :], mod_ref[4:5, :])
        h_ref[...] = h.astype(BF16)
        acc_ref[...] = jnp.zeros_like(acc_ref)

    a = jnp.dot(h_ref[...], w1_ref[...], preferred_element_type=F32)
    a = jnp.maximum(a, 0.0)
    acc_ref[...] += jnp.dot((a * a).astype(BF16), w2_ref[...], preferred_element_type=F32)

    @pl.when(j == pl.num_programs(1) - 1)
    def _():
        y = x_ref[...] + mod_ref[5:6, :] * acc_ref[...]
        if final_norm:
            ms = jnp.mean(y * y, axis=-1, keepdims=True)
            y = y * lax.rsqrt(ms + EPS) * gf_ref[...]
        o_ref[...] = y


def _mlp(x, mods, gain, w1, w2, gain_final, *, rows_per_group, final_norm, name):
    m, d = x.shape
    hdim = w1.shape[1]
    tm, th = 1024, 1024
    tpg = rows_per_group // tm
    return pl.pallas_call(
        functools.partial(_mlp_kernel, final_norm=final_norm),
        out_shape=jax.ShapeDtypeStruct((m, d), F32),
        grid=(m // tm, hdim // th),
        in_specs=[pl.BlockSpec((tm, d), lambda i, j: (i, 0)),
                  pl.BlockSpec((None, 6, d), lambda i, j: (i // tpg, 0, 0)),
                  pl.BlockSpec((1, d), lambda i, j: (0, 0)),
                  pl.BlockSpec((d, th), lambda i, j: (0, j)),
                  pl.BlockSpec((th, d), lambda i, j: (j, 0)),
                  pl.BlockSpec((1, d), lambda i, j: (0, 0))],
        out_specs=pl.BlockSpec((tm, d), lambda i, j: (i, 0)),
        scratch_shapes=[pltpu.VMEM((tm, d), BF16), pltpu.VMEM((tm, d), F32)],
        compiler_params=_params(("parallel", "arbitrary")),
        name=name,
    )(x, mods, gain.reshape(1, d), w1, w2, gain_final.reshape(1, d))


CONV_HALO_ROWS = 16


CONV_TAPS_SHIFTED = (0, 1, 3, 4)


def _conv_kernel(prev_ref, cur_ref, next_ref, sh_ref, w_ref, b_ref, o_ref, *, tiles_per_seq):
    pos = pl.program_id(0) % tiles_per_seq
    cur_b = cur_ref[...]
    cur = cur_b.astype(F32)
    tr = cur.shape[0]
    h = CONV_HALO_ROWS
    w = w_ref[...]
    bias = b_ref[...]

    def silu(y):
        return (y / (1.0 + jnp.exp2(y * (-LOG2E)))).astype(o_ref.dtype)

    y = bias + w[2:3] * cur
    for idx, k in enumerate(CONV_TAPS_SHIFTED):
        y = y + w[k:k + 1] * jnp.dot(sh_ref[idx], cur_b, preferred_element_type=F32)
    o_ref[...] = silu(y)

    def edge(x):
        ye = bias + w[2:3] * x
        for k in CONV_TAPS_SHIFTED:
            ye = ye + w[k:k + 1] * pltpu.roll(x, (SSD_CONV_W // 2 - k) % x.shape[0], 0)
        return silu(ye)[h:2 * h]

    pv = jnp.where(pos == 0, 0.0, prev_ref[...].astype(F32))
    nx = jnp.where(pos == tiles_per_seq - 1, 0.0, next_ref[...].astype(F32))
    o_ref[0:h, :] = edge(jnp.concatenate([pv, cur[0:2 * h]], axis=0))
    o_ref[tr - h:tr, :] = edge(jnp.concatenate([cur[tr - 2 * h:tr], nx], axis=0))


def _conv_silu(zx, conv_w, conv_b, *, seq_len):
    m = zx.shape[0]
    tr, tc = 256, 1024
    c0 = SSD_D_INNER // tc
    halo_per_tile = tr // CONV_HALO_ROWS
    n_halo = m // CONV_HALO_ROWS
    t = jnp.arange(tr)
    shifts = jnp.stack([(t[None, :] == t[:, None] + (k - SSD_CONV_W // 2)).astype(BF16)
                        for k in CONV_TAPS_SHIFTED])
    return pl.pallas_call(
        functools.partial(_conv_kernel, tiles_per_seq=seq_len // tr),
        out_shape=jax.ShapeDtypeStruct((m, SSD_CONV_DIM), BF16),
        grid=(m // tr, SSD_CONV_DIM // tc),
        in_specs=[
            pl.BlockSpec((CONV_HALO_ROWS, tc),
                         lambda i, j: (jnp.maximum(i * halo_per_tile - 1, 0), c0 + j)),
            pl.BlockSpec((tr, tc), lambda i, j: (i, c0 + j)),
            pl.BlockSpec((CONV_HALO_ROWS, tc),
                         lambda i, j: (jnp.minimum((i + 1) * halo_per_tile, n_halo - 1), c0 + j)),
            pl.BlockSpec((len(CONV_TAPS_SHIFTED), tr, tr), lambda i, j: (0, 0, 0)),
            pl.BlockSpec((SSD_CONV_W, tc), lambda i, j: (0, j)),
            pl.BlockSpec((1, tc), lambda i, j: (0, j))],
        out_specs=pl.BlockSpec((tr, tc), lambda i, j: (i, j)),
        compiler_params=_params(("parallel", "parallel")),
        name="ssd_conv_silu",
    )(zx, zx, zx, shifts, conv_w, conv_b.reshape(1, SSD_CONV_DIM))


def _split3(a):
    a1 = a.astype(BF16)
    r1 = a - a1.astype(F32)
    a2 = r1.astype(BF16)
    a3 = (r1 - a2.astype(F32)).astype(BF16)
    return a1, a2, a3


def _dot_exact_left(m01, a):
    return sum(jnp.dot(m01, p, preferred_element_type=F32) for p in _split3(a))


def _spread_heads(a, e2_ref):
    hi = a.astype(BF16)
    lo = (a - hi.astype(F32)).astype(BF16)
    return jnp.dot(jnp.concatenate([hi, lo], axis=1), e2_ref[...], preferred_element_type=F32)


SSD_SEQS_PER_STEP = 2


def _ssd_decays(dt_ref, dtb_ref, alog_ref, *, reverse):
    q = SSD_CHUNK
    pre = dt_ref[...] + dtb_ref[...]
    dt = jnp.maximum(pre, 0.0) + jnp.log1p(jnp.exp(-jnp.abs(pre)))
    a_dt = dt * (-jnp.exp(alog_ref[...]) * LOG2E)
    ri = lax.broadcasted_iota(jnp.int32, (q, q), 0)
    ci = lax.broadcasted_iota(jnp.int32, (q, q), 1)
    keep = (ri <= ci) if reverse else (ri >= ci)
    cs = _dot_exact_left(jnp.where(keep, 1.0, 0.0).astype(BF16), a_dt)
    edge = cs[0:1, :] if reverse else cs[q - 1:q, :]
    factors = jnp.concatenate([jnp.exp2(cs), jnp.exp2(edge - cs) * dt], axis=0)
    return dict(keep=keep, cs=cs, cs_t=cs.T, dt_t=dt.T, factors=factors)


def _ssd_chunk(xbc_ref, decays, exp_cs, to_end, z_ref, yf_ref, ng_ref, drep_ref,
               y_ref, s_ref, ybuf_ref, *, reverse, combine):
    q = SSD_CHUNK
    gw = SSD_HPG * HEAD_DIM
    col0 = SSD_HEADS if reverse else 0
    keep, cs, cs_t, dt_t = decays["keep"], decays["cs"], decays["cs_t"], decays["dt_t"]
    exp_edge = exp_cs[0:1, :] if reverse else exp_cs[q - 1:q, :]
    lane = lax.broadcasted_iota(jnp.int32, (1, LANES), 1)
    keep_lo = jnp.where(lane < HEAD_DIM, 1.0, 0.0).astype(BF16)
    keep_hi = jnp.where(lane < HEAD_DIM, 0.0, 1.0).astype(BF16)

    for g in range(SSD_GROUPS):
        gs = slice(g * gw, (g + 1) * gw)
        b_g = xbc_ref[:, SSD_D_INNER + g * SSD_STATE:SSD_D_INNER + (g + 1) * SSD_STATE]
        c_off = SSD_D_INNER + SSD_GROUPS * SSD_STATE
        c_g = xbc_ref[:, c_off + g * SSD_STATE:c_off + (g + 1) * SSD_STATE]
        cb = lax.dot_general(c_g, b_g, (((1,), (1,)), ((), ())), preferred_element_type=F32)
        cb = jnp.where(keep, cb, 0.0)
        s_g = s_ref[:, gs]
        y_off = jnp.dot(c_g, s_g.astype(BF16), preferred_element_type=F32)
        for mp in range(gw // LANES):
            m = g * (gw // LANES) + mp
            ls = slice(m * LANES, (m + 1) * LANES)
            x_pair = xbc_ref[:, ls]
            k0 = col0 + 2 * m
            ws = []
            for k in (k0, k0 + 1):
                seg = jnp.minimum(cs[:, k:k + 1] - cs_t[k:k + 1, :], 0.0)
                ws.append((jnp.exp2(seg) * cb * dt_t[k:k + 1, :]).astype(BF16))
            y_pair = (jnp.dot(jnp.concatenate(ws, axis=1),
                              jnp.concatenate([x_pair * keep_lo, x_pair * keep_hi], axis=0),
                              preferred_element_type=F32)
                      + y_off[:, mp * LANES:(mp + 1) * LANES] * exp_cs[:, ls])
            if not combine:
                y_pair = y_pair + drep_ref[:, ls] * x_pair.astype(F32)
            ybuf_ref[:, ls] = y_pair
        xs_g = (to_end[:, gs] * xbc_ref[:, gs].astype(F32)).astype(BF16)
        b_t = b_g.astype(F32).T.astype(BF16)
        s_ref[:, gs] = s_g * exp_edge[:, gs] + jnp.dot(b_t, xs_g, preferred_element_type=F32)

    if combine:
        z = z_ref[...].astype(F32)
        gated = (ybuf_ref[...] + yf_ref[...]) * (z * _sigmoid(z))
        ms = jnp.mean(gated * gated, axis=-1, keepdims=True)
        y_ref[...] = (gated * lax.rsqrt(ms + EPS) * ng_ref[...]).astype(y_ref.dtype)


def _ssd_kernel(*refs, reverse, has_h0, emit_state, combine):
    it = iter(refs)
    xbc_ref, dt_ref, dtb_ref, alog_ref, e_ref = (next(it) for _ in range(5))
    h0_ref = next(it) if has_h0 else None
    z_ref = yf_ref = ng_ref = drep_ref = None
    if combine:
        z_ref, yf_ref, ng_ref = next(it), next(it), next(it)
    else:
        drep_ref = next(it)
    y_ref = next(it)
    st_ref = next(it) if emit_state else None
    s_ref = next(it)
    ybuf_ref = next(it) if combine else y_ref
    q = SSD_CHUNK
    c = pl.program_id(1)
    seqs = range(SSD_SEQS_PER_STEP)

    @pl.when(c == 0)
    def _():
        if has_h0:
            for sq in seqs:
                for k in range(SSD_D_INNER // q):
                    s_ref[sq, :, k * q:(k + 1) * q] = h0_ref[sq, k * q:(k + 1) * q, :].T
        else:
            s_ref[...] = jnp.zeros_like(s_ref)

    decays = [_ssd_decays(dt_ref.at[sq], dtb_ref, alog_ref, reverse=reverse) for sq in seqs]
    spread = _spread_heads(jnp.concatenate([d["factors"] for d in decays], axis=0), e_ref)
    for sq in seqs:
        r0 = sq * 2 * q
        _ssd_chunk(xbc_ref.at[sq], decays[sq], spread[r0:r0 + q], spread[r0 + q:r0 + 2 * q],
                   z_ref.at[sq] if combine else None, yf_ref.at[sq] if combine else None,
                   ng_ref, drep_ref, y_ref.at[sq], s_ref.at[sq], ybuf_ref.at[sq],
                   reverse=reverse, combine=combine)

    if emit_state:
        @pl.when(c == pl.num_programs(1) - 1)
        def _():
            for sq in seqs:
                for k in range(SSD_D_INNER // q):
                    st_ref[sq, k * q:(k + 1) * q, :] = s_ref[sq, :, k * q:(k + 1) * q].T


def _ssd_scan(xbc, dt_raw, dt_bias, a_log, expand, *, nseq, seq_len, reverse, h0=None,
              layer=0, emit_state=False, zx=None, y_fwd=None, norm_g=None, d_rep=None, name):
    combine = zx is not None
    nchunk = seq_len // SSD_CHUNK
    q = SSD_CHUNK
    sp = SSD_SEQS_PER_STEP
    nb = nseq // sp

    def per_seq(a):
        return a.reshape(nb, sp, seq_len, a.shape[-1])

    def chunk_spec(width):
        return pl.BlockSpec((None, sp, q, width),
                            lambda b, c: (b, 0, (nchunk - 1 - c) if reverse else c, 0))

    def const_spec(shape):
        return pl.BlockSpec(shape, lambda b, c: (0,) * len(shape))

    in_specs = [chunk_spec(SSD_CONV_DIM), chunk_spec(LANES), const_spec((1, LANES)),
                const_spec((1, LANES)), const_spec((2 * LANES, SSD_D_INNER))]
    args = [per_seq(xbc), per_seq(dt_raw), dt_bias, a_log, expand]
    if h0 is not None:
        in_specs.append(pl.BlockSpec((None, sp, None, SSD_D_INNER, SSD_STATE),
                                     lambda b, c: (b, 0, layer, 0, 0)))
        args.append(h0.reshape((nb, sp) + h0.shape[1:]))
    if combine:
        in_specs += [chunk_spec(SSD_D_INNER), chunk_spec(SSD_D_INNER),
                     const_spec((1, SSD_D_INNER))]
        args += [per_seq(zx), per_seq(y_fwd), norm_g]
    else:
        in_specs.append(const_spec((1, SSD_D_INNER)))
        args.append(d_rep)
    out_shape = [jax.ShapeDtypeStruct((nb, sp, seq_len, SSD_D_INNER), BF16 if combine else F32)]
    out_specs = [chunk_spec(SSD_D_INNER)]
    if emit_state:
        out_shape.append(jax.ShapeDtypeStruct((nb, sp, SSD_D_INNER, SSD_STATE), F32))
        out_specs.append(pl.BlockSpec((None, sp, SSD_D_INNER, SSD_STATE),
                                      lambda b, c: (b, 0, 0, 0)))
    scratch = [pltpu.VMEM((sp, SSD_STATE, SSD_D_INNER), F32)]
    if combine:
        scratch.append(pltpu.VMEM((sp, q, SSD_D_INNER), F32))
    outs = list(pl.pallas_call(
        functools.partial(_ssd_kernel, reverse=reverse, has_h0=h0 is not None,
                          emit_state=emit_state, combine=combine),
        out_shape=out_shape,
        grid=(nb, nchunk),
        in_specs=in_specs,
        out_specs=out_specs,
        scratch_shapes=scratch,
        compiler_params=_params(("parallel", "arbitrary")),
        name=name,
    )(*args))
    outs[0] = outs[0].reshape(nseq * seq_len, SSD_D_INNER)
    if emit_state:
        outs[1] = outs[1].reshape(nseq, SSD_D_INNER, SSD_STATE)
    return outs


def _head_rms(x, gain_row):
    rows, width = x.shape
    lane = lax.broadcasted_iota(jnp.int32, (rows, LANES), 1)
    lo = lane < HEAD_DIM
    outs = []
    for j in range(width // LANES):
        xb = x[:, j * LANES:(j + 1) * LANES]
        sq = xb * xb
        s_lo = jnp.sum(jnp.where(lo, sq, 0.0), axis=-1, keepdims=True)
        s_hi = jnp.sum(jnp.where(lo, 0.0, sq), axis=-1, keepdims=True)
        ms = jnp.where(lo, s_lo, s_hi) * (1.0 / HEAD_DIM)
        outs.append(xb * lax.rsqrt(ms + EPS))
    return jnp.concatenate(outs, axis=1) * gain_row


def _rope(x, cos, sin_signed):
    rows, width = x.shape
    lane = lax.broadcasted_iota(jnp.int32, (rows, LANES), 1)
    even = (lane % 2) == 0
    outs = []
    for j in range(width // LANES):
        xb = x[:, j * LANES:(j + 1) * LANES]
        partner = jnp.where(even, pltpu.roll(xb, LANES - 1, 1), pltpu.roll(xb, 1, 1))
        outs.append(xb * cos + partner * sin_signed)
    return jnp.concatenate(outs, axis=1)


def _qknorm_kernel(*refs, rope, emit_cache):
    it = iter(refs)
    q_ref, k_ref, v_ref, qg_ref, kg_ref = (next(it) for _ in range(5))
    if rope:
        cos_ref, sin_ref = next(it), next(it)
    qo_ref, ko_ref, vo_ref = next(it), next(it), next(it)
    q = _head_rms(q_ref[...], qg_ref[...])
    k = _head_rms(k_ref[...], kg_ref[...])
    if emit_cache:
        kc_ref, vc_ref = next(it), next(it)
        kc_ref[...] = k
        vc_ref[...] = v_ref[...]
    if rope:
        q = _rope(q, cos_ref[...], sin_ref[...])
        k = _rope(k, cos_ref[...], sin_ref[...])
    qo_ref[...] = (q * (HEAD_DIM ** -0.5 * LOG2E)).astype(BF16)
    ko_ref[...] = k.astype(BF16)
    v = v_ref[...]
    vo_ref[...] = (_v_with_ones(v) if rope else v).astype(BF16)


def _rope_tables(seq_len):
    n_pairs = HEAD_DIM // 4
    t = jnp.arange(seq_len)
    row = (t // GRID_W).astype(F32)
    col = (t % GRID_W).astype(F32)
    freqs = ROPE_BASE ** (-jnp.arange(n_pairs, dtype=F32) / n_pairs)
    ang = jnp.concatenate([row[:, None] * freqs, col[:, None] * freqs], axis=-1)
    cos = jnp.repeat(jnp.cos(ang), 2, axis=-1)
    sin = jnp.repeat(jnp.sin(ang), 2, axis=-1)
    sign = jnp.where(jnp.arange(HEAD_DIM) % 2 == 0, -1.0, 1.0).astype(F32)
    return jnp.tile(cos, (1, 2)), jnp.tile(sin * sign, (1, 2))


def _qk_norm(qkv, q_gain, k_gain, *, seq_len, rope, emit_cache):
    m = qkv.shape[0]
    tm = 256
    nq = ATTN_HEADS * HEAD_DIM
    nk = ATTN_KV_HEADS * HEAD_DIM
    qg = jnp.tile(q_gain, ATTN_HEADS).reshape(1, nq)
    kg = jnp.tile(k_gain, ATTN_KV_HEADS).reshape(1, nk)
    in_specs = [pl.BlockSpec((tm, nq), lambda i: (i, 0)),
                pl.BlockSpec((tm, nk), lambda i: (i, nq // nk)),
                pl.BlockSpec((tm, nk), lambda i: (i, nq // nk + 1)),
                pl.BlockSpec((1, nq), lambda i: (0, 0)),
                pl.BlockSpec((1, nk), lambda i: (0, 0))]
    args = [qkv, qkv, qkv, qg, kg]
    if rope:
        cos, sin = _rope_tables(seq_len)
        tps = seq_len // tm
        in_specs += [pl.BlockSpec((tm, LANES), lambda i: (i % tps, 0)),
                     pl.BlockSpec((tm, LANES), lambda i: (i % tps, 0))]
        args += [cos, sin]
    out_shape = [jax.ShapeDtypeStruct((m, nq), BF16), jax.ShapeDtypeStruct((m, nk), BF16),
                 jax.ShapeDtypeStruct((m, nk), BF16)]
    out_specs = [pl.BlockSpec((tm, nq), lambda i: (i, 0)), pl.BlockSpec((tm, nk), lambda i: (i, 0)),
                 pl.BlockSpec((tm, nk), lambda i: (i, 0))]
    if rope:
        out_shape[2] = jax.ShapeDtypeStruct((m, ATTN_KV_HEADS * LANES), BF16)
        out_specs[2] = pl.BlockSpec((tm, ATTN_KV_HEADS * LANES), lambda i: (i, 0))
    if emit_cache:
        out_shape += [jax.ShapeDtypeStruct((m, nk), F32)] * 2
        out_specs += [pl.BlockSpec((tm, nk), lambda i: (i, 0))] * 2
    return pl.pallas_call(
        functools.partial(_qknorm_kernel, rope=rope, emit_cache=emit_cache),
        out_shape=out_shape,
        grid=(m // tm,),
        in_specs=in_specs,
        out_specs=out_specs,
        compiler_params=_params(("parallel",)),
        name="gqa_qk_norm_rope" if rope else "gqa_qk_norm",
    )(*args)


def _place_half(x, lane, src_half, dst_half):
    if src_half != dst_half:
        x = pltpu.roll(x, HEAD_DIM, 1)
    keep = (lane >= HEAD_DIM) if dst_half else (lane < HEAD_DIM)
    return jnp.where(keep, x, 0.0)


def _v_with_ones(v):
    lane = lax.broadcasted_iota(jnp.int32, v.shape[:-1] + (LANES,), v.ndim - 1)
    tiles = []
    for g in range(v.shape[-1] // HEAD_DIM):
        pair = v[..., (g // 2) * LANES:(g // 2 + 1) * LANES]
        own = (lane >= HEAD_DIM) if g % 2 else (lane < HEAD_DIM)
        tiles.append(jnp.where(own, pair, 1.0))
    return jnp.concatenate(tiles, axis=-1)


def _attn_kernel(*refs, hq, hkv, tk, has_ctx, q_scale, v_ones):
    if has_ctx:
        q_ref, k_ref, v_ref, ck_ref, cv_ref, o_ref = refs
    else:
        q_ref, k_ref, v_ref, o_ref = refs
    grp = hq // hkv
    tq = q_ref.shape[0]
    rows = grp * tq
    lane = lax.broadcasted_iota(jnp.int32, (tq, LANES), 1)
    nt = (((1,), (1,)), ((), ()))

    qst = []
    for g in range(hkv):
        parts = []
        for u in range(grp):
            h = g * grp + u
            qb = q_ref[:, (h // 2) * LANES:(h // 2 + 1) * LANES].astype(F32)
            if q_scale != 1.0:
                qb = qb * q_scale
            parts.append(_place_half(qb, lane, h % 2, g % 2).astype(BF16))
        qst.append(parts[0] if grp == 1 else jnp.concatenate(parts, axis=0))

    def step(g, kblk, vblk, carry):
        m_i, l_i, acc = carry
        s = lax.dot_general(qst[g], kblk.astype(BF16), nt, preferred_element_type=F32)
        m_n = jnp.maximum(m_i, jnp.max(s, axis=-1, keepdims=True))
        alpha = jnp.exp2(m_i - m_n)
        p = jnp.exp2(s - m_n)
        acc = alpha * acc + jnp.dot(p.astype(BF16), vblk.astype(BF16),
                                    preferred_element_type=F32)
        if v_ones:
            return m_n, l_i, acc
        return m_n, alpha * l_i + jnp.sum(p, axis=-1, keepdims=True), acc

    def all_heads(kv_at, carries):
        out = []
        for g in range(hkv):
            kc = slice((g // 2) * LANES, (g // 2 + 1) * LANES)
            vc = slice(g * LANES, (g + 1) * LANES) if v_ones else kc
            out.append(step(g, *kv_at(kc, vc), carries[g]))
        return tuple(out)

    carries = tuple((jnp.full((rows, 1), -jnp.inf, F32), jnp.zeros((rows, 1), F32),
                     jnp.zeros((rows, LANES), F32)) for _ in range(hkv))
    n_self = k_ref.shape[0] // tk
    if n_self == 1:
        carries = all_heads(lambda kc, vc: (k_ref[:, kc], v_ref[:, vc]), carries)
    else:
        def body(ci, car):
            r0 = pl.multiple_of(ci * tk, tk)
            return all_heads(lambda kc, vc: (k_ref[pl.ds(r0, tk), kc], v_ref[pl.ds(r0, tk), vc]),
                             car)
        carries = lax.fori_loop(0, n_self, body, carries, unroll=2)
    if has_ctx:
        carries = all_heads(lambda kc, vc: (ck_ref[:, kc], cv_ref[:, vc]), carries)

    head_out = [None] * hq
    for g in range(hkv):
        _, l_f, acc = carries[g]
        if v_ones:
            ones_lane = HEAD_DIM * (1 - g % 2)
            l_f = acc[:, ones_lane:ones_lane + 1]
        o = acc / l_f
        for u in range(grp):
            head_out[g * grp + u] = o[u * tq:(u + 1) * tq]
    for mpair in range(hq // 2):
        oa, ob = head_out[2 * mpair], head_out[2 * mpair + 1]
        if ((2 * mpair) // grp) % 2 != 0:
            oa = pltpu.roll(oa, HEAD_DIM, 1)
        if ((2 * mpair + 1) // grp) % 2 != 1:
            ob = pltpu.roll(ob, HEAD_DIM, 1)
        o_ref[:, mpair * LANES:(mpair + 1) * LANES] = jnp.where(
            lane < HEAD_DIM, oa, ob).astype(o_ref.dtype)


def _attention(q_arr, q_col, k_arr, k_col, v_arr, v_col, ctx_k, ctx_v, *, nb, seq_len,
               hq, hkv, tq, tk, q_scale, name, v_ones=False):
    wq, wk = hq * HEAD_DIM, hkv * HEAD_DIM
    wv = hkv * LANES if v_ones else wk
    qt = seq_len // tq
    in_specs = [pl.BlockSpec((tq, wq), lambda b, i: (b * qt + i, q_col)),
                pl.BlockSpec((seq_len, wk), lambda b, i: (b, k_col)),
                pl.BlockSpec((seq_len, wv), lambda b, i: (b, v_col))]
    args = [q_arr, k_arr, v_arr]
    has_ctx = ctx_k is not None
    if has_ctx:
        lc = ctx_k.shape[1]
        in_specs += [pl.BlockSpec((None, lc, wk), lambda b, i: (b, 0, 0)),
                     pl.BlockSpec((None, lc, wv), lambda b, i: (b, 0, 0))]
        args += [ctx_k, ctx_v]
    return pl.pallas_call(
        functools.partial(_attn_kernel, hq=hq, hkv=hkv, tk=tk, has_ctx=has_ctx, q_scale=q_scale,
                          v_ones=v_ones),
        out_shape=jax.ShapeDtypeStruct((nb * seq_len, wq), BF16),
        grid=(nb, qt),
        in_specs=in_specs,
        out_specs=pl.BlockSpec((tq, wq), lambda b, i: (b * qt + i, 0)),
        compiler_params=_params(("parallel", "arbitrary")),
        name=name,
    )(*args)


NAT_ROWS_PER_STEP = 2


def _nat_kernel(q_ref, k_ref, v_ref, ck_ref, cv_ref, *rest, rows):
    bias_refs, o_ref = rest[:NAT_ROWS_PER_STEP], rest[NAT_ROWS_PER_STEP]
    kh, w = NAT_KH, GRID_W
    nloc = kh * w
    pair_rows = 2 * w
    lane = lax.broadcasted_iota(jnp.int32, (w, LANES), 1)
    nt = (((1,), (1,)), ((), ()))
    q_scale = (HEAD_DIM ** -0.5) * LOG2E
    for mpair in range(NAT_HEADS // 2):
        ls = slice(mpair * LANES, (mpair + 1) * LANES)
        k_ctx = ck_ref[:, ls]
        v_ctx = cv_ref[:, ls]
        qs = []
        for j in range(NAT_ROWS_PER_STEP):
            qb = q_ref[j * w:(j + 1) * w, ls].astype(F32) * q_scale
            qs.append(jnp.concatenate([_place_half(qb, lane, 0, 0), _place_half(qb, lane, 1, 1)],
                                      axis=0).astype(BF16))
        s_ctx_all = lax.dot_general(jnp.concatenate(qs, axis=0), k_ctx, nt,
                                    preferred_element_type=F32)
        p_ctx, o_loc, denom = [], [], []
        for j in range(NAT_ROWS_PER_STEP):
            r = pl.program_id(1) * NAT_ROWS_PER_STEP + j
            k0 = pl.multiple_of(jnp.clip(r - kh // 2, 0, rows - kh) * w, w)
            s_loc = (lax.dot_general(qs[j], k_ref[pl.ds(k0, nloc), ls], nt,
                                     preferred_element_type=F32) + bias_refs[j][mpair])
            s_ctx = s_ctx_all[j * pair_rows:(j + 1) * pair_rows]
            mx = jnp.maximum(jnp.max(s_loc, axis=-1, keepdims=True),
                             jnp.max(s_ctx, axis=-1, keepdims=True))
            p_loc = jnp.exp2(s_loc - mx)
            pc = jnp.exp2(s_ctx - mx)
            denom.append(jnp.sum(p_loc, axis=-1, keepdims=True)
                         + jnp.sum(pc, axis=-1, keepdims=True))
            o_loc.append(jnp.dot(p_loc.astype(BF16), v_ref[pl.ds(k0, nloc), ls],
                                 preferred_element_type=F32))
            p_ctx.append(pc.astype(BF16))
        o_ctx_all = jnp.dot(jnp.concatenate(p_ctx, axis=0), v_ctx, preferred_element_type=F32)
        for j in range(NAT_ROWS_PER_STEP):
            o = (o_loc[j] + o_ctx_all[j * pair_rows:(j + 1) * pair_rows]) / denom[j]
            o_ref[j * w:(j + 1) * w, ls] = jnp.where(lane < HEAD_DIM, o[:w], o[w:]).astype(o_ref.dtype)


def _nat_bias(rpb):
    col = jnp.arange(GRID_W)
    cstart = jnp.clip(col - NAT_KW // 2, 0, GRID_W - NAT_KW)
    col_ok = (col[None, :] >= cstart[:, None]) & (col[None, :] < cstart[:, None] + NAT_KW)
    dc_idx = jnp.clip(col[None, :] - col[:, None] + NAT_KW - 1, 0, 2 * NAT_KW - 2)
    rpb_c = jnp.where(col_ok[None, None], rpb[:, :, dc_idx] * LOG2E, NEG_INF)
    variants = []
    for d0 in range(NAT_KH):
        blk = rpb_c[:, d0:d0 + NAT_KH].transpose(0, 2, 1, 3)
        variants.append(blk.reshape(NAT_HEADS // 2, 2 * GRID_W, NAT_KH * GRID_W))
    return jnp.stack(variants, axis=0)


def _nat_latent(qkv, ctx_k, ctx_v, rpb, *, nb, seq_len):
    rows = seq_len // GRID_W
    kh = NAT_KH
    rb = NAT_ROWS_PER_STEP
    bias = _nat_bias(rpb)
    lc = ctx_k.shape[1]
    wd = NAT_HEADS * HEAD_DIM
    steps = rows // rb

    def bias_spec(j):
        def idx(b, i):
            r = i * rb + j
            return (jnp.clip(r - kh // 2, 0, rows - kh) - r + kh - 1, 0, 0, 0)
        return pl.BlockSpec((None, NAT_HEADS // 2, 2 * GRID_W, kh * GRID_W), idx)

    return pl.pallas_call(
        functools.partial(_nat_kernel, rows=rows),
        out_shape=jax.ShapeDtypeStruct((nb * seq_len, wd), BF16),
        grid=(nb, steps),
        in_specs=[pl.BlockSpec((rb * GRID_W, wd), lambda b, i: (b * steps + i, 0)),
                  pl.BlockSpec((seq_len, wd), lambda b, i: (b, 1)),
                  pl.BlockSpec((seq_len, wd), lambda b, i: (b, 2)),
                  pl.BlockSpec((None, lc, wd), lambda b, i: (b, 0, 0)),
                  pl.BlockSpec((None, lc, wd), lambda b, i: (b, 0, 0))]
                 + [bias_spec(j) for j in range(rb)],
        out_specs=pl.BlockSpec((rb * GRID_W, wd), lambda b, i: (b * steps + i, 0)),
        compiler_params=_params(("parallel", "arbitrary"), NAT_VMEM_LIMIT_BYTES),
        name="nat_latent",
    )(qkv, qkv, qkv, ctx_k, ctx_v, *([bias] * rb))


def _ssd_layer(xp, xs, mods_p, mods_s, gain, j, np_, ns_, lp, ls, state_f, state_b,
               ssd_w_in, ssd_conv_w, ssd_conv_b, ssd_dt_bias, ssd_a_log, ssd_d, ssd_norm,
               ssd_w_out):
    w_zx = ssd_w_in[j, :, :SSD_ZX_DIM].astype(BF16)
    w_dt = jnp.pad(ssd_w_in[j, :, SSD_ZX_DIM:], ((0, 0), (0, LANES - 2 * SSD_HEADS))).astype(BF16)
    w_out = ssd_w_out[j].astype(BF16)
    pad = LANES - 2 * SSD_HEADS
    dt_bias = jnp.pad(ssd_dt_bias[j].reshape(-1), (0, pad)).reshape(1, LANES)
    a_log = jnp.pad(ssd_a_log[j].reshape(-1), (0, pad)).reshape(1, LANES)
    d_rep = jnp.repeat(ssd_d[j], HEAD_DIM).reshape(1, SSD_D_INNER)
    norm_g = ssd_norm[j].reshape(1, SSD_D_INNER)
    head_of_col = jnp.arange(SSD_D_INNER) // HEAD_DIM
    expand = [jnp.tile((jnp.arange(LANES)[:, None] == head_of_col[None, :] + off).astype(BF16), (2, 1))
              for off in (0, SSD_HEADS)]

    outs = []
    states = None
    for (x, mods, nseq, seq_len, is_prompt) in ((xp, mods_p, np_, lp, True),
                                                (xs, mods_s, ns_, ls, False)):
        tag = "p" if is_prompt else "s"
        zx = _mod_matmul(x, mods, gain, w_zx, rows_per_group=seq_len if not is_prompt else x.shape[0],
                         out_dtype=BF16, tm=1024, tn=1024, name="ssd_in_zx_" + tag)
        dt_raw = _mod_matmul(x, mods, gain, w_dt,
                             rows_per_group=seq_len if not is_prompt else x.shape[0],
                             out_dtype=F32, tm=1024, tn=LANES, name="ssd_in_dt_" + tag)
        xbc = _conv_silu(zx, ssd_conv_w[j], ssd_conv_b[j], seq_len=seq_len)
        common = dict(nseq=nseq, seq_len=seq_len, layer=j)
        fwd = _ssd_scan(xbc, dt_raw, dt_bias, a_log, expand[0], reverse=False,
                        h0=None if is_prompt else state_f, emit_state=is_prompt, d_rep=d_rep,
                        name="ssd_scan_fwd_" + tag, **common)
        bwd = _ssd_scan(xbc, dt_raw, dt_bias, a_log, expand[1], reverse=True,
                        h0=None if is_prompt else state_b, emit_state=is_prompt, zx=zx,
                        y_fwd=fwd[0], norm_g=norm_g, name="ssd_scan_bwd_" + tag, **common)
        if is_prompt:
            states = (fwd[1], bwd[1])
        outs.append(_proj_residual(bwd[0], x, mods,
                                   w_out, rows_per_group=seq_len if not is_prompt else x.shape[0],
                                   tm=512, name="ssd_out_" + tag))
    return outs[0], outs[1], states


def kernel(x_prompt, x_sample, c, state_ssd_fwd, state_ssd_bwd, cache_attn_k, cache_attn_v, cache_nat_k, cache_nat_v, c_ctx, ada_w, ada_b, norm_mix, norm_mlp, mlp_w1, mlp_w2, ssd_w_in, ssd_conv_w, ssd_conv_b, ssd_dt_bias, ssd_a_log, ssd_d, ssd_norm, ssd_w_out, attn_w_qkv, attn_q_norm, attn_k_norm, attn_w_out, nat_w_qkv, nat_rpb, nat_w_out, norm_final):
    np_, lp, d = x_prompt.shape
    ns_, ls, _ = x_sample.shape
    mp_rows, ms_rows = np_ * lp, ns_ * ls
    xp = x_prompt.reshape(mp_rows, d)
    xs = x_sample.reshape(ms_rows, d)

    cond_rows = 16
    cond = jnp.concatenate([c_ctx[None, :], c, jnp.zeros((cond_rows - 1 - ns_, d), F32)], axis=0)
    mods_all = _ada_mods(cond, ada_w, ada_b).reshape(DEPTH, cond_rows, 6, d)

    n_ssd = state_ssd_fwd.shape[1]
    st_f = state_ssd_fwd.reshape(ns_, n_ssd, SSD_D_INNER, SSD_STATE)
    st_b = state_ssd_bwd.reshape(ns_, n_ssd, SSD_D_INNER, SSD_STATE)

    sf_out, sb_out, ak_out, av_out, nk_out, nv_out = [], [], [], [], [], []
    for i in range(DEPTH):
        kind, j = i % N_MIXERS, i // N_MIXERS
        mods_p = mods_all[i, 0:1]
        mods_s = mods_all[i, 1:1 + ns_]
        rpg_p, rpg_s = mp_rows, ls
        if kind == 0:
            xp, xs, (hf, hb) = _ssd_layer(
                xp, xs, mods_p, mods_s, norm_mix[i], j, np_, ns_, lp, ls, st_f, st_b,
                ssd_w_in, ssd_conv_w, ssd_conv_b, ssd_dt_bias, ssd_a_log, ssd_d, ssd_norm,
                ssd_w_out)
            sf_out.append(hf.reshape(np_, SSD_HEADS, HEAD_DIM, SSD_STATE))
            sb_out.append(hb.reshape(np_, SSD_HEADS, HEAD_DIM, SSD_STATE))
        elif kind == 1:
            w_qkv = attn_w_qkv[j].astype(BF16)
            w_out = attn_w_out[j].astype(BF16)
            nk = ATTN_KV_HEADS * HEAD_DIM
            qkv = _mod_matmul(xp, mods_p, norm_mix[i], w_qkv, rows_per_group=rpg_p,
                              out_dtype=F32, tm=1024, tn=768, name="gqa_qkv_p")
            q, k, v, kc, vc = _qk_norm(qkv, attn_q_norm[j], attn_k_norm[j], seq_len=lp,
                                       rope=False, emit_cache=True)
            o = _attention(q, 0, k, 0, v, 0, None, None, nb=np_, seq_len=lp, hq=ATTN_HEADS,
                           hkv=ATTN_KV_HEADS, tq=128, tk=lp, q_scale=1.0, name="gqa_attn_p")
            xp = _proj_residual(o, xp, mods_p, w_out, rows_per_group=rpg_p, tm=512,
                                name="gqa_out_p")
            ak_out.append(kc.reshape(np_, lp, ATTN_KV_HEADS, HEAD_DIM))
            av_out.append(vc.reshape(np_, lp, ATTN_KV_HEADS, HEAD_DIM))
            qkv = _mod_matmul(xs, mods_s, norm_mix[i], w_qkv, rows_per_group=rpg_s,
                              out_dtype=F32, tm=1024, tn=768, name="gqa_qkv_s")
            q, k, v = _qk_norm(qkv, attn_q_norm[j], attn_k_norm[j], seq_len=ls, rope=True,
                               emit_cache=False)
            past = cache_attn_k.shape[2]
            ck = cache_attn_k[:, j].reshape(ns_, past, nk).astype(BF16)
            cv = _v_with_ones(cache_attn_v[:, j].reshape(ns_, past, nk)).astype(BF16)
            o = _attention(q, 0, k, 0, v, 0, ck, cv, nb=ns_, seq_len=ls, hq=ATTN_HEADS,
                           hkv=ATTN_KV_HEADS, tq=128, tk=2048, q_scale=1.0, name="gqa_attn_s",
                           v_ones=True)
            xs = _proj_residual(o, xs, mods_s, w_out, rows_per_group=rpg_s, tm=512,
                                name="gqa_out_s")
        else:
            w_qkv = nat_w_qkv[j].astype(BF16)
            w_out = nat_w_out[j].astype(BF16)
            wd = NAT_HEADS * HEAD_DIM
            scale = HEAD_DIM ** -0.5 * LOG2E
            qkv = _mod_matmul(xp, mods_p, norm_mix[i], w_qkv, rows_per_group=rpg_p,
                              out_dtype=F32, tm=1024, tn=1024, name="nat_qkv_p")
            o = _attention(qkv, 0, qkv, 1, qkv, 2, None, None, nb=np_, seq_len=lp, hq=NAT_HEADS,
                           hkv=NAT_HEADS, tq=lp, tk=lp, q_scale=scale, name="nat_attn_p")
            xp = _proj_residual(o, xp, mods_p, w_out, rows_per_group=rpg_p, tm=512,
                                name="nat_out_p")
            nk_out.append(qkv[:, wd:2 * wd].reshape(np_, lp, NAT_HEADS, HEAD_DIM))
            nv_out.append(qkv[:, 2 * wd:].reshape(np_, lp, NAT_HEADS, HEAD_DIM))
            qkv = _mod_matmul(xs, mods_s, norm_mix[i], w_qkv, rows_per_group=rpg_s,
                              out_dtype=BF16, tm=1024, tn=1024, name="nat_qkv_s")
            past = cache_nat_k.shape[2]
            ck = cache_nat_k[:, j].reshape(ns_, past, wd).astype(BF16)
            cv = cache_nat_v[:, j].reshape(ns_, past, wd).astype(BF16)
            o = _nat_latent(qkv, ck, cv, nat_rpb[j], nb=ns_, seq_len=ls)
            xs = _proj_residual(o, xs, mods_s, w_out, rows_per_group=rpg_s, tm=512,
                                name="nat_out_s")
        last = i == DEPTH - 1
        w1 = mlp_w1[i].astype(BF16)
        w2 = mlp_w2[i].astype(BF16)
        xp = _mlp(xp, mods_p, norm_mlp[i], w1, w2, norm_final, rows_per_group=rpg_p,
                  final_norm=last, name="mlp_p")
        xs = _mlp(xs, mods_s, norm_mlp[i], w1, w2, norm_final, rows_per_group=rpg_s,
                  final_norm=last, name="mlp_s")

    return (xp.reshape(np_, lp, d), xs.reshape(ns_, ls, d),
            jnp.stack(sf_out, axis=1), jnp.stack(sb_out, axis=1),
            jnp.stack(ak_out, axis=1), jnp.stack(av_out, axis=1),
            jnp.stack(nk_out, axis=1), jnp.stack(nv_out, axis=1))
```

```python
import functools
import math

import jax
import jax.numpy as jnp
from jax import lax
from jax.experimental import pallas as pl
from jax.experimental.pallas import tpu as pltpu

F32 = jnp.float32
BF16 = jnp.bfloat16

D_MODEL = 1024
DEPTH = 4
GRID_W = 64
N_MIXERS = 3
EPS = 1e-6
NEG_INF = -1e30
HEAD_DIM = 64
LANES = 128
LOG2E = math.log2(math.e)
SSD_D_INNER = 2 * D_MODEL
SSD_HEADS = SSD_D_INNER // HEAD_DIM
SSD_GROUPS = 4
SSD_HPG = SSD_HEADS // SSD_GROUPS
SSD_STATE = 128
SSD_CONV_W = 5
SSD_CHUNK = 128
SSD_CONV_DIM = SSD_D_INNER + 2 * SSD_GROUPS * SSD_STATE
SSD_ZX_DIM = SSD_D_INNER + SSD_CONV_DIM
ATTN_HEADS = D_MODEL // HEAD_DIM
ATTN_KV_HEADS = 4
ROPE_BASE = 10000.0
NAT_HEADS = D_MODEL // HEAD_DIM
NAT_KH = 8
NAT_KW = 16
MLP_HIDDEN = 4 * D_MODEL

VMEM_LIMIT_BYTES = 48 * 1024 * 1024
NAT_VMEM_LIMIT_BYTES = 56 * 1024 * 1024
MLP_VMEM_LIMIT_BYTES = 56 * 1024 * 1024


def _params(semantics, vmem=VMEM_LIMIT_BYTES):
    return pltpu.CompilerParams(dimension_semantics=semantics, vmem_limit_bytes=vmem)


def _sigmoid(x):
    return 1.0 / (1.0 + jnp.exp(-x))


def _modulated_norm(x, gain, shift, scale):
    ms = jnp.mean(x * x, axis=-1, keepdims=True)
    return (x * lax.rsqrt(ms + EPS) * gain) * (1.0 + scale) + shift


def _ada_kernel(c_ref, w_ref, b_ref, o_ref):
    c = c_ref[...]
    a = (c * _sigmoid(c)).astype(BF16)
    o_ref[...] = jnp.dot(a, w_ref[...].astype(BF16), preferred_element_type=F32) + b_ref[...]


def _ada_mods(cond, ada_w, ada_b):
    depth, d, n = ada_w.shape
    r = cond.shape[0]
    tn = 1024
    return pl.pallas_call(
        _ada_kernel,
        out_shape=jax.ShapeDtypeStruct((depth, r, n), F32),
        grid=(depth, n // tn),
        in_specs=[pl.BlockSpec((r, d), lambda l, j: (0, 0)),
                  pl.BlockSpec((None, d, tn), lambda l, j: (l, 0, j)),
                  pl.BlockSpec((None, 1, tn), lambda l, j: (l, 0, j))],
        out_specs=pl.BlockSpec((None, r, tn), lambda l, j: (l, 0, j)),
        compiler_params=_params(("parallel", "parallel")),
        name="ada_mods",
    )(cond, ada_w, ada_b.reshape(depth, 1, n))


def _modmm_kernel(*refs, shift_row, scale_row, has_side):
    if has_side:
        x_ref, mod_ref, g_ref, w_ref, ws_ref, o_ref, os_ref, h_ref = refs
    else:
        x_ref, mod_ref, g_ref, w_ref, o_ref, h_ref = refs

    @pl.when(pl.program_id(1) == 0)
    def _():
        h = _modulated_norm(x_ref[...], g_ref[...],
                            mod_ref[shift_row:shift_row + 1, :],
                            mod_ref[scale_row:scale_row + 1, :])
        h_ref[...] = h.astype(BF16)
        if has_side:
            os_ref[...] = jnp.dot(h_ref[...], ws_ref[...], preferred_element_type=F32)

    o_ref[...] = jnp.dot(h_ref[...], w_ref[...],
                         preferred_element_type=F32).astype(o_ref.dtype)


def _mod_matmul(x, mods, gain, w, *, rows_per_group, out_dtype, tm, tn, name, w_side=None):
    m, d = x.shape
    n = w.shape[1]
    tpg = rows_per_group // tm
    has_side = w_side is not None
    in_specs = [pl.BlockSpec((tm, d), lambda i, j: (i, 0)),
                pl.BlockSpec((None, 6, d), lambda i, j: (i // tpg, 0, 0)),
                pl.BlockSpec((1, d), lambda i, j: (0, 0)),
                pl.BlockSpec((d, tn), lambda i, j: (0, j))]
    args = [x, mods, gain.reshape(1, d), w]
    out_shape = jax.ShapeDtypeStruct((m, n), out_dtype)
    out_specs = pl.BlockSpec((tm, tn), lambda i, j: (i, j))
    if has_side:
        ns = w_side.shape[1]
        in_specs.append(pl.BlockSpec((d, ns), lambda i, j: (0, 0)))
        args.append(w_side)
        out_shape = [out_shape, jax.ShapeDtypeStruct((m, ns), F32)]
        out_specs = [out_specs, pl.BlockSpec((tm, ns), lambda i, j: (i, 0))]
    return pl.pallas_call(
        functools.partial(_modmm_kernel, shift_row=0, scale_row=1, has_side=has_side),
        out_shape=out_shape,
        grid=(m // tm, n // tn),
        in_specs=in_specs,
        out_specs=out_specs,
        scratch_shapes=[pltpu.VMEM((tm, d), BF16)],
        compiler_params=_params(("parallel", "arbitrary")),
        name=name,
    )(*args)


def _proj_res_kernel(a_ref, x_ref, mod_ref, w_ref, o_ref, *, gate_row):
    y = jnp.dot(a_ref[...], w_ref[...], preferred_element_type=F32)
    o_ref[...] = x_ref[...] + mod_ref[gate_row:gate_row + 1, :] * y


def _proj_residual(a, x, mods, w, *, rows_per_group, tm, name):
    m, k = a.shape
    d = x.shape[1]
    tpg = rows_per_group // tm
    return pl.pallas_call(
        functools.partial(_proj_res_kernel, gate_row=2),
        out_shape=jax.ShapeDtypeStruct((m, d), F32),
        grid=(m // tm,),
        in_specs=[pl.BlockSpec((tm, k), lambda i: (i, 0)),
                  pl.BlockSpec((tm, d), lambda i: (i, 0)),
                  pl.BlockSpec((None, 6, d), lambda i: (i // tpg, 0, 0)),
                  pl.BlockSpec((k, d), lambda i: (0, 0))],
        out_specs=pl.BlockSpec((tm, d), lambda i: (i, 0)),
        compiler_params=_params(("parallel",)),
        name=name,
    )(a, x, mods, w)


def _mlp_kernel(x_ref, xn_ref, mod_ref, modn_ref, g_ref, w1_ref, w2_ref, gf_ref, o_ref, h_ref,
                acc_ref, *, final_norm):
    i, j = pl.program_id(0), pl.program_id(1)
    last = pl.num_programs(1) - 1
    slot = i % 2

    def prepare(x_r, mod_r, dst):
        h = _modulated_norm(x_r[...], g_ref[...], mod_r[3:4, :], mod_r[4:5, :])
        h_ref[dst] = h.astype(BF16)

    def hidden_step():
        a = jnp.dot(h_ref[slot], w1_ref[...], preferred_element_type=F32)
        a = jnp.maximum(a, 0.0)
        return jnp.dot((a * a).astype(BF16), w2_ref[...], preferred_element_type=F32)

    @pl.when((i == 0) & (j == 0))
    def _():
        prepare(x_ref, mod_ref, 0)

    @pl.when(j == 0)
    def _():
        acc_ref[...] = hidden_step()

    @pl.when((j > 0) & (j < last))
    def _():
        acc_ref[...] += hidden_step()

    @pl.when(j == last)
    def _():
        y = x_ref[...] + mod_ref[5:6, :] * (acc_ref[...] + hidden_step())
        if final_norm:
            ms = jnp.mean(y * y, axis=-1, keepdims=True)
            y = y * lax.rsqrt(ms + EPS) * gf_ref[...]
        o_ref[...] = y
        prepare(xn_ref, modn_ref, 1 - slot)


def _mlp(x, mods, gain, w1, w2, gain_final, *, rows_per_group, final_norm, name):
    m, d = x.shape
    hdim = w1.shape[1]
    tm, th = 1024, 1024
    tpg = rows_per_group // tm
    ni = m // tm

    def nxt(i):
        return jnp.minimum(i + 1, ni - 1)

    return pl.pallas_call(
        functools.partial(_mlp_kernel, final_norm=final_norm),
        out_shape=jax.ShapeDtypeStruct((m, d), F32),
        grid=(ni, hdim // th),
        in_specs=[pl.BlockSpec((tm, d), lambda i, j: (i, 0)),
                  pl.BlockSpec((tm, d), lambda i, j: (nxt(i), 0)),
                  pl.BlockSpec((None, 6, d), lambda i, j: (i // tpg, 0, 0)),
                  pl.BlockSpec((None, 6, d), lambda i, j: (nxt(i) // tpg, 0, 0)),
                  pl.BlockSpec((1, d), lambda i, j: (0, 0)),
                  pl.BlockSpec((d, th), lambda i, j: (0, j)),
                  pl.BlockSpec((th, d), lambda i, j: (j, 0)),
                  pl.BlockSpec((1, d), lambda i, j: (0, 0))],
        out_specs=pl.BlockSpec((tm, d), lambda i, j: (i, 0)),
        scratch_shapes=[pltpu.VMEM((2, tm, d), BF16), pltpu.VMEM((tm, d), F32)],
        compiler_params=_params(("arbitrary", "arbitrary"), MLP_VMEM_LIMIT_BYTES),
        name=name,
    )(x, x, mods, mods, gain.reshape(1, d), w1, w2, gain_final.reshape(1, d))


CONV_HALO_ROWS = 16


CONV_TAPS_SHIFTED = (0, 1, 3, 4)


def _conv_kernel(prev_ref, cur_ref, next_ref, sh_ref, w_ref, b_ref, o_ref, *, tiles_per_seq):
    pos = pl.program_id(0) % tiles_per_seq
    cur_b = cur_ref[...]
    cur = cur_b.astype(F32)
    tr = cur.shape[0]
    h = CONV_HALO_ROWS
    w = w_ref[...]
    bias = b_ref[...]

    def silu(y):
        return (y / (1.0 + jnp.exp2(y * (-LOG2E)))).astype(o_ref.dtype)

    y = bias + w[2:3] * cur
    for idx, k in enumerate(CONV_TAPS_SHIFTED):
        y = y + w[k:k + 1] * jnp.dot(sh_ref[idx], cur_b, preferred_element_type=F32)
    o_ref[...] = silu(y)

    def edge(x):
        ye = bias + w[2:3] * x
        for k in CONV_TAPS_SHIFTED:
            ye = ye + w[k:k + 1] * pltpu.roll(x, (SSD_CONV_W // 2 - k) % x.shape[0], 0)
        return silu(ye)[h:2 * h]

    pv = jnp.where(pos == 0, 0.0, prev_ref[...].astype(F32))
    nx = jnp.where(pos == tiles_per_seq - 1, 0.0, next_ref[...].astype(F32))
    o_ref[0:h, :] = edge(jnp.concatenate([pv, cur[0:2 * h]], axis=0))
    o_ref[tr - h:tr, :] = edge(jnp.concatenate([cur[tr - 2 * h:tr], nx], axis=0))


def _conv_silu(zx, conv_w, conv_b, *, seq_len):
    m = zx.shape[0]
    tr, tc = 256, 1024
    c0 = SSD_D_INNER // tc
    halo_per_tile = tr // CONV_HALO_ROWS
    n_halo = m // CONV_HALO_ROWS
    t = jnp.arange(tr)
    shifts = jnp.stack([(t[None, :] == t[:, None] + (k - SSD_CONV_W // 2)).astype(BF16)
                        for k in CONV_TAPS_SHIFTED])
    return pl.pallas_call(
        functools.partial(_conv_kernel, tiles_per_seq=seq_len // tr),
        out_shape=jax.ShapeDtypeStruct((m, SSD_CONV_DIM), BF16),
        grid=(m // tr, SSD_CONV_DIM // tc),
        in_specs=[
            pl.BlockSpec((CONV_HALO_ROWS, tc),
                         lambda i, j: (jnp.maximum(i * halo_per_tile - 1, 0), c0 + j)),
            pl.BlockSpec((tr, tc), lambda i, j: (i, c0 + j)),
            pl.BlockSpec((CONV_HALO_ROWS, tc),
                         lambda i, j: (jnp.minimum((i + 1) * halo_per_tile, n_halo - 1), c0 + j)),
            pl.BlockSpec((len(CONV_TAPS_SHIFTED), tr, tr), lambda i, j: (0, 0, 0)),
            pl.BlockSpec((SSD_CONV_W, tc), lambda i, j: (0, j)),
            pl.BlockSpec((1, tc), lambda i, j: (0, j))],
        out_specs=pl.BlockSpec((tr, tc), lambda i, j: (i, j)),
        compiler_params=_params(("parallel", "parallel")),
        name="ssd_conv_silu",
    )(zx, zx, zx, shifts, conv_w, conv_b.reshape(1, SSD_CONV_DIM))


def _split3(a):
    a1 = a.astype(BF16)
    r1 = a - a1.astype(F32)
    a2 = r1.astype(BF16)
    a3 = (r1 - a2.astype(F32)).astype(BF16)
    return a1, a2, a3


def _dot_exact_left(m01, a):
    return sum(jnp.dot(m01, p, preferred_element_type=F32) for p in _split3(a))


def _spread_heads(a, e2_ref):
    hi = a.astype(BF16)
    lo = (a - hi.astype(F32)).astype(BF16)
    return jnp.dot(jnp.concatenate([hi, lo], axis=1), e2_ref[...], preferred_element_type=F32)


SSD_SEQS_PER_STEP = 2


def _ssd_decays(dt_ref, dtb_ref, alog_ref, *, reverse):
    q = SSD_CHUNK
    pre = dt_ref[...] + dtb_ref[...]
    dt = jnp.maximum(pre, 0.0) + jnp.log1p(jnp.exp(-jnp.abs(pre)))
    a_dt = dt * (-jnp.exp(alog_ref[...]) * LOG2E)
    ri = lax.broadcasted_iota(jnp.int32, (q, q), 0)
    ci = lax.broadcasted_iota(jnp.int32, (q, q), 1)
    keep = (ri <= ci) if reverse else (ri >= ci)
    cs = _dot_exact_left(jnp.where(keep, 1.0, 0.0).astype(BF16), a_dt)
    edge = cs[0:1, :] if reverse else cs[q - 1:q, :]
    factors = jnp.concatenate([jnp.exp2(cs), jnp.exp2(edge - cs) * dt], axis=0)
    return dict(keep=keep, cs=cs, cs_t=cs.T, dt_t=dt.T, factors=factors)


def _ssd_chunk(xbc_ref, decays, exp_cs, to_end, z_ref, yf_ref, ng_ref, drep_ref,
               y_ref, s_ref, ybuf_ref, *, reverse, combine):
    q = SSD_CHUNK
    gw = SSD_HPG * HEAD_DIM
    col0 = SSD_HEADS if reverse else 0
    keep, cs, cs_t, dt_t = decays["keep"], decays["cs"], decays["cs_t"], decays["dt_t"]
    exp_edge = exp_cs[0:1, :] if reverse else exp_cs[q - 1:q, :]
    lane = lax.broadcasted_iota(jnp.int32, (1, LANES), 1)
    keep_lo = jnp.where(lane < HEAD_DIM, 1.0, 0.0).astype(BF16)
    keep_hi = jnp.where(lane < HEAD_DIM, 0.0, 1.0).astype(BF16)

    for g in range(SSD_GROUPS):
        gs = slice(g * gw, (g + 1) * gw)
        b_g = xbc_ref[:, SSD_D_INNER + g * SSD_STATE:SSD_D_INNER + (g + 1) * SSD_STATE]
        c_off = SSD_D_INNER + SSD_GROUPS * SSD_STATE
        c_g = xbc_ref[:, c_off + g * SSD_STATE:c_off + (g + 1) * SSD_STATE]
        cb = lax.dot_general(c_g, b_g, (((1,), (1,)), ((), ())), preferred_element_type=F32)
        cb = jnp.where(keep, cb, 0.0)
        s_g = s_ref[:, gs]
        y_off = jnp.dot(c_g, s_g.astype(BF16), preferred_element_type=F32)
        for mp in range(gw // LANES):
            m = g * (gw // LANES) + mp
            ls = slice(m * LANES, (m + 1) * LANES)
            x_pair = xbc_ref[:, ls]
            k0 = col0 + 2 * m
            ws = []
            for k in (k0, k0 + 1):
                seg = jnp.minimum(cs[:, k:k + 1] - cs_t[k:k + 1, :], 0.0)
                ws.append((jnp.exp2(seg) * cb * dt_t[k:k + 1, :]).astype(BF16))
            y_pair = (jnp.dot(jnp.concatenate(ws, axis=1),
                              jnp.concatenate([x_pair * keep_lo, x_pair * keep_hi], axis=0),
                              preferred_element_type=F32)
                      + y_off[:, mp * LANES:(mp + 1) * LANES] * exp_cs[:, ls])
            if not combine:
                y_pair = y_pair + drep_ref[:, ls] * x_pair.astype(F32)
            ybuf_ref[:, ls] = y_pair
        xs_g = (to_end[:, gs] * xbc_ref[:, gs].astype(F32)).astype(BF16)
        b_t = b_g.astype(F32).T.astype(BF16)
        s_ref[:, gs] = s_g * exp_edge[:, gs] + jnp.dot(b_t, xs_g, preferred_element_type=F32)

    if combine:
        z = z_ref[...].astype(F32)
        gated = (ybuf_ref[...] + yf_ref[...]) * (z * _sigmoid(z))
        ms = jnp.mean(gated * gated, axis=-1, keepdims=True)
        y_ref[...] = (gated * lax.rsqrt(ms + EPS) * ng_ref[...]).astype(y_ref.dtype)


def _ssd_kernel(*refs, reverse, has_h0, emit_state, combine):
    it = iter(refs)
    xbc_ref, dt_ref, dtb_ref, alog_ref, e_ref = (next(it) for _ in range(5))
    h0_ref = next(it) if has_h0 else None
    z_ref = yf_ref = ng_ref = drep_ref = None
    if combine:
        z_ref, yf_ref, ng_ref = next(it), next(it), next(it)
    else:
        drep_ref = next(it)
    y_ref = next(it)
    st_ref = next(it) if emit_state else None
    s_ref = next(it)
    ybuf_ref = next(it) if combine else y_ref
    q = SSD_CHUNK
    c = pl.program_id(1)
    seqs = range(SSD_SEQS_PER_STEP)

    @pl.when(c == 0)
    def _():
        if has_h0:
            for sq in seqs:
                for k in range(SSD_D_INNER // q):
                    s_ref[sq, :, k * q:(k + 1) * q] = h0_ref[sq, k * q:(k + 1) * q, :].T
        else:
            s_ref[...] = jnp.zeros_like(s_ref)

    decays = [_ssd_decays(dt_ref.at[sq], dtb_ref, alog_ref, reverse=reverse) for sq in seqs]
    spread = _spread_heads(jnp.concatenate([d["factors"] for d in decays], axis=0), e_ref)
    for sq in seqs:
        r0 = sq * 2 * q
        _ssd_chunk(xbc_ref.at[sq], decays[sq], spread[r0:r0 + q], spread[r0 + q:r0 + 2 * q],
                   z_ref.at[sq] if combine else None, yf_ref.at[sq] if combine else None,
                   ng_ref, drep_ref, y_ref.at[sq], s_ref.at[sq], ybuf_ref.at[sq],
                   reverse=reverse, combine=combine)

    if emit_state:
        @pl.when(c == pl.num_programs(1) - 1)
        def _():
            for sq in seqs:
                for k in range(SSD_D_INNER // q):
                    st_ref[sq, k * q:(k + 1) * q, :] = s_ref[sq, :, k * q:(k + 1) * q].T


def _ssd_scan(xbc, dt_raw, dt_bias, a_log, expand, *, nseq, seq_len, reverse, h0=None,
              layer=0, emit_state=False, zx=None, y_fwd=None, norm_g=None, d_rep=None, name):
    combine = zx is not None
    nchunk = seq_len // SSD_CHUNK
    q = SSD_CHUNK
    sp = SSD_SEQS_PER_STEP
    nb = nseq // sp

    def per_seq(a):
        return a.reshape(nb, sp, seq_len, a.shape[-1])

    def chunk_spec(width):
        return pl.BlockSpec((None, sp, q, width),
                            lambda b, c: (b, 0, (nchunk - 1 - c) if reverse else c, 0))

    def const_spec(shape):
        return pl.BlockSpec(shape, lambda b, c: (0,) * len(shape))

    in_specs = [chunk_spec(SSD_CONV_DIM), chunk_spec(LANES), const_spec((1, LANES)),
                const_spec((1, LANES)), const_spec((2 * LANES, SSD_D_INNER))]
    args = [per_seq(xbc), per_seq(dt_raw), dt_bias, a_log, expand]
    if h0 is not None:
        in_specs.append(pl.BlockSpec((None, sp, None, SSD_D_INNER, SSD_STATE),
                                     lambda b, c: (b, 0, layer, 0, 0)))
        args.append(h0.reshape((nb, sp) + h0.shape[1:]))
    if combine:
        in_specs += [chunk_spec(SSD_D_INNER), chunk_spec(SSD_D_INNER),
                     const_spec((1, SSD_D_INNER))]
        args += [per_seq(zx), per_seq(y_fwd), norm_g]
    else:
        in_specs.append(const_spec((1, SSD_D_INNER)))
        args.append(d_rep)
    out_shape = [jax.ShapeDtypeStruct((nb, sp, seq_len, SSD_D_INNER), BF16 if combine else F32)]
    out_specs = [chunk_spec(SSD_D_INNER)]
    if emit_state:
        out_shape.append(jax.ShapeDtypeStruct((nb, sp, SSD_D_INNER, SSD_STATE), F32))
        out_specs.append(pl.BlockSpec((None, sp, SSD_D_INNER, SSD_STATE),
                                      lambda b, c: (b, 0, 0, 0)))
    scratch = [pltpu.VMEM((sp, SSD_STATE, SSD_D_INNER), F32)]
    if combine:
        scratch.append(pltpu.VMEM((sp, q, SSD_D_INNER), F32))
    outs = list(pl.pallas_call(
        functools.partial(_ssd_kernel, reverse=reverse, has_h0=h0 is not None,
                          emit_state=emit_state, combine=combine),
        out_shape=out_shape,
        grid=(nb, nchunk),
        in_specs=in_specs,
        out_specs=out_specs,
        scratch_shapes=scratch,
        compiler_params=_params(("parallel", "arbitrary")),
        name=name,
    )(*args))
    outs[0] = outs[0].reshape(nseq * seq_len, SSD_D_INNER)
    if emit_state:
        outs[1] = outs[1].reshape(nseq, SSD_D_INNER, SSD_STATE)
    return outs


def _head_rms(x, gain_row):
    rows, width = x.shape
    lane = lax.broadcasted_iota(jnp.int32, (rows, LANES), 1)
    lo = lane < HEAD_DIM
    outs = []
    for j in range(width // LANES):
        xb = x[:, j * LANES:(j + 1) * LANES]
        sq = xb * xb
        s_lo = jnp.sum(jnp.where(lo, sq, 0.0), axis=-1, keepdims=True)
        s_hi = jnp.sum(jnp.where(lo, 0.0, sq), axis=-1, keepdims=True)
        ms = jnp.where(lo, s_lo, s_hi) * (1.0 / HEAD_DIM)
        outs.append(xb * lax.rsqrt(ms + EPS))
    return jnp.concatenate(outs, axis=1) * gain_row


def _rope(x, cos, sin_signed):
    rows, width = x.shape
    lane = lax.broadcasted_iota(jnp.int32, (rows, LANES), 1)
    even = (lane % 2) == 0
    outs = []
    for j in range(width // LANES):
        xb = x[:, j * LANES:(j + 1) * LANES]
        partner = jnp.where(even, pltpu.roll(xb, LANES - 1, 1), pltpu.roll(xb, 1, 1))
        outs.append(xb * cos + partner * sin_signed)
    return jnp.concatenate(outs, axis=1)


def _qknorm_kernel(*refs, rope, emit_cache):
    it = iter(refs)
    q_ref, k_ref, v_ref, qg_ref, kg_ref = (next(it) for _ in range(5))
    if rope:
        cos_ref, sin_ref = next(it), next(it)
    qo_ref, ko_ref, vo_ref = next(it), next(it), next(it)
    q = _head_rms(q_ref[...], qg_ref[...])
    k = _head_rms(k_ref[...], kg_ref[...])
    if emit_cache:
        kc_ref, vc_ref = next(it), next(it)
        kc_ref[...] = k
        vc_ref[...] = v_ref[...]
    if rope:
        q = _rope(q, cos_ref[...], sin_ref[...])
        k = _rope(k, cos_ref[...], sin_ref[...])
    qo_ref[...] = (q * (HEAD_DIM ** -0.5 * LOG2E)).astype(BF16)
    ko_ref[...] = k.astype(BF16)
    v = v_ref[...]
    vo_ref[...] = (_v_with_ones(v) if rope else v).astype(BF16)


def _rope_tables(seq_len):
    n_pairs = HEAD_DIM // 4
    t = jnp.arange(seq_len)
    row = (t // GRID_W).astype(F32)
    col = (t % GRID_W).astype(F32)
    freqs = ROPE_BASE ** (-jnp.arange(n_pairs, dtype=F32) / n_pairs)
    ang = jnp.concatenate([row[:, None] * freqs, col[:, None] * freqs], axis=-1)
    cos = jnp.repeat(jnp.cos(ang), 2, axis=-1)
    sin = jnp.repeat(jnp.sin(ang), 2, axis=-1)
    sign = jnp.where(jnp.arange(HEAD_DIM) % 2 == 0, -1.0, 1.0).astype(F32)
    return jnp.tile(cos, (1, 2)), jnp.tile(sin * sign, (1, 2))


def _qk_norm(qkv, q_gain, k_gain, *, seq_len, rope, emit_cache):
    m = qkv.shape[0]
    tm = 256
    nq = ATTN_HEADS * HEAD_DIM
    nk = ATTN_KV_HEADS * HEAD_DIM
    qg = jnp.tile(q_gain, ATTN_HEADS).reshape(1, nq)
    kg = jnp.tile(k_gain, ATTN_KV_HEADS).reshape(1, nk)
    in_specs = [pl.BlockSpec((tm, nq), lambda i: (i, 0)),
                pl.BlockSpec((tm, nk), lambda i: (i, nq // nk)),
                pl.BlockSpec((tm, nk), lambda i: (i, nq // nk + 1)),
                pl.BlockSpec((1, nq), lambda i: (0, 0)),
                pl.BlockSpec((1, nk), lambda i: (0, 0))]
    args = [qkv, qkv, qkv, qg, kg]
    if rope:
        cos, sin = _rope_tables(seq_len)
        tps = seq_len // tm
        in_specs += [pl.BlockSpec((tm, LANES), lambda i: (i % tps, 0)),
                     pl.BlockSpec((tm, LANES), lambda i: (i % tps, 0))]
        args += [cos, sin]
    out_shape = [jax.ShapeDtypeStruct((m, nq), BF16), jax.ShapeDtypeStruct((m, nk), BF16),
                 jax.ShapeDtypeStruct((m, nk), BF16)]
    out_specs = [pl.BlockSpec((tm, nq), lambda i: (i, 0)), pl.BlockSpec((tm, nk), lambda i: (i, 0)),
                 pl.BlockSpec((tm, nk), lambda i: (i, 0))]
    if rope:
        out_shape[2] = jax.ShapeDtypeStruct((m, ATTN_KV_HEADS * LANES), BF16)
        out_specs[2] = pl.BlockSpec((tm, ATTN_KV_HEADS * LANES), lambda i: (i, 0))
    if emit_cache:
        out_shape += [jax.ShapeDtypeStruct((m, nk), F32)] * 2
        out_specs += [pl.BlockSpec((tm, nk), lambda i: (i, 0))] * 2
    return pl.pallas_call(
        functools.partial(_qknorm_kernel, rope=rope, emit_cache=emit_cache),
        out_shape=out_shape,
        grid=(m // tm,),
        in_specs=in_specs,
        out_specs=out_specs,
        compiler_params=_params(("parallel",)),
        name="gqa_qk_norm_rope" if rope else "gqa_qk_norm",
    )(*args)


def _place_half(x, lane, src_half, dst_half):
    if src_half != dst_half:
        x = pltpu.roll(x, HEAD_DIM, 1)
    keep = (lane >= HEAD_DIM) if dst_half else (lane < HEAD_DIM)
    return jnp.where(keep, x, 0.0)


def _v_with_ones(v):
    lane = lax.broadcasted_iota(jnp.int32, v.shape[:-1] + (LANES,), v.ndim - 1)
    tiles = []
    for g in range(v.shape[-1] // HEAD_DIM):
        pair = v[..., (g // 2) * LANES:(g // 2 + 1) * LANES]
        own = (lane >= HEAD_DIM) if g % 2 else (lane < HEAD_DIM)
        tiles.append(jnp.where(own, pair, 1.0))
    return jnp.concatenate(tiles, axis=-1)


def _attn_kernel(*refs, hq, hkv, tk, has_ctx, q_scale, v_ones):
    if has_ctx:
        q_ref, k_ref, v_ref, ck_ref, cv_ref, o_ref = refs
    else:
        q_ref, k_ref, v_ref, o_ref = refs
    grp = hq // hkv
    tq = q_ref.shape[0]
    rows = grp * tq
    lane = lax.broadcasted_iota(jnp.int32, (tq, LANES), 1)
    nt = (((1,), (1,)), ((), ()))

    qst = []
    for g in range(hkv):
        parts = []
        for u in range(grp):
            h = g * grp + u
            qb = q_ref[:, (h // 2) * LANES:(h // 2 + 1) * LANES].astype(F32)
            if q_scale != 1.0:
                qb = qb * q_scale
            parts.append(_place_half(qb, lane, h % 2, g % 2).astype(BF16))
        qst.append(parts[0] if grp == 1 else jnp.concatenate(parts, axis=0))

    def step(g, kblk, vblk, carry):
        m_i, l_i, acc = carry
        s = lax.dot_general(qst[g], kblk.astype(BF16), nt, preferred_element_type=F32)
        m_n = jnp.maximum(m_i, jnp.max(s, axis=-1, keepdims=True))
        alpha = jnp.exp2(m_i - m_n)
        p = jnp.exp2(s - m_n)
        acc = alpha * acc + jnp.dot(p.astype(BF16), vblk.astype(BF16),
                                    preferred_element_type=F32)
        if v_ones:
            return m_n, l_i, acc
        return m_n, alpha * l_i + jnp.sum(p, axis=-1, keepdims=True), acc

    def all_heads(kv_at, carries):
        out = []
        for g in range(hkv):
            kc = slice((g // 2) * LANES, (g // 2 + 1) * LANES)
            vc = slice(g * LANES, (g + 1) * LANES) if v_ones else kc
            out.append(step(g, *kv_at(kc, vc), carries[g]))
        return tuple(out)

    carries = tuple((jnp.full((rows, 1), -jnp.inf, F32), jnp.zeros((rows, 1), F32),
                     jnp.zeros((rows, LANES), F32)) for _ in range(hkv))
    n_self = k_ref.shape[0] // tk
    if n_self == 1:
        carries = all_heads(lambda kc, vc: (k_ref[:, kc], v_ref[:, vc]), carries)
    else:
        def body(ci, car):
            r0 = pl.multiple_of(ci * tk, tk)
            return all_heads(lambda kc, vc: (k_ref[pl.ds(r0, tk), kc], v_ref[pl.ds(r0, tk), vc]),
                             car)
        carries = lax.fori_loop(0, n_self, body, carries, unroll=2)
    if has_ctx:
        carries = all_heads(lambda kc, vc: (ck_ref[:, kc], cv_ref[:, vc]), carries)

    head_out = [None] * hq
    for g in range(hkv):
        _, l_f, acc = carries[g]
        if v_ones:
            ones_lane = HEAD_DIM * (1 - g % 2)
            l_f = acc[:, ones_lane:ones_lane + 1]
        o = acc / l_f
        for u in range(grp):
            head_out[g * grp + u] = o[u * tq:(u + 1) * tq]
    for mpair in range(hq // 2):
        oa, ob = head_out[2 * mpair], head_out[2 * mpair + 1]
        if ((2 * mpair) // grp) % 2 != 0:
            oa = pltpu.roll(oa, HEAD_DIM, 1)
        if ((2 * mpair + 1) // grp) % 2 != 1:
            ob = pltpu.roll(ob, HEAD_DIM, 1)
        o_ref[:, mpair * LANES:(mpair + 1) * LANES] = jnp.where(
            lane < HEAD_DIM, oa, ob).astype(o_ref.dtype)


def _attention(q_arr, q_col, k_arr, k_col, v_arr, v_col, ctx_k, ctx_v, *, nb, seq_len,
               hq, hkv, tq, tk, q_scale, name, v_ones=False):
    wq, wk = hq * HEAD_DIM, hkv * HEAD_DIM
    wv = hkv * LANES if v_ones else wk
    qt = seq_len // tq
    in_specs = [pl.BlockSpec((tq, wq), lambda b, i: (b * qt + i, q_col)),
                pl.BlockSpec((seq_len, wk), lambda b, i: (b, k_col)),
                pl.BlockSpec((seq_len, wv), lambda b, i: (b, v_col))]
    args = [q_arr, k_arr, v_arr]
    has_ctx = ctx_k is not None
    if has_ctx:
        lc = ctx_k.shape[1]
        in_specs += [pl.BlockSpec((None, lc, wk), lambda b, i: (b, 0, 0)),
                     pl.BlockSpec((None, lc, wv), lambda b, i: (b, 0, 0))]
        args += [ctx_k, ctx_v]
    return pl.pallas_call(
        functools.partial(_attn_kernel, hq=hq, hkv=hkv, tk=tk, has_ctx=has_ctx, q_scale=q_scale,
                          v_ones=v_ones),
        out_shape=jax.ShapeDtypeStruct((nb * seq_len, wq), BF16),
        grid=(nb, qt),
        in_specs=in_specs,
        out_specs=pl.BlockSpec((tq, wq), lambda b, i: (b * qt + i, 0)),
        compiler_params=_params(("parallel", "arbitrary")),
        name=name,
    )(*args)


NAT_ROWS_PER_STEP = 2


def _nat_kernel(q_ref, k_ref, v_ref, ck_ref, cv_ref, *rest, rows):
    bias_refs, o_ref = rest[:NAT_ROWS_PER_STEP], rest[NAT_ROWS_PER_STEP]
    kh, w = NAT_KH, GRID_W
    nloc = kh * w
    pair_rows = 2 * w
    lane = lax.broadcasted_iota(jnp.int32, (w, LANES), 1)
    nt = (((1,), (1,)), ((), ()))
    q_scale = (HEAD_DIM ** -0.5) * LOG2E
    for mpair in range(NAT_HEADS // 2):
        ls = slice(mpair * LANES, (mpair + 1) * LANES)
        k_ctx = ck_ref[:, ls]
        v_ctx = cv_ref[:, ls]
        qs = []
        for j in range(NAT_ROWS_PER_STEP):
            qb = q_ref[j * w:(j + 1) * w, ls].astype(F32) * q_scale
            qs.append(jnp.concatenate([_place_half(qb, lane, 0, 0), _place_half(qb, lane, 1, 1)],
                                      axis=0).astype(BF16))
        s_ctx_all = lax.dot_general(jnp.concatenate(qs, axis=0), k_ctx, nt,
                                    preferred_element_type=F32)
        p_ctx, o_loc, denom = [], [], []
        for j in range(NAT_ROWS_PER_STEP):
            r = pl.program_id(1) * NAT_ROWS_PER_STEP + j
            k0 = pl.multiple_of(jnp.clip(r - kh // 2, 0, rows - kh) * w, w)
            s_loc = (lax.dot_general(qs[j], k_ref[pl.ds(k0, nloc), ls], nt,
                                     preferred_element_type=F32) + bias_refs[j][mpair])
            s_ctx = s_ctx_all[j * pair_rows:(j + 1) * pair_rows]
            mx = jnp.maximum(jnp.max(s_loc, axis=-1, keepdims=True),
                             jnp.max(s_ctx, axis=-1, keepdims=True))
            p_loc = jnp.exp2(s_loc - mx)
            pc = jnp.exp2(s_ctx - mx)
            denom.append(jnp.sum(p_loc, axis=-1, keepdims=True)
                         + jnp.sum(pc, axis=-1, keepdims=True))
            o_loc.append(jnp.dot(p_loc.astype(BF16), v_ref[pl.ds(k0, nloc), ls],
                                 preferred_element_type=F32))
            p_ctx.append(pc.astype(BF16))
        o_ctx_all = jnp.dot(jnp.concatenate(p_ctx, axis=0), v_ctx, preferred_element_type=F32)
        for j in range(NAT_ROWS_PER_STEP):
            o = (o_loc[j] + o_ctx_all[j * pair_rows:(j + 1) * pair_rows]) / denom[j]
            o_ref[j * w:(j + 1) * w, ls] = jnp.where(lane < HEAD_DIM, o[:w], o[w:]).astype(o_ref.dtype)


def _nat_bias(rpb):
    col = jnp.arange(GRID_W)
    cstart = jnp.clip(col - NAT_KW // 2, 0, GRID_W - NAT_KW)
    col_ok = (col[None, :] >= cstart[:, None]) & (col[None, :] < cstart[:, None] + NAT_KW)
    dc_idx = jnp.clip(col[None, :] - col[:, None] + NAT_KW - 1, 0, 2 * NAT_KW - 2)
    rpb_c = jnp.where(col_ok[None, None], rpb[:, :, dc_idx] * LOG2E, NEG_INF)
    variants = []
    for d0 in range(NAT_KH):
        blk = rpb_c[:, d0:d0 + NAT_KH].transpose(0, 2, 1, 3)
        variants.append(blk.reshape(NAT_HEADS // 2, 2 * GRID_W, NAT_KH * GRID_W))
    return jnp.stack(variants, axis=0)


def _nat_latent(qkv, ctx_k, ctx_v, rpb, *, nb, seq_len):
    rows = seq_len // GRID_W
    kh = NAT_KH
    rb = NAT_ROWS_PER_STEP
    bias = _nat_bias(rpb)
    lc = ctx_k.shape[1]
    wd = NAT_HEADS * HEAD_DIM
    steps = rows // rb

    def bias_spec(j):
        def idx(b, i):
            r = i * rb + j
            return (jnp.clip(r - kh // 2, 0, rows - kh) - r + kh - 1, 0, 0, 0)
        return pl.BlockSpec((None, NAT_HEADS // 2, 2 * GRID_W, kh * GRID_W), idx)

    return pl.pallas_call(
        functools.partial(_nat_kernel, rows=rows),
        out_shape=jax.ShapeDtypeStruct((nb * seq_len, wd), BF16),
        grid=(nb, steps),
        in_specs=[pl.BlockSpec((rb * GRID_W, wd), lambda b, i: (b * steps + i, 0)),
                  pl.BlockSpec((seq_len, wd), lambda b, i: (b, 1)),
                  pl.BlockSpec((seq_len, wd), lambda b, i: (b, 2)),
                  pl.BlockSpec((None, lc, wd), lambda b, i: (b, 0, 0)),
                  pl.BlockSpec((None, lc, wd), lambda b, i: (b, 0, 0))]
                 + [bias_spec(j) for j in range(rb)],
        out_specs=pl.BlockSpec((rb * GRID_W, wd), lambda b, i: (b * steps + i, 0)),
        compiler_params=_params(("parallel", "arbitrary"), NAT_VMEM_LIMIT_BYTES),
        name="nat_latent",
    )(qkv, qkv, qkv, ctx_k, ctx_v, *([bias] * rb))


def _ssd_layer(xp, xs, mods_p, mods_s, gain, j, np_, ns_, lp, ls, state_f, state_b,
               ssd_w_in, ssd_conv_w, ssd_conv_b, ssd_dt_bias, ssd_a_log, ssd_d, ssd_norm,
               ssd_w_out):
    w_zx = ssd_w_in[j, :, :SSD_ZX_DIM].astype(BF16)
    w_dt = jnp.pad(ssd_w_in[j, :, SSD_ZX_DIM:], ((0, 0), (0, LANES - 2 * SSD_HEADS))).astype(BF16)
    w_out = ssd_w_out[j].astype(BF16)
    pad = LANES - 2 * SSD_HEADS
    dt_bias = jnp.pad(ssd_dt_bias[j].reshape(-1), (0, pad)).reshape(1, LANES)
    a_log = jnp.pad(ssd_a_log[j].reshape(-1), (0, pad)).reshape(1, LANES)
    d_rep = jnp.repeat(ssd_d[j], HEAD_DIM).reshape(1, SSD_D_INNER)
    norm_g = ssd_norm[j].reshape(1, SSD_D_INNER)
    head_of_col = jnp.arange(SSD_D_INNER) // HEAD_DIM
    expand = [jnp.tile((jnp.arange(LANES)[:, None] == head_of_col[None, :] + off).astype(BF16), (2, 1))
              for off in (0, SSD_HEADS)]

    outs = []
    states = None
    for (x, mods, nseq, seq_len, is_prompt) in ((xp, mods_p, np_, lp, True),
                                                (xs, mods_s, ns_, ls, False)):
        tag = "p" if is_prompt else "s"
        zx, dt_raw = _mod_matmul(x, mods, gain, w_zx, w_side=w_dt,
                                 rows_per_group=seq_len if not is_prompt else x.shape[0],
                                 out_dtype=BF16, tm=1024, tn=1024, name="ssd_in_" + tag)
        xbc = _conv_silu(zx, ssd_conv_w[j], ssd_conv_b[j], seq_len=seq_len)
        common = dict(nseq=nseq, seq_len=seq_len, layer=j)
        fwd = _ssd_scan(xbc, dt_raw, dt_bias, a_log, expand[0], reverse=False,
                        h0=None if is_prompt else state_f, emit_state=is_prompt, d_rep=d_rep,
                        name="ssd_scan_fwd_" + tag, **common)
        bwd = _ssd_scan(xbc, dt_raw, dt_bias, a_log, expand[1], reverse=True,
                        h0=None if is_prompt else state_b, emit_state=is_prompt, zx=zx,
                        y_fwd=fwd[0], norm_g=norm_g, name="ssd_scan_bwd_" + tag, **common)
        if is_prompt:
            states = (fwd[1], bwd[1])
        outs.append(_proj_residual(bwd[0], x, mods,
                                   w_out, rows_per_group=seq_len if not is_prompt else x.shape[0],
                                   tm=512, name="ssd_out_" + tag))
    return outs[0], outs[1], states


def kernel(x_prompt, x_sample, c, state_ssd_fwd, state_ssd_bwd, cache_attn_k, cache_attn_v, cache_nat_k, cache_nat_v, c_ctx, ada_w, ada_b, norm_mix, norm_mlp, mlp_w1, mlp_w2, ssd_w_in, ssd_conv_w, ssd_conv_b, ssd_dt_bias, ssd_a_log, ssd_d, ssd_norm, ssd_w_out, attn_w_qkv, attn_q_norm, attn_k_norm, attn_w_out, nat_w_qkv, nat_rpb, nat_w_out, norm_final):
    np_, lp, d = x_prompt.shape
    ns_, ls, _ = x_sample.shape
    mp_rows, ms_rows = np_ * lp, ns_ * ls
    xp = x_prompt.reshape(mp_rows, d)
    xs = x_sample.reshape(ms_rows, d)

    cond_rows = 16
    cond = jnp.concatenate([c_ctx[None, :], c, jnp.zeros((cond_rows - 1 - ns_, d), F32)], axis=0)
    mods_all = _ada_mods(cond, ada_w, ada_b).reshape(DEPTH, cond_rows, 6, d)

    n_ssd = state_ssd_fwd.shape[1]
    st_f = state_ssd_fwd.reshape(ns_, n_ssd, SSD_D_INNER, SSD_STATE)
    st_b = state_ssd_bwd.reshape(ns_, n_ssd, SSD_D_INNER, SSD_STATE)

    sf_out, sb_out, ak_out, av_out, nk_out, nv_out = [], [], [], [], [], []
    for i in range(DEPTH):
        kind, j = i % N_MIXERS, i // N_MIXERS
        mods_p = mods_all[i, 0:1]
        mods_s = mods_all[i, 1:1 + ns_]
        rpg_p, rpg_s = mp_rows, ls
        if kind == 0:
            xp, xs, (hf, hb) = _ssd_layer(
                xp, xs, mods_p, mods_s, norm_mix[i], j, np_, ns_, lp, ls, st_f, st_b,
                ssd_w_in, ssd_conv_w, ssd_conv_b, ssd_dt_bias, ssd_a_log, ssd_d, ssd_norm,
                ssd_w_out)
            sf_out.append(hf.reshape(np_, SSD_HEADS, HEAD_DIM, SSD_STATE))
            sb_out.append(hb.reshape(np_, SSD_HEADS, HEAD_DIM, SSD_STATE))
        elif kind == 1:
            w_qkv = attn_w_qkv[j].astype(BF16)
            w_out = attn_w_out[j].astype(BF16)
            nk = ATTN_KV_HEADS * HEAD_DIM
            qkv = _mod_matmul(xp, mods_p, norm_mix[i], w_qkv, rows_per_group=rpg_p,
                              out_dtype=F32, tm=1024, tn=768, name="gqa_qkv_p")
            q, k, v, kc, vc = _qk_norm(qkv, attn_q_norm[j], attn_k_norm[j], seq_len=lp,
                                       rope=False, emit_cache=True)
            o = _attention(q, 0, k, 0, v, 0, None, None, nb=np_, seq_len=lp, hq=ATTN_HEADS,
                           hkv=ATTN_KV_HEADS, tq=128, tk=lp, q_scale=1.0, name="gqa_attn_p")
            xp = _proj_residual(o, xp, mods_p, w_out, rows_per_group=rpg_p, tm=512,
                                name="gqa_out_p")
            ak_out.append(kc.reshape(np_, lp, ATTN_KV_HEADS, HEAD_DIM))
            av_out.append(vc.reshape(np_, lp, ATTN_KV_HEADS, HEAD_DIM))
            qkv = _mod_matmul(xs, mods_s, norm_mix[i], w_qkv, rows_per_group=rpg_s,
                              out_dtype=F32, tm=1024, tn=768, name="gqa_qkv_s")
            q, k, v = _qk_norm(qkv, attn_q_norm[j], attn_k_norm[j], seq_len=ls, rope=True,
                               emit_cache=False)
            past = cache_attn_k.shape[2]
            ck = cache_attn_k[:, j].reshape(ns_, past, nk).astype(BF16)
            cv = _v_with_ones(cache_attn_v[:, j].reshape(ns_, past, nk)).astype(BF16)
            o = _attention(q, 0, k, 0, v, 0, ck, cv, nb=ns_, seq_len=ls, hq=ATTN_HEADS,
                           hkv=ATTN_KV_HEADS, tq=128, tk=2048, q_scale=1.0, name="gqa_attn_s",
                           v_ones=True)
            xs = _proj_residual(o, xs, mods_s, w_out, rows_per_group=rpg_s, tm=512,
                                name="gqa_out_s")
        else:
            w_qkv = nat_w_qkv[j].astype(BF16)
            w_out = nat_w_out[j].astype(BF16)
            wd = NAT_HEADS * HEAD_DIM
            scale = HEAD_DIM ** -0.5 * LOG2E
            qkv = _mod_matmul(xp, mods_p, norm_mix[i], w_qkv, rows_per_group=rpg_p,
                              out_dtype=F32, tm=1024, tn=1024, name="nat_qkv_p")
            o = _attention(qkv, 0, qkv, 1, qkv, 2, None, None, nb=np_, seq_len=lp, hq=NAT_HEADS,
                           hkv=NAT_HEADS, tq=lp, tk=lp, q_scale=scale, name="nat_attn_p")
            xp = _proj_residual(o, xp, mods_p, w_out, rows_per_group=rpg_p, tm=512,
                                name="nat_out_p")
            nk_out.append(qkv[:, wd:2 * wd].reshape(np_, lp, NAT_HEADS, HEAD_DIM))
            nv_out.append(qkv[:, 2 * wd:].reshape(np_, lp, NAT_HEADS, HEAD_DIM))
            qkv = _mod_matmul(xs, mods_s, norm_mix[i], w_qkv, rows_per_group=rpg_s,
                              out_dtype=BF16, tm=1024, tn=1024, name="nat_qkv_s")
            past = cache_nat_k.shape[2]
            ck = cache_nat_k[:, j].reshape(ns_, past, wd).astype(BF16)
            cv = cache_nat_v[:, j].reshape(ns_, past, wd).astype(BF16)
            o = _nat_latent(qkv, ck, cv, nat_rpb[j], nb=ns_, seq_len=ls)
            xs = _proj_residual(o, xs, mods_s, w_out, rows_per_group=rpg_s, tm=512,
                                name="nat_out_s")
        last = i == DEPTH - 1
        w1 = mlp_w1[i].astype(BF16)
        w2 = mlp_w2[i].astype(BF16)
        xp = _mlp(xp, mods_p, norm_mlp[i], w1, w2, norm_final, rows_per_group=rpg_p,
                  final_norm=last, name="mlp_p")
        xs = _mlp(xs, mods_s, norm_mlp[i], w1, w2, norm_final, rows_per_group=rpg_s,
                  final_norm=last, name="mlp_s")

    return (xp.reshape(np_, lp, d), xs.reshape(ns_, ls, d),
            jnp.stack(sf_out, axis=1), jnp.stack(sb_out, axis=1),
            jnp.stack(ak_out, axis=1), jnp.stack(av_out, axis=1),
            jnp.stack(nk_out, axis=1), jnp.stack(nv_out, axis=1))
```

```python
import functools
import math

import jax
import jax.numpy as jnp
from jax import lax
from jax.experimental import pallas as pl
from jax.experimental.pallas import tpu as pltpu

F32 = jnp.float32
BF16 = jnp.bfloat16

D_MODEL = 1024
DEPTH = 4
GRID_W = 64
N_MIXERS = 3
EPS = 1e-6
NEG_INF = -1e30
HEAD_DIM = 64
LANES = 128
LOG2E = math.log2(math.e)
SSD_D_INNER = 2 * D_MODEL
SSD_HEADS = SSD_D_INNER // HEAD_DIM
SSD_GROUPS = 4
SSD_HPG = SSD_HEADS // SSD_GROUPS
SSD_STATE = 128
SSD_CONV_W = 5
SSD_CHUNK = 128
SSD_CONV_DIM = SSD_D_INNER + 2 * SSD_GROUPS * SSD_STATE
SSD_ZX_DIM = SSD_D_INNER + SSD_CONV_DIM
ATTN_HEADS = D_MODEL // HEAD_DIM
ATTN_KV_HEADS = 4
ROPE_BASE = 10000.0
NAT_HEADS = D_MODEL // HEAD_DIM
NAT_KH = 8
NAT_KW = 16
MLP_HIDDEN = 4 * D_MODEL

VMEM_LIMIT_BYTES = 48 * 1024 * 1024
NAT_VMEM_LIMIT_BYTES = 56 * 1024 * 1024
MLP_VMEM_LIMIT_BYTES = 56 * 1024 * 1024


def _params(semantics, vmem=VMEM_LIMIT_BYTES):
    return pltpu.CompilerParams(dimension_semantics=semantics, vmem_limit_bytes=vmem)


def _sigmoid(x):
    return 1.0 / (1.0 + jnp.exp(-x))


def _modulated_norm(x, gain, shift, scale):
    ms = jnp.mean(x * x, axis=-1, keepdims=True)
    return (x * lax.rsqrt(ms + EPS) * gain) * (1.0 + scale) + shift


def _ada_kernel(c_ref, w_ref, b_ref, o_ref):
    c = c_ref[...]
    a = (c * _sigmoid(c)).astype(BF16)
    o_ref[...] = jnp.dot(a, w_ref[...].astype(BF16), preferred_element_type=F32) + b_ref[...]


def _ada_mods(cond, ada_w, ada_b):
    depth, d, n = ada_w.shape
    r = cond.shape[0]
    tn = 1024
    return pl.pallas_call(
        _ada_kernel,
        out_shape=jax.ShapeDtypeStruct((depth, r, n), F32),
        grid=(depth, n // tn),
        in_specs=[pl.BlockSpec((r, d), lambda l, j: (0, 0)),
                  pl.BlockSpec((None, d, tn), lambda l, j: (l, 0, j)),
                  pl.BlockSpec((None, 1, tn), lambda l, j: (l, 0, j))],
        out_specs=pl.BlockSpec((None, r, tn), lambda l, j: (l, 0, j)),
        compiler_params=_params(("parallel", "parallel")),
        name="ada_mods",
    )(cond, ada_w, ada_b.reshape(depth, 1, n))


def _modmm_kernel(*refs, shift_row, scale_row, has_side):
    if has_side:
        x_ref, mod_ref, g_ref, w_ref, ws_ref, o_ref, os_ref, h_ref = refs
    else:
        x_ref, mod_ref, g_ref, w_ref, o_ref, h_ref = refs

    @pl.when(pl.program_id(1) == 0)
    def _():
        h = _modulated_norm(x_ref[...], g_ref[...],
                            mod_ref[shift_row:shift_row + 1, :],
                            mod_ref[scale_row:scale_row + 1, :])
        h_ref[...] = h.astype(BF16)
        if has_side:
            os_ref[...] = jnp.dot(h_ref[...], ws_ref[...], preferred_element_type=F32)

    o_ref[...] = jnp.dot(h_ref[...], w_ref[...],
                         preferred_element_type=F32).astype(o_ref.dtype)


def _mod_matmul(x, mods, gain, w, *, rows_per_group, out_dtype, tm, tn, name, w_side=None):
    m, d = x.shape
    n = w.shape[1]
    tpg = rows_per_group // tm
    has_side = w_side is not None
    in_specs = [pl.BlockSpec((tm, d), lambda i, j: (i, 0)),
                pl.BlockSpec((None, 6, d), lambda i, j: (i // tpg, 0, 0)),
                pl.BlockSpec((1, d), lambda i, j: (0, 0)),
                pl.BlockSpec((d, tn), lambda i, j: (0, j))]
    args = [x, mods, gain.reshape(1, d), w]
    out_shape = jax.ShapeDtypeStruct((m, n), out_dtype)
    out_specs = pl.BlockSpec((tm, tn), lambda i, j: (i, j))
    if has_side:
        ns = w_side.shape[1]
        in_specs.append(pl.BlockSpec((d, ns), lambda i, j: (0, 0)))
        args.append(w_side)
        out_shape = [out_shape, jax.ShapeDtypeStruct((m, ns), F32)]
        out_specs = [out_specs, pl.BlockSpec((tm, ns), lambda i, j: (i, 0))]
    return pl.pallas_call(
        functools.partial(_modmm_kernel, shift_row=0, scale_row=1, has_side=has_side),
        out_shape=out_shape,
        grid=(m // tm, n // tn),
        in_specs=in_specs,
        out_specs=out_specs,
        scratch_shapes=[pltpu.VMEM((tm, d), BF16)],
        compiler_params=_params(("parallel", "arbitrary")),
        name=name,
    )(*args)


def _proj_res_kernel(a_ref, x_ref, mod_ref, w_ref, o_ref, *, gate_row):
    y = jnp.dot(a_ref[...], w_ref[...], preferred_element_type=F32)
    o_ref[...] = x_ref[...] + mod_ref[gate_row:gate_row + 1, :] * y


def _proj_residual(a, x, mods, w, *, rows_per_group, tm, name):
    m, k = a.shape
    d = x.shape[1]
    tpg = rows_per_group // tm
    return pl.pallas_call(
        functools.partial(_proj_res_kernel, gate_row=2),
        out_shape=jax.ShapeDtypeStruct((m, d), F32),
        grid=(m // tm,),
        in_specs=[pl.BlockSpec((tm, k), lambda i: (i, 0)),
                  pl.BlockSpec((tm, d), lambda i: (i, 0)),
                  pl.BlockSpec((None, 6, d), lambda i: (i // tpg, 0, 0)),
                  pl.BlockSpec((k, d), lambda i: (0, 0))],
        out_specs=pl.BlockSpec((tm, d), lambda i: (i, 0)),
        compiler_params=_params(("parallel",)),
        name=name,
    )(a, x, mods, w)


def _mlp_kernel(x_ref, xn_ref, mod_ref, modn_ref, g_ref, w1_ref, w2_ref, gf_ref, o_ref, h_ref,
                acc_ref, *, final_norm):
    i, j = pl.program_id(0), pl.program_id(1)
    last = pl.num_programs(1) - 1
    slot = i % 2

    def prepare(x_r, mod_r, dst):
        h = _modulated_norm(x_r[...], g_ref[...], mod_r[3:4, :], mod_r[4:5, :])
        h_ref[dst] = h.astype(BF16)

    def hidden_step():
        a = jnp.dot(h_ref[slot], w1_ref[...], preferred_element_type=F32)
        a = jnp.maximum(a, 0.0)
        return jnp.dot((a * a).astype(BF16), w2_ref[...], preferred_element_type=F32)

    @pl.when((i == 0) & (j == 0))
    def _():
        prepare(x_ref, mod_ref, 0)

    @pl.when(j == 0)
    def _():
        acc_ref[...] = hidden_step()

    @pl.when((j > 0) & (j < last))
    def _():
        acc_ref[...] += hidden_step()

    @pl.when(j == last)
    def _():
        y = x_ref[...] + mod_ref[5:6, :] * (acc_ref[...] + hidden_step())
        if final_norm:
            ms = jnp.mean(y * y, axis=-1, keepdims=True)
            y = y * lax.rsqrt(ms + EPS) * gf_ref[...]
        o_ref[...] = y
        prepare(xn_ref, modn_ref, 1 - slot)


def _mlp(x, mods, gain, w1, w2, gain_final, *, rows_per_group, final_norm, name):
    m, d = x.shape
    hdim = w1.shape[1]
    tm, th = 1024, 1024
    tpg = rows_per_group // tm
    ni = m // tm

    def nxt(i):
        return jnp.minimum(i + 1, ni - 1)

    return pl.pallas_call(
        functools.partial(_mlp_kernel, final_norm=final_norm),
        out_shape=jax.ShapeDtypeStruct((m, d), F32),
        grid=(ni, hdim // th),
        in_specs=[pl.BlockSpec((tm, d), lambda i, j: (i, 0)),
                  pl.BlockSpec((tm, d), lambda i, j: (nxt(i), 0)),
                  pl.BlockSpec((None, 6, d), lambda i, j: (i // tpg, 0, 0)),
                  pl.BlockSpec((None, 6, d), lambda i, j: (nxt(i) // tpg, 0, 0)),
                  pl.BlockSpec((1, d), lambda i, j: (0, 0)),
                  pl.BlockSpec((d, th), lambda i, j: (0, j)),
                  pl.BlockSpec((th, d), lambda i, j: (j, 0)),
                  pl.BlockSpec((1, d), lambda i, j: (0, 0))],
        out_specs=pl.BlockSpec((tm, d), lambda i, j: (i, 0)),
        scratch_shapes=[pltpu.VMEM((2, tm, d), BF16), pltpu.VMEM((tm, d), F32)],
        compiler_params=_params(("arbitrary", "arbitrary"), MLP_VMEM_LIMIT_BYTES),
        name=name,
    )(x, x, mods, mods, gain.reshape(1, d), w1, w2, gain_final.reshape(1, d))


CONV_HALO_ROWS = 16


CONV_TAPS_SHIFTED = (0, 1, 3, 4)


def _conv_kernel(prev_ref, cur_ref, next_ref, sh_ref, w_ref, b_ref, o_ref, *, tiles_per_seq):
    pos = pl.program_id(0) % tiles_per_seq
    cur_b = cur_ref[...]
    cur = cur_b.astype(F32)
    tr = cur.shape[0]
    h = CONV_HALO_ROWS
    w = w_ref[...]
    bias = b_ref[...]

    def silu(y):
        return (y / (1.0 + jnp.exp2(y * (-LOG2E)))).astype(o_ref.dtype)

    y = bias + w[2:3] * cur
    for idx, k in enumerate(CONV_TAPS_SHIFTED):
        y = y + w[k:k + 1] * jnp.dot(sh_ref[idx], cur_b, preferred_element_type=F32)
    o_ref[...] = silu(y)

    def edge(x):
        ye = bias + w[2:3] * x
        for k in CONV_TAPS_SHIFTED:
            ye = ye + w[k:k + 1] * pltpu.roll(x, (SSD_CONV_W // 2 - k) % x.shape[0], 0)
        return silu(ye)[h:2 * h]

    pv = jnp.where(pos == 0, 0.0, prev_ref[...].astype(F32))
    nx = jnp.where(pos == tiles_per_seq - 1, 0.0, next_ref[...].astype(F32))
    o_ref[0:h, :] = edge(jnp.concatenate([pv, cur[0:2 * h]], axis=0))
    o_ref[tr - h:tr, :] = edge(jnp.concatenate([cur[tr - 2 * h:tr], nx], axis=0))


def _conv_silu(zx, conv_w, conv_b, *, seq_len):
    m = zx.shape[0]
    tr, tc = 256, 1024
    c0 = SSD_D_INNER // tc
    halo_per_tile = tr // CONV_HALO_ROWS
    n_halo = m // CONV_HALO_ROWS
    t = jnp.arange(tr)
    shifts = jnp.stack([(t[None, :] == t[:, None] + (k - SSD_CONV_W // 2)).astype(BF16)
                        for k in CONV_TAPS_SHIFTED])
    return pl.pallas_call(
        functools.partial(_conv_kernel, tiles_per_seq=seq_len // tr),
        out_shape=jax.ShapeDtypeStruct((m, SSD_CONV_DIM), BF16),
        grid=(m // tr, SSD_CONV_DIM // tc),
        in_specs=[
            pl.BlockSpec((CONV_HALO_ROWS, tc),
                         lambda i, j: (jnp.maximum(i * halo_per_tile - 1, 0), c0 + j)),
            pl.BlockSpec((tr, tc), lambda i, j: (i, c0 + j)),
            pl.BlockSpec((CONV_HALO_ROWS, tc),
                         lambda i, j: (jnp.minimum((i + 1) * halo_per_tile, n_halo - 1), c0 + j)),
            pl.BlockSpec((len(CONV_TAPS_SHIFTED), tr, tr), lambda i, j: (0, 0, 0)),
            pl.BlockSpec((SSD_CONV_W, tc), lambda i, j: (0, j)),
            pl.BlockSpec((1, tc), lambda i, j: (0, j))],
        out_specs=pl.BlockSpec((tr, tc), lambda i, j: (i, j)),
        compiler_params=_params(("parallel", "parallel")),
        name="ssd_conv_silu",
    )(zx, zx, zx, shifts, conv_w, conv_b.reshape(1, SSD_CONV_DIM))


def _split3(a):
    a1 = a.astype(BF16)
    r1 = a - a1.astype(F32)
    a2 = r1.astype(BF16)
    a3 = (r1 - a2.astype(F32)).astype(BF16)
    return a1, a2, a3


def _dot_exact_left(m01, a):
    return sum(jnp.dot(m01, p, preferred_element_type=F32) for p in _split3(a))


def _spread_heads(a, e2_ref):
    hi = a.astype(BF16)
    lo = (a - hi.astype(F32)).astype(BF16)
    return jnp.dot(jnp.concatenate([hi, lo], axis=1), e2_ref[...], preferred_element_type=F32)


SSD_SEQS_PER_STEP = 2


def _ssd_decays(dt_ref, dtb_ref, alog_ref, *, reverse):
    q = SSD_CHUNK
    pre = dt_ref[...] + dtb_ref[...]
    dt = jnp.maximum(pre, 0.0) + jnp.log1p(jnp.exp(-jnp.abs(pre)))
    a_dt = dt * (-jnp.exp(alog_ref[...]) * LOG2E)
    ri = lax.broadcasted_iota(jnp.int32, (q, q), 0)
    ci = lax.broadcasted_iota(jnp.int32, (q, q), 1)
    keep = (ri <= ci) if reverse else (ri >= ci)
    cs = _dot_exact_left(jnp.where(keep, 1.0, 0.0).astype(BF16), a_dt)
    edge = cs[0:1, :] if reverse else cs[q - 1:q, :]
    factors = jnp.concatenate([jnp.exp2(cs), jnp.exp2(edge - cs) * dt], axis=0)
    return dict(keep=keep, cs=cs, cs_t=cs.T, dt_t=dt.T, factors=factors)


def _ssd_chunk(xbc_ref, decays, exp_cs, to_end, z_ref, yf_ref, ng_ref, drep_ref,
               y_ref, s_ref, ybuf_ref, *, reverse, combine):
    q = SSD_CHUNK
    gw = SSD_HPG * HEAD_DIM
    col0 = SSD_HEADS if reverse else 0
    keep, cs, cs_t, dt_t = decays["keep"], decays["cs"], decays["cs_t"], decays["dt_t"]
    exp_edge = exp_cs[0:1, :] if reverse else exp_cs[q - 1:q, :]
    lane = lax.broadcasted_iota(jnp.int32, (1, LANES), 1)
    keep_lo = jnp.where(lane < HEAD_DIM, 1.0, 0.0).astype(BF16)
    keep_hi = jnp.where(lane < HEAD_DIM, 0.0, 1.0).astype(BF16)

    for g in range(SSD_GROUPS):
        gs = slice(g * gw, (g + 1) * gw)
        b_g = xbc_ref[:, SSD_D_INNER + g * SSD_STATE:SSD_D_INNER + (g + 1) * SSD_STATE]
        c_off = SSD_D_INNER + SSD_GROUPS * SSD_STATE
        c_g = xbc_ref[:, c_off + g * SSD_STATE:c_off + (g + 1) * SSD_STATE]
        cb = lax.dot_general(c_g, b_g, (((1,), (1,)), ((), ())), preferred_element_type=F32)
        cb = jnp.where(keep, cb, 0.0)
        s_g = s_ref[:, gs]
        y_off = jnp.dot(c_g, s_g.astype(BF16), preferred_element_type=F32)
        for mp in range(gw // LANES):
            m = g * (gw // LANES) + mp
            ls = slice(m * LANES, (m + 1) * LANES)
            x_pair = xbc_ref[:, ls]
            k0 = col0 + 2 * m
            ws = []
            for k in (k0, k0 + 1):
                seg = jnp.minimum(cs[:, k:k + 1] - cs_t[k:k + 1, :], 0.0)
                ws.append((jnp.exp2(seg) * cb * dt_t[k:k + 1, :]).astype(BF16))
            y_pair = (jnp.dot(jnp.concatenate(ws, axis=1),
                              jnp.concatenate([x_pair * keep_lo, x_pair * keep_hi], axis=0),
                              preferred_element_type=F32)
                      + y_off[:, mp * LANES:(mp + 1) * LANES] * exp_cs[:, ls])
            if not combine:
                y_pair = y_pair + drep_ref[:, ls] * x_pair.astype(F32)
            ybuf_ref[:, ls] = y_pair
        xs_g = (to_end[:, gs] * xbc_ref[:, gs].astype(F32)).astype(BF16)
        b_t = b_g.astype(F32).T.astype(BF16)
        s_ref[:, gs] = s_g * exp_edge[:, gs] + jnp.dot(b_t, xs_g, preferred_element_type=F32)

    if combine:
        z = z_ref[...].astype(F32)
        gated = (ybuf_ref[...] + yf_ref[...]) * (z * _sigmoid(z))
        ms = jnp.mean(gated * gated, axis=-1, keepdims=True)
        y_ref[...] = (gated * lax.rsqrt(ms + EPS) * ng_ref[...]).astype(y_ref.dtype)


def _ssd_kernel(*refs, reverse, has_h0, emit_state, combine):
    it = iter(refs)
    xbc_ref, dt_ref, dtb_ref, alog_ref, e_ref = (next(it) for _ in range(5))
    h0_ref = next(it) if has_h0 else None
    z_ref = yf_ref = ng_ref = drep_ref = None
    if combine:
        z_ref, yf_ref, ng_ref = next(it), next(it), next(it)
    else:
        drep_ref = next(it)
    y_ref = next(it)
    st_ref = next(it) if emit_state else None
    s_ref = next(it)
    ybuf_ref = next(it) if combine else y_ref
    q = SSD_CHUNK
    c = pl.program_id(1)
    seqs = range(SSD_SEQS_PER_STEP)

    @pl.when(c == 0)
    def _():
        if has_h0:
            for sq in seqs:
                for k in range(SSD_D_INNER // q):
                    s_ref[sq, :, k * q:(k + 1) * q] = h0_ref[sq, k * q:(k + 1) * q, :].T
        else:
            s_ref[...] = jnp.zeros_like(s_ref)

    decays = [_ssd_decays(dt_ref.at[sq], dtb_ref, alog_ref, reverse=reverse) for sq in seqs]
    spread = _spread_heads(jnp.concatenate([d["factors"] for d in decays], axis=0), e_ref)
    for sq in seqs:
        r0 = sq * 2 * q
        _ssd_chunk(xbc_ref.at[sq], decays[sq], spread[r0:r0 + q], spread[r0 + q:r0 + 2 * q],
                   z_ref.at[sq] if combine else None, yf_ref.at[sq] if combine else None,
                   ng_ref, drep_ref, y_ref.at[sq], s_ref.at[sq], ybuf_ref.at[sq],
                   reverse=reverse, combine=combine)

    if emit_state:
        @pl.when(c == pl.num_programs(1) - 1)
        def _():
            for sq in seqs:
                for k in range(SSD_D_INNER // q):
                    st_ref[sq, k * q:(k + 1) * q, :] = s_ref[sq, :, k * q:(k + 1) * q].T


def _ssd_scan(xbc, dt_raw, dt_bias, a_log, expand, *, nseq, seq_len, reverse, h0=None,
              layer=0, emit_state=False, zx=None, y_fwd=None, norm_g=None, d_rep=None, name):
    combine = zx is not None
    nchunk = seq_len // SSD_CHUNK
    q = SSD_CHUNK
    sp = SSD_SEQS_PER_STEP
    nb = nseq // sp

    def per_seq(a):
        return a.reshape(nb, sp, seq_len, a.shape[-1])

    def chunk_spec(width):
        return pl.BlockSpec((None, sp, q, width),
                            lambda b, c: (b, 0, (nchunk - 1 - c) if reverse else c, 0))

    def const_spec(shape):
        return pl.BlockSpec(shape, lambda b, c: (0,) * len(shape))

    in_specs = [chunk_spec(SSD_CONV_DIM), chunk_spec(LANES), const_spec((1, LANES)),
                const_spec((1, LANES)), const_spec((2 * LANES, SSD_D_INNER))]
    args = [per_seq(xbc), per_seq(dt_raw), dt_bias, a_log, expand]
    if h0 is not None:
        in_specs.append(pl.BlockSpec((None, sp, None, SSD_D_INNER, SSD_STATE),
                                     lambda b, c: (b, 0, layer, 0, 0)))
        args.append(h0.reshape((nb, sp) + h0.shape[1:]))
    if combine:
        in_specs += [chunk_spec(SSD_D_INNER), chunk_spec(SSD_D_INNER),
                     const_spec((1, SSD_D_INNER))]
        args += [per_seq(zx), per_seq(y_fwd), norm_g]
    else:
        in_specs.append(const_spec((1, SSD_D_INNER)))
        args.append(d_rep)
    out_shape = [jax.ShapeDtypeStruct((nb, sp, seq_len, SSD_D_INNER), BF16 if combine else F32)]
    out_specs = [chunk_spec(SSD_D_INNER)]
    if emit_state:
        out_shape.append(jax.ShapeDtypeStruct((nb, sp, SSD_D_INNER, SSD_STATE), F32))
        out_specs.append(pl.BlockSpec((None, sp, SSD_D_INNER, SSD_STATE),
                                      lambda b, c: (b, 0, 0, 0)))
    scratch = [pltpu.VMEM((sp, SSD_STATE, SSD_D_INNER), F32)]
    if combine:
        scratch.append(pltpu.VMEM((sp, q, SSD_D_INNER), F32))
    outs = list(pl.pallas_call(
        functools.partial(_ssd_kernel, reverse=reverse, has_h0=h0 is not None,
                          emit_state=emit_state, combine=combine),
        out_shape=out_shape,
        grid=(nb, nchunk),
        in_specs=in_specs,
        out_specs=out_specs,
        scratch_shapes=scratch,
        compiler_params=_params(("parallel", "arbitrary")),
        name=name,
    )(*args))
    outs[0] = outs[0].reshape(nseq * seq_len, SSD_D_INNER)
    if emit_state:
        outs[1] = outs[1].reshape(nseq, SSD_D_INNER, SSD_STATE)
    return outs


def _head_rms(x, gain_row):
    rows, width = x.shape
    lane = lax.broadcasted_iota(jnp.int32, (rows, LANES), 1)
    lo = lane < HEAD_DIM
    outs = []
    for j in range(width // LANES):
        xb = x[:, j * LANES:(j + 1) * LANES]
        sq = xb * xb
        s_lo = jnp.sum(jnp.where(lo, sq, 0.0), axis=-1, keepdims=True)
        s_hi = jnp.sum(jnp.where(lo, 0.0, sq), axis=-1, keepdims=True)
        ms = jnp.where(lo, s_lo, s_hi) * (1.0 / HEAD_DIM)
        outs.append(xb * lax.rsqrt(ms + EPS))
    return jnp.concatenate(outs, axis=1) * gain_row


def _rope(x, cos, sin_signed):
    rows, width = x.shape
    lane = lax.broadcasted_iota(jnp.int32, (rows, LANES), 1)
    even = (lane % 2) == 0
    outs = []
    for j in range(width // LANES):
        xb = x[:, j * LANES:(j + 1) * LANES]
        partner = jnp.where(even, pltpu.roll(xb, LANES - 1, 1), pltpu.roll(xb, 1, 1))
        outs.append(xb * cos + partner * sin_signed)
    return jnp.concatenate(outs, axis=1)


def _qknorm_kernel(*refs, rope, emit_cache):
    it = iter(refs)
    q_ref, k_ref, v_ref, qg_ref, kg_ref = (next(it) for _ in range(5))
    if rope:
        cos_ref, sin_ref = next(it), next(it)
    qo_ref, ko_ref, vo_ref = next(it), next(it), next(it)
    q = _head_rms(q_ref[...], qg_ref[...])
    k = _head_rms(k_ref[...], kg_ref[...])
    if emit_cache:
        kc_ref, vc_ref = next(it), next(it)
        kc_ref[...] = k
        vc_ref[...] = v_ref[...]
    if rope:
        q = _rope(q, cos_ref[...], sin_ref[...])
        k = _rope(k, cos_ref[...], sin_ref[...])
    qo_ref[...] = (q * (HEAD_DIM ** -0.5 * LOG2E)).astype(BF16)
    ko_ref[...] = k.astype(BF16)
    v = v_ref[...]
    vo_ref[...] = (_v_with_ones(v) if rope else v).astype(BF16)


def _rope_tables(seq_len):
    n_pairs = HEAD_DIM // 4
    t = jnp.arange(seq_len)
    row = (t // GRID_W).astype(F32)
    col = (t % GRID_W).astype(F32)
    freqs = ROPE_BASE ** (-jnp.arange(n_pairs, dtype=F32) / n_pairs)
    ang = jnp.concatenate([row[:, None] * freqs, col[:, None] * freqs], axis=-1)
    cos = jnp.repeat(jnp.cos(ang), 2, axis=-1)
    sin = jnp.repeat(jnp.sin(ang), 2, axis=-1)
    sign = jnp.where(jnp.arange(HEAD_DIM) % 2 == 0, -1.0, 1.0).astype(F32)
    return jnp.tile(cos, (1, 2)), jnp.tile(sin * sign, (1, 2))


def _qk_norm(qkv, q_gain, k_gain, *, seq_len, rope, emit_cache):
    m = qkv.shape[0]
    tm = 256
    nq = ATTN_HEADS * HEAD_DIM
    nk = ATTN_KV_HEADS * HEAD_DIM
    qg = jnp.tile(q_gain, ATTN_HEADS).reshape(1, nq)
    kg = jnp.tile(k_gain, ATTN_KV_HEADS).reshape(1, nk)
    in_specs = [pl.BlockSpec((tm, nq), lambda i: (i, 0)),
                pl.BlockSpec((tm, nk), lambda i: (i, nq // nk)),
                pl.BlockSpec((tm, nk), lambda i: (i, nq // nk + 1)),
                pl.BlockSpec((1, nq), lambda i: (0, 0)),
                pl.BlockSpec((1, nk), lambda i: (0, 0))]
    args = [qkv, qkv, qkv, qg, kg]
    if rope:
        cos, sin = _rope_tables(seq_len)
        tps = seq_len // tm
        in_specs += [pl.BlockSpec((tm, LANES), lambda i: (i % tps, 0)),
                     pl.BlockSpec((tm, LANES), lambda i: (i % tps, 0))]
        args += [cos, sin]
    out_shape = [jax.ShapeDtypeStruct((m, nq), BF16), jax.ShapeDtypeStruct((m, nk), BF16),
                 jax.ShapeDtypeStruct((m, nk), BF16)]
    out_specs = [pl.BlockSpec((tm, nq), lambda i: (i, 0)), pl.BlockSpec((tm, nk), lambda i: (i, 0)),
                 pl.BlockSpec((tm, nk), lambda i: (i, 0))]
    if rope:
        out_shape[2] = jax.ShapeDtypeStruct((m, ATTN_KV_HEADS * LANES), BF16)
        out_specs[2] = pl.BlockSpec((tm, ATTN_KV_HEADS * LANES), lambda i: (i, 0))
    if emit_cache:
        out_shape += [jax.ShapeDtypeStruct((m, nk), F32)] * 2
        out_specs += [pl.BlockSpec((tm, nk), lambda i: (i, 0))] * 2
    return pl.pallas_call(
        functools.partial(_qknorm_kernel, rope=rope, emit_cache=emit_cache),
        out_shape=out_shape,
        grid=(m // tm,),
        in_specs=in_specs,
        out_specs=out_specs,
        compiler_params=_params(("parallel",)),
        name="gqa_qk_norm_rope" if rope else "gqa_qk_norm",
    )(*args)


def _place_half(x, lane, src_half, dst_half):
    if src_half != dst_half:
        x = pltpu.roll(x, HEAD_DIM, 1)
    keep = (lane >= HEAD_DIM) if dst_half else (lane < HEAD_DIM)
    return jnp.where(keep, x, 0.0)


def _v_with_ones(v):
    lane = lax.broadcasted_iota(jnp.int32, v.shape[:-1] + (LANES,), v.ndim - 1)
    tiles = []
    for g in range(v.shape[-1] // HEAD_DIM):
        pair = v[..., (g // 2) * LANES:(g // 2 + 1) * LANES]
        own = (lane >= HEAD_DIM) if g % 2 else (lane < HEAD_DIM)
        tiles.append(jnp.where(own, pair, 1.0))
    return jnp.concatenate(tiles, axis=-1)


def _attn_kernel(*refs, hq, hkv, tk, has_ctx, q_scale, v_ones):
    if has_ctx:
        q_ref, k_ref, v_ref, ck_ref, cv_ref, o_ref = refs
    else:
        q_ref, k_ref, v_ref, o_ref = refs
    grp = hq // hkv
    tq = q_ref.shape[0]
    rows = grp * tq
    lane = lax.broadcasted_iota(jnp.int32, (tq, LANES), 1)
    nt = (((1,), (1,)), ((), ()))

    qst = []
    for g in range(hkv):
        parts = []
        for u in range(grp):
            h = g * grp + u
            qb = q_ref[:, (h // 2) * LANES:(h // 2 + 1) * LANES].astype(F32)
            if q_scale != 1.0:
                qb = qb * q_scale
            parts.append(_place_half(qb, lane, h % 2, g % 2).astype(BF16))
        qst.append(parts[0] if grp == 1 else jnp.concatenate(parts, axis=0))

    def step(g, kblk, vblk, carry):
        m_i, l_i, acc = carry
        s = lax.dot_general(qst[g], kblk.astype(BF16), nt, preferred_element_type=F32)
        m_n = jnp.maximum(m_i, jnp.max(s, axis=-1, keepdims=True))
        alpha = jnp.exp2(m_i - m_n)
        p = jnp.exp2(s - m_n)
        acc = alpha * acc + jnp.dot(p.astype(BF16), vblk.astype(BF16),
                                    preferred_element_type=F32)
        if v_ones:
            return m_n, l_i, acc
        return m_n, alpha * l_i + jnp.sum(p, axis=-1, keepdims=True), acc

    def all_heads(kv_at, carries):
        out = []
        for g in range(hkv):
            kc = slice((g // 2) * LANES, (g // 2 + 1) * LANES)
            vc = slice(g * LANES, (g + 1) * LANES) if v_ones else kc
            out.append(step(g, *kv_at(kc, vc), carries[g]))
        return tuple(out)

    carries = tuple((jnp.full((rows, 1), -jnp.inf, F32), jnp.zeros((rows, 1), F32),
                     jnp.zeros((rows, LANES), F32)) for _ in range(hkv))
    n_self = k_ref.shape[0] // tk
    if n_self == 1:
        carries = all_heads(lambda kc, vc: (k_ref[:, kc], v_ref[:, vc]), carries)
    else:
        def body(ci, car):
            r0 = pl.multiple_of(ci * tk, tk)
            return all_heads(lambda kc, vc: (k_ref[pl.ds(r0, tk), kc], v_ref[pl.ds(r0, tk), vc]),
                             car)
        carries = lax.fori_loop(0, n_self, body, carries, unroll=2)
    if has_ctx:
        carries = all_heads(lambda kc, vc: (ck_ref[:, kc], cv_ref[:, vc]), carries)

    head_out = [None] * hq
    for g in range(hkv):
        _, l_f, acc = carries[g]
        if v_ones:
            ones_lane = HEAD_DIM * (1 - g % 2)
            l_f = acc[:, ones_lane:ones_lane + 1]
        o = acc / l_f
        for u in range(grp):
            head_out[g * grp + u] = o[u * tq:(u + 1) * tq]
    for mpair in range(hq // 2):
        oa, ob = head_out[2 * mpair], head_out[2 * mpair + 1]
        if ((2 * mpair) // grp) % 2 != 0:
            oa = pltpu.roll(oa, HEAD_DIM, 1)
        if ((2 * mpair + 1) // grp) % 2 != 1:
            ob = pltpu.roll(ob, HEAD_DIM, 1)
        o_ref[:, mpair * LANES:(mpair + 1) * LANES] = jnp.where(
            lane < HEAD_DIM, oa, ob).astype(o_ref.dtype)


def _attention(q_arr, q_col, k_arr, k_col, v_arr, v_col, ctx_k, ctx_v, *, nb, seq_len,
               hq, hkv, tq, tk, q_scale, name, v_ones=False):
    wq, wk = hq * HEAD_DIM, hkv * HEAD_DIM
    wv = hkv * LANES if v_ones else wk
    qt = seq_len // tq
    in_specs = [pl.BlockSpec((tq, wq), lambda b, i: (b * qt + i, q_col)),
                pl.BlockSpec((seq_len, wk), lambda b, i: (b, k_col)),
                pl.BlockSpec((seq_len, wv), lambda b, i: (b, v_col))]
    args = [q_arr, k_arr, v_arr]
    has_ctx = ctx_k is not None
    if has_ctx:
        lc = ctx_k.shape[1]
        in_specs += [pl.BlockSpec((None, lc, wk), lambda b, i: (b, 0, 0)),
                     pl.BlockSpec((None, lc, wv), lambda b, i: (b, 0, 0))]
        args += [ctx_k, ctx_v]
    return pl.pallas_call(
        functools.partial(_attn_kernel, hq=hq, hkv=hkv, tk=tk, has_ctx=has_ctx, q_scale=q_scale,
                          v_ones=v_ones),
        out_shape=jax.ShapeDtypeStruct((nb * seq_len, wq), BF16),
        grid=(nb, qt),
        in_specs=in_specs,
        out_specs=pl.BlockSpec((tq, wq), lambda b, i: (b * qt + i, 0)),
        compiler_params=_params(("parallel", "arbitrary")),
        name=name,
    )(*args)


NAT_ROWS_PER_STEP = 4


def _nat_kernel(q_ref, k_ref, v_ref, ck_ref, cv_ref, *rest, rows):
    bias_refs, o_ref = rest[:NAT_ROWS_PER_STEP], rest[NAT_ROWS_PER_STEP]
    kh, w = NAT_KH, GRID_W
    nloc = kh * w
    pair_rows = 2 * w
    lane = lax.broadcasted_iota(jnp.int32, (w, LANES), 1)
    nt = (((1,), (1,)), ((), ()))
    q_scale = (HEAD_DIM ** -0.5) * LOG2E
    for mpair in range(NAT_HEADS // 2):
        ls = slice(mpair * LANES, (mpair + 1) * LANES)
        k_ctx = ck_ref[:, ls]
        v_ctx = cv_ref[:, ls]
        qs = []
        for j in range(NAT_ROWS_PER_STEP):
            qb = q_ref[j * w:(j + 1) * w, ls].astype(F32) * q_scale
            qs.append(jnp.concatenate([_place_half(qb, lane, 0, 0), _place_half(qb, lane, 1, 1)],
                                      axis=0).astype(BF16))
        s_ctx_all = lax.dot_general(jnp.concatenate(qs, axis=0), k_ctx, nt,
                                    preferred_element_type=F32)
        p_ctx, o_loc, denom = [], [], []
        for j in range(NAT_ROWS_PER_STEP):
            r = pl.program_id(1) * NAT_ROWS_PER_STEP + j
            k0 = pl.multiple_of(jnp.clip(r - kh // 2, 0, rows - kh) * w, w)
            s_loc = (lax.dot_general(qs[j], k_ref[pl.ds(k0, nloc), ls], nt,
                                     preferred_element_type=F32) + bias_refs[j][mpair])
            s_ctx = s_ctx_all[j * pair_rows:(j + 1) * pair_rows]
            mx = jnp.maximum(jnp.max(s_loc, axis=-1, keepdims=True),
                             jnp.max(s_ctx, axis=-1, keepdims=True))
            p_loc = jnp.exp2(s_loc - mx)
            pc = jnp.exp2(s_ctx - mx)
            denom.append(jnp.sum(p_loc, axis=-1, keepdims=True)
                         + jnp.sum(pc, axis=-1, keepdims=True))
            o_loc.append(jnp.dot(p_loc.astype(BF16), v_ref[pl.ds(k0, nloc), ls],
                                 preferred_element_type=F32))
            p_ctx.append(pc.astype(BF16))
        o_ctx_all = jnp.dot(jnp.concatenate(p_ctx, axis=0), v_ctx, preferred_element_type=F32)
        for j in range(NAT_ROWS_PER_STEP):
            o = (o_loc[j] + o_ctx_all[j * pair_rows:(j + 1) * pair_rows]) / denom[j]
            o_ref[j * w:(j + 1) * w, ls] = jnp.where(lane < HEAD_DIM, o[:w], o[w:]).astype(o_ref.dtype)


def _nat_bias(rpb):
    col = jnp.arange(GRID_W)
    cstart = jnp.clip(col - NAT_KW // 2, 0, GRID_W - NAT_KW)
    col_ok = (col[None, :] >= cstart[:, None]) & (col[None, :] < cstart[:, None] + NAT_KW)
    dc_idx = jnp.clip(col[None, :] - col[:, None] + NAT_KW - 1, 0, 2 * NAT_KW - 2)
    rpb_c = jnp.where(col_ok[None, None], rpb[:, :, dc_idx] * LOG2E, NEG_INF)
    variants = []
    for d0 in range(NAT_KH):
        blk = rpb_c[:, d0:d0 + NAT_KH].transpose(0, 2, 1, 3)
        variants.append(blk.reshape(NAT_HEADS // 2, 2 * GRID_W, NAT_KH * GRID_W))
    return jnp.stack(variants, axis=0).astype(BF16)


def _nat_latent(qkv, ctx_k, ctx_v, rpb, *, nb, seq_len):
    rows = seq_len // GRID_W
    kh = NAT_KH
    rb = NAT_ROWS_PER_STEP
    bias = _nat_bias(rpb)
    lc = ctx_k.shape[1]
    wd = NAT_HEADS * HEAD_DIM
    steps = rows // rb

    def bias_spec(j):
        def idx(b, i):
            r = i * rb + j
            return (jnp.clip(r - kh // 2, 0, rows - kh) - r + kh - 1, 0, 0, 0)
        return pl.BlockSpec((None, NAT_HEADS // 2, 2 * GRID_W, kh * GRID_W), idx)

    return pl.pallas_call(
        functools.partial(_nat_kernel, rows=rows),
        out_shape=jax.ShapeDtypeStruct((nb * seq_len, wd), BF16),
        grid=(nb, steps),
        in_specs=[pl.BlockSpec((rb * GRID_W, wd), lambda b, i: (b * steps + i, 0)),
                  pl.BlockSpec((seq_len, wd), lambda b, i: (b, 1), pipeline_mode=pl.Buffered(1)),
                  pl.BlockSpec((seq_len, wd), lambda b, i: (b, 2), pipeline_mode=pl.Buffered(1)),
                  pl.BlockSpec((None, lc, wd), lambda b, i: (b, 0, 0)),
                  pl.BlockSpec((None, lc, wd), lambda b, i: (b, 0, 0))]
                 + [bias_spec(j) for j in range(rb)],
        out_specs=pl.BlockSpec((rb * GRID_W, wd), lambda b, i: (b * steps + i, 0)),
        compiler_params=_params(("parallel", "arbitrary"), NAT_VMEM_LIMIT_BYTES),
        name="nat_latent",
    )(qkv, qkv, qkv, ctx_k, ctx_v, *([bias] * rb))


def _ssd_layer(xp, xs, mods_p, mods_s, gain, j, np_, ns_, lp, ls, state_f, state_b,
               ssd_w_in, ssd_conv_w, ssd_conv_b, ssd_dt_bias, ssd_a_log, ssd_d, ssd_norm,
               ssd_w_out):
    w_zx = ssd_w_in[j, :, :SSD_ZX_DIM].astype(BF16)
    w_dt = jnp.pad(ssd_w_in[j, :, SSD_ZX_DIM:], ((0, 0), (0, LANES - 2 * SSD_HEADS))).astype(BF16)
    w_out = ssd_w_out[j].astype(BF16)
    pad = LANES - 2 * SSD_HEADS
    dt_bias = jnp.pad(ssd_dt_bias[j].reshape(-1), (0, pad)).reshape(1, LANES)
    a_log = jnp.pad(ssd_a_log[j].reshape(-1), (0, pad)).reshape(1, LANES)
    d_rep = jnp.repeat(ssd_d[j], HEAD_DIM).reshape(1, SSD_D_INNER)
    norm_g = ssd_norm[j].reshape(1, SSD_D_INNER)
    head_of_col = jnp.arange(SSD_D_INNER) // HEAD_DIM
    expand = [jnp.tile((jnp.arange(LANES)[:, None] == head_of_col[None, :] + off).astype(BF16), (2, 1))
              for off in (0, SSD_HEADS)]

    outs = []
    states = None
    for (x, mods, nseq, seq_len, is_prompt) in ((xp, mods_p, np_, lp, True),
                                                (xs, mods_s, ns_, ls, False)):
        tag = "p" if is_prompt else "s"
        zx, dt_raw = _mod_matmul(x, mods, gain, w_zx, w_side=w_dt,
                                 rows_per_group=seq_len if not is_prompt else x.shape[0],
                                 out_dtype=BF16, tm=1024, tn=1024, name="ssd_in_" + tag)
        xbc = _conv_silu(zx, ssd_conv_w[j], ssd_conv_b[j], seq_len=seq_len)
        common = dict(nseq=nseq, seq_len=seq_len, layer=j)
        fwd = _ssd_scan(xbc, dt_raw, dt_bias, a_log, expand[0], reverse=False,
                        h0=None if is_prompt else state_f, emit_state=is_prompt, d_rep=d_rep,
                        name="ssd_scan_fwd_" + tag, **common)
        bwd = _ssd_scan(xbc, dt_raw, dt_bias, a_log, expand[1], reverse=True,
                        h0=None if is_prompt else state_b, emit_state=is_prompt, zx=zx,
                        y_fwd=fwd[0], norm_g=norm_g, name="ssd_scan_bwd_" + tag, **common)
        if is_prompt:
            states = (fwd[1], bwd[1])
        outs.append(_proj_residual(bwd[0], x, mods,
                                   w_out, rows_per_group=seq_len if not is_prompt else x.shape[0],
                                   tm=512, name="ssd_out_" + tag))
    return outs[0], outs[1], states


def kernel(x_prompt, x_sample, c, state_ssd_fwd, state_ssd_bwd, cache_attn_k, cache_attn_v, cache_nat_k, cache_nat_v, c_ctx, ada_w, ada_b, norm_mix, norm_mlp, mlp_w1, mlp_w2, ssd_w_in, ssd_conv_w, ssd_conv_b, ssd_dt_bias, ssd_a_log, ssd_d, ssd_norm, ssd_w_out, attn_w_qkv, attn_q_norm, attn_k_norm, attn_w_out, nat_w_qkv, nat_rpb, nat_w_out, norm_final):
    np_, lp, d = x_prompt.shape
    ns_, ls, _ = x_sample.shape
    mp_rows, ms_rows = np_ * lp, ns_ * ls
    xp = x_prompt.reshape(mp_rows, d)
    xs = x_sample.reshape(ms_rows, d)

    cond_rows = 16
    cond = jnp.concatenate([c_ctx[None, :], c, jnp.zeros((cond_rows - 1 - ns_, d), F32)], axis=0)
    mods_all = _ada_mods(cond, ada_w, ada_b).reshape(DEPTH, cond_rows, 6, d)

    n_ssd = state_ssd_fwd.shape[1]
    st_f = state_ssd_fwd.reshape(ns_, n_ssd, SSD_D_INNER, SSD_STATE)
    st_b = state_ssd_bwd.reshape(ns_, n_ssd, SSD_D_INNER, SSD_STATE)

    sf_out, sb_out, ak_out, av_out, nk_out, nv_out = [], [], [], [], [], []
    for i in range(DEPTH):
        kind, j = i % N_MIXERS, i // N_MIXERS
        mods_p = mods_all[i, 0:1]
        mods_s = mods_all[i, 1:1 + ns_]
        rpg_p, rpg_s = mp_rows, ls
        if kind == 0:
            xp, xs, (hf, hb) = _ssd_layer(
                xp, xs, mods_p, mods_s, norm_mix[i], j, np_, ns_, lp, ls, st_f, st_b,
                ssd_w_in, ssd_conv_w, ssd_conv_b, ssd_dt_bias, ssd_a_log, ssd_d, ssd_norm,
                ssd_w_out)
            sf_out.append(hf.reshape(np_, SSD_HEADS, HEAD_DIM, SSD_STATE))
            sb_out.append(hb.reshape(np_, SSD_HEADS, HEAD_DIM, SSD_STATE))
        elif kind == 1:
            w_qkv = attn_w_qkv[j].astype(BF16)
            w_out = attn_w_out[j].astype(BF16)
            nk = ATTN_KV_HEADS * HEAD_DIM
            qkv = _mod_matmul(xp, mods_p, norm_mix[i], w_qkv, rows_per_group=rpg_p,
                              out_dtype=F32, tm=1024, tn=768, name="gqa_qkv_p")
            q, k, v, kc, vc = _qk_norm(qkv, attn_q_norm[j], attn_k_norm[j], seq_len=lp,
                                       rope=False, emit_cache=True)
            o = _attention(q, 0, k, 0, v, 0, None, None, nb=np_, seq_len=lp, hq=ATTN_HEADS,
                           hkv=ATTN_KV_HEADS, tq=128, tk=lp, q_scale=1.0, name="gqa_attn_p")
            xp = _proj_residual(o, xp, mods_p, w_out, rows_per_group=rpg_p, tm=512,
                                name="gqa_out_p")
            ak_out.append(kc.reshape(np_, lp, ATTN_KV_HEADS, HEAD_DIM))
            av_out.append(vc.reshape(np_, lp, ATTN_KV_HEADS, HEAD_DIM))
            qkv = _mod_matmul(xs, mods_s, norm_mix[i], w_qkv, rows_per_group=rpg_s,
                              out_dtype=F32, tm=1024, tn=768, name="gqa_qkv_s")
            q, k, v = _qk_norm(qkv, attn_q_norm[j], attn_k_norm[j], seq_len=ls, rope=True,
                               emit_cache=False)
            past = cache_attn_k.shape[2]
            ck = cache_attn_k[:, j].reshape(ns_, past, nk).astype(BF16)
            cv = _v_with_ones(cache_attn_v[:, j].reshape(ns_, past, nk)).astype(BF16)
            o = _attention(q, 0, k, 0, v, 0, ck, cv, nb=ns_, seq_len=ls, hq=ATTN_HEADS,
                           hkv=ATTN_KV_HEADS, tq=128, tk=2048, q_scale=1.0, name="gqa_attn_s",
                           v_ones=True)
            xs = _proj_residual(o, xs, mods_s, w_out, rows_per_group=rpg_s, tm=512,
                                name="gqa_out_s")
        else:
            w_qkv = nat_w_qkv[j].astype(BF16)
            w_out = nat_w_out[j].astype(BF16)
            wd = NAT_HEADS * HEAD_DIM
            scale = HEAD_DIM ** -0.5 * LOG2E
            qkv = _mod_matmul(xp, mods_p, norm_mix[i], w_qkv, rows_per_group=rpg_p,
                              out_dtype=F32, tm=1024, tn=1024, name="nat_qkv_p")
            o = _attention(qkv, 0, qkv, 1, qkv, 2, None, None, nb=np_, seq_len=lp, hq=NAT_HEADS,
                           hkv=NAT_HEADS, tq=lp, tk=lp, q_scale=scale, name="nat_attn_p")
            xp = _proj_residual(o, xp, mods_p, w_out, rows_per_group=rpg_p, tm=512,
                                name="nat_out_p")
            nk_out.append(qkv[:, wd:2 * wd].reshape(np_, lp, NAT_HEADS, HEAD_DIM))
            nv_out.append(qkv[:, 2 * wd:].reshape(np_, lp, NAT_HEADS, HEAD_DIM))
            qkv = _mod_matmul(xs, mods_s, norm_mix[i], w_qkv, rows_per_group=rpg_s,
                              out_dtype=BF16, tm=1024, tn=1024, name="nat_qkv_s")
            past = cache_nat_k.shape[2]
            ck = cache_nat_k[:, j].reshape(ns_, past, wd).astype(BF16)
            cv = cache_nat_v[:, j].reshape(ns_, past, wd).astype(BF16)
            o = _nat_latent(qkv, ck, cv, nat_rpb[j], nb=ns_, seq_len=ls)
            xs = _proj_residual(o, xs, mods_s, w_out, rows_per_group=rpg_s, tm=512,
                                name="nat_out_s")
        last = i == DEPTH - 1
        w1 = mlp_w1[i].astype(BF16)
        w2 = mlp_w2[i].astype(BF16)
        xp = _mlp(xp, mods_p, norm_mlp[i], w1, w2, norm_final, rows_per_group=rpg_p,
                  final_norm=last, name="mlp_p")
        xs = _mlp(xs, mods_s, norm_mlp[i], w1, w2, norm_final, rows_per_group=rpg_s,
                  final_norm=last, name="mlp_s")

    return (xp.reshape(np_, lp, d), xs.reshape(ns_, ls, d),
            jnp.stack(sf_out, axis=1), jnp.stack(sb_out, axis=1),
            jnp.stack(ak_out, axis=1), jnp.stack(av_out, axis=1),
            jnp.stack(nk_out, axis=1), jnp.stack(nv_out, axis=1))
```

```python
import functools
import math

import jax
import jax.numpy as jnp
from jax import lax
from jax.experimental import pallas as pl
from jax.experimental.pallas import tpu as pltpu

F32 = jnp.float32
BF16 = jnp.bfloat16

D_MODEL = 1024
DEPTH = 4
GRID_W = 64
N_MIXERS = 3
EPS = 1e-6
NEG_INF = -1e30
HEAD_DIM = 64
LANES = 128
LOG2E = math.log2(math.e)
SSD_D_INNER = 2 * D_MODEL
SSD_HEADS = SSD_D_INNER // HEAD_DIM
SSD_GROUPS = 4
SSD_HPG = SSD_HEADS // SSD_GROUPS
SSD_STATE = 128
SSD_CONV_W = 5
SSD_CHUNK = 128
SSD_CONV_DIM = SSD_D_INNER + 2 * SSD_GROUPS * SSD_STATE
SSD_ZX_DIM = SSD_D_INNER + SSD_CONV_DIM
ATTN_HEADS = D_MODEL // HEAD_DIM
ATTN_KV_HEADS = 4
ROPE_BASE = 10000.0
NAT_HEADS = D_MODEL // HEAD_DIM
NAT_KH = 8
NAT_KW = 16
MLP_HIDDEN = 4 * D_MODEL

VMEM_LIMIT_BYTES = 48 * 1024 * 1024
NAT_VMEM_LIMIT_BYTES = 56 * 1024 * 1024
MLP_VMEM_LIMIT_BYTES = 56 * 1024 * 1024


def _params(semantics, vmem=VMEM_LIMIT_BYTES):
    return pltpu.CompilerParams(dimension_semantics=semantics, vmem_limit_bytes=vmem)


def _sigmoid(x):
    return 1.0 / (1.0 + jnp.exp(-x))


def _modulated_norm(x, gain, shift, scale):
    ms = jnp.mean(x * x, axis=-1, keepdims=True)
    return (x * lax.rsqrt(ms + EPS) * gain) * (1.0 + scale) + shift


def _ada_kernel(c_ref, w_ref, b_ref, o_ref):
    c = c_ref[...]
    a = (c * _sigmoid(c)).astype(BF16)
    o_ref[...] = jnp.dot(a, w_ref[...].astype(BF16), preferred_element_type=F32) + b_ref[...]


def _ada_mods(cond, ada_w, ada_b):
    depth, d, n = ada_w.shape
    r = cond.shape[0]
    tn = 1024
    return pl.pallas_call(
        _ada_kernel,
        out_shape=jax.ShapeDtypeStruct((depth, r, n), F32),
        grid=(depth, n // tn),
        in_specs=[pl.BlockSpec((r, d), lambda l, j: (0, 0)),
                  pl.BlockSpec((None, d, tn), lambda l, j: (l, 0, j)),
                  pl.BlockSpec((None, 1, tn), lambda l, j: (l, 0, j))],
        out_specs=pl.BlockSpec((None, r, tn), lambda l, j: (l, 0, j)),
        compiler_params=_params(("parallel", "parallel")),
        name="ada_mods",
    )(cond, ada_w, ada_b.reshape(depth, 1, n))


def _modmm_kernel(*refs, shift_row, scale_row, has_side):
    if has_side:
        x_ref, mod_ref, g_ref, w_ref, ws_ref, o_ref, os_ref, h_ref = refs
    else:
        x_ref, mod_ref, g_ref, w_ref, o_ref, h_ref = refs

    @pl.when(pl.program_id(1) == 0)
    def _():
        h = _modulated_norm(x_ref[...], g_ref[...],
                            mod_ref[shift_row:shift_row + 1, :],
                            mod_ref[scale_row:scale_row + 1, :])
        h_ref[...] = h.astype(BF16)
        if has_side:
            os_ref[...] = jnp.dot(h_ref[...], ws_ref[...], preferred_element_type=F32)

    o_ref[...] = jnp.dot(h_ref[...], w_ref[...],
                         preferred_element_type=F32).astype(o_ref.dtype)


def _mod_matmul(x, mods, gain, w, *, rows_per_group, out_dtype, tm, tn, name, w_side=None):
    m, d = x.shape
    n = w.shape[1]
    tpg = rows_per_group // tm
    has_side = w_side is not None
    in_specs = [pl.BlockSpec((tm, d), lambda i, j: (i, 0)),
                pl.BlockSpec((None, 6, d), lambda i, j: (i // tpg, 0, 0)),
                pl.BlockSpec((1, d), lambda i, j: (0, 0)),
                pl.BlockSpec((d, tn), lambda i, j: (0, j))]
    args = [x, mods, gain.reshape(1, d), w]
    out_shape = jax.ShapeDtypeStruct((m, n), out_dtype)
    out_specs = pl.BlockSpec((tm, tn), lambda i, j: (i, j))
    if has_side:
        ns = w_side.shape[1]
        in_specs.append(pl.BlockSpec((d, ns), lambda i, j: (0, 0)))
        args.append(w_side)
        out_shape = [out_shape, jax.ShapeDtypeStruct((m, ns), F32)]
        out_specs = [out_specs, pl.BlockSpec((tm, ns), lambda i, j: (i, 0))]
    return pl.pallas_call(
        functools.partial(_modmm_kernel, shift_row=0, scale_row=1, has_side=has_side),
        out_shape=out_shape,
        grid=(m // tm, n // tn),
        in_specs=in_specs,
        out_specs=out_specs,
        scratch_shapes=[pltpu.VMEM((tm, d), BF16)],
        compiler_params=_params(("parallel", "arbitrary")),
        name=name,
    )(*args)


def _proj_res_kernel(a_ref, x_ref, mod_ref, w_ref, o_ref, *, gate_row):
    y = jnp.dot(a_ref[...], w_ref[...], preferred_element_type=F32)
    o_ref[...] = x_ref[...] + mod_ref[gate_row:gate_row + 1, :] * y


def _proj_residual(a, x, mods, w, *, rows_per_group, tm, name):
    m, k = a.shape
    d = x.shape[1]
    tpg = rows_per_group // tm
    return pl.pallas_call(
        functools.partial(_proj_res_kernel, gate_row=2),
        out_shape=jax.ShapeDtypeStruct((m, d), F32),
        grid=(m // tm,),
        in_specs=[pl.BlockSpec((tm, k), lambda i: (i, 0)),
                  pl.BlockSpec((tm, d), lambda i: (i, 0)),
                  pl.BlockSpec((None, 6, d), lambda i: (i // tpg, 0, 0)),
                  pl.BlockSpec((k, d), lambda i: (0, 0))],
        out_specs=pl.BlockSpec((tm, d), lambda i: (i, 0)),
        compiler_params=_params(("parallel",)),
        name=name,
    )(a, x, mods, w)


def _mlp_kernel(x_ref, xn_ref, mod_ref, modn_ref, g_ref, w1_ref, w2_ref, gf_ref, o_ref, h_ref,
                acc_ref, *, final_norm):
    i, j = pl.program_id(0), pl.program_id(1)
    last = pl.num_programs(1) - 1
    slot = i % 2

    def prepare(x_r, mod_r, dst):
        h = _modulated_norm(x_r[...], g_ref[...], mod_r[3:4, :], mod_r[4:5, :])
        h_ref[dst] = h.astype(BF16)

    def hidden_step():
        a = jnp.dot(h_ref[slot], w1_ref[...], preferred_element_type=F32)
        a = jnp.maximum(a, 0.0)
        return jnp.dot((a * a).astype(BF16), w2_ref[...], preferred_element_type=F32)

    @pl.when((i == 0) & (j == 0))
    def _():
        prepare(x_ref, mod_ref, 0)

    @pl.when(j == 0)
    def _():
        acc_ref[...] = hidden_step()

    @pl.when((j > 0) & (j < last))
    def _():
        acc_ref[...] += hidden_step()

    @pl.when(j == last)
    def _():
        y = x_ref[...] + mod_ref[5:6, :] * (acc_ref[...] + hidden_step())
        if final_norm:
            ms = jnp.mean(y * y, axis=-1, keepdims=True)
            y = y * lax.rsqrt(ms + EPS) * gf_ref[...]
        o_ref[...] = y
        prepare(xn_ref, modn_ref, 1 - slot)


def _mlp(x, mods, gain, w1, w2, gain_final, *, rows_per_group, final_norm, name):
    m, d = x.shape
    hdim = w1.shape[1]
    tm, th = 1024, 1024
    tpg = rows_per_group // tm
    ni = m // tm

    def nxt(i):
        return jnp.minimum(i + 1, ni - 1)

    return pl.pallas_call(
        functools.partial(_mlp_kernel, final_norm=final_norm),
        out_shape=jax.ShapeDtypeStruct((m, d), F32),
        grid=(ni, hdim // th),
        in_specs=[pl.BlockSpec((tm, d), lambda i, j: (i, 0)),
                  pl.BlockSpec((tm, d), lambda i, j: (nxt(i), 0)),
                  pl.BlockSpec((None, 6, d), lambda i, j: (i // tpg, 0, 0)),
                  pl.BlockSpec((None, 6, d), lambda i, j: (nxt(i) // tpg, 0, 0)),
                  pl.BlockSpec((1, d), lambda i, j: (0, 0)),
                  pl.BlockSpec((d, th), lambda i, j: (0, j)),
                  pl.BlockSpec((th, d), lambda i, j: (j, 0)),
                  pl.BlockSpec((1, d), lambda i, j: (0, 0))],
        out_specs=pl.BlockSpec((tm, d), lambda i, j: (i, 0)),
        scratch_shapes=[pltpu.VMEM((2, tm, d), BF16), pltpu.VMEM((tm, d), F32)],
        compiler_params=_params(("arbitrary", "arbitrary"), MLP_VMEM_LIMIT_BYTES),
        name=name,
    )(x, x, mods, mods, gain.reshape(1, d), w1, w2, gain_final.reshape(1, d))


CONV_HALO_ROWS = 16


CONV_TAPS_SHIFTED = (0, 1, 3, 4)


def _conv_kernel(prev_ref, cur_ref, next_ref, sh_ref, w_ref, b_ref, o_ref, *, tiles_per_seq):
    pos = pl.program_id(0) % tiles_per_seq
    cur_b = cur_ref[...]
    cur = cur_b.astype(F32)
    tr = cur.shape[0]
    h = CONV_HALO_ROWS
    w = w_ref[...]
    bias = b_ref[...]

    def silu(y):
        return (y / (1.0 + jnp.exp2(y * (-LOG2E)))).astype(o_ref.dtype)

    y = bias + w[2:3] * cur
    for idx, k in enumerate(CONV_TAPS_SHIFTED):
        y = y + w[k:k + 1] * jnp.dot(sh_ref[idx], cur_b, preferred_element_type=F32)
    o_ref[...] = silu(y)

    def edge(x):
        ye = bias + w[2:3] * x
        for k in CONV_TAPS_SHIFTED:
            ye = ye + w[k:k + 1] * pltpu.roll(x, (SSD_CONV_W // 2 - k) % x.shape[0], 0)
        return silu(ye)[h:2 * h]

    pv = jnp.where(pos == 0, 0.0, prev_ref[...].astype(F32))
    nx = jnp.where(pos == tiles_per_seq - 1, 0.0, next_ref[...].astype(F32))
    o_ref[0:h, :] = edge(jnp.concatenate([pv, cur[0:2 * h]], axis=0))
    o_ref[tr - h:tr, :] = edge(jnp.concatenate([cur[tr - 2 * h:tr], nx], axis=0))


def _conv_silu(zx, conv_w, conv_b, *, seq_len):
    m = zx.shape[0]
    tr, tc = 256, 1024
    c0 = SSD_D_INNER // tc
    halo_per_tile = tr // CONV_HALO_ROWS
    n_halo = m // CONV_HALO_ROWS
    t = jnp.arange(tr)
    shifts = jnp.stack([(t[None, :] == t[:, None] + (k - SSD_CONV_W // 2)).astype(BF16)
                        for k in CONV_TAPS_SHIFTED])
    return pl.pallas_call(
        functools.partial(_conv_kernel, tiles_per_seq=seq_len // tr),
        out_shape=jax.ShapeDtypeStruct((m, SSD_CONV_DIM), BF16),
        grid=(m // tr, SSD_CONV_DIM // tc),
        in_specs=[
            pl.BlockSpec((CONV_HALO_ROWS, tc),
                         lambda i, j: (jnp.maximum(i * halo_per_tile - 1, 0), c0 + j)),
            pl.BlockSpec((tr, tc), lambda i, j: (i, c0 + j)),
            pl.BlockSpec((CONV_HALO_ROWS, tc),
                         lambda i, j: (jnp.minimum((i + 1) * halo_per_tile, n_halo - 1), c0 + j)),
            pl.BlockSpec((len(CONV_TAPS_SHIFTED), tr, tr), lambda i, j: (0, 0, 0)),
            pl.BlockSpec((SSD_CONV_W, tc), lambda i, j: (0, j)),
            pl.BlockSpec((1, tc), lambda i, j: (0, j))],
        out_specs=pl.BlockSpec((tr, tc), lambda i, j: (i, j)),
        compiler_params=_params(("parallel", "parallel")),
        name="ssd_conv_silu",
    )(zx, zx, zx, shifts, conv_w, conv_b.reshape(1, SSD_CONV_DIM))


def _split3(a):
    a1 = a.astype(BF16)
    r1 = a - a1.astype(F32)
    a2 = r1.astype(BF16)
    a3 = (r1 - a2.astype(F32)).astype(BF16)
    return a1, a2, a3


def _dot_exact_left(m01, a):
    return sum(jnp.dot(m01, p, preferred_element_type=F32) for p in _split3(a))


def _spread_heads(a, e2_ref):
    hi = a.astype(BF16)
    lo = (a - hi.astype(F32)).astype(BF16)
    return jnp.dot(jnp.concatenate([hi, lo], axis=1), e2_ref[...], preferred_element_type=F32)


SSD_SEQS_PER_STEP = 4


def _ssd_decays(dt_ref, dtb_ref, alog_ref, *, reverse):
    q = SSD_CHUNK
    pre = dt_ref[...] + dtb_ref[...]
    dt = jnp.maximum(pre, 0.0) + jnp.log1p(jnp.exp(-jnp.abs(pre)))
    a_dt = dt * (-jnp.exp(alog_ref[...]) * LOG2E)
    ri = lax.broadcasted_iota(jnp.int32, (q, q), 0)
    ci = lax.broadcasted_iota(jnp.int32, (q, q), 1)
    keep = (ri <= ci) if reverse else (ri >= ci)
    cs = _dot_exact_left(jnp.where(keep, 1.0, 0.0).astype(BF16), a_dt)
    edge = cs[0:1, :] if reverse else cs[q - 1:q, :]
    factors = jnp.concatenate([jnp.exp2(cs), jnp.exp2(edge - cs) * dt], axis=0)
    return dict(keep=keep, cs=cs, cs_t=cs.T, dt_t=dt.T, factors=factors)


def _ssd_chunk(xbc_ref, decays, exp_cs, to_end, z_ref, yf_ref, ng_ref, drep_ref,
               y_ref, s_ref, ybuf_ref, *, reverse, combine):
    q = SSD_CHUNK
    gw = SSD_HPG * HEAD_DIM
    col0 = SSD_HEADS if reverse else 0
    keep, cs, cs_t, dt_t = decays["keep"], decays["cs"], decays["cs_t"], decays["dt_t"]
    exp_edge = exp_cs[0:1, :] if reverse else exp_cs[q - 1:q, :]
    lane = lax.broadcasted_iota(jnp.int32, (1, LANES), 1)
    keep_lo = jnp.where(lane < HEAD_DIM, 1.0, 0.0).astype(BF16)
    keep_hi = jnp.where(lane < HEAD_DIM, 0.0, 1.0).astype(BF16)

    for g in range(SSD_GROUPS):
        gs = slice(g * gw, (g + 1) * gw)
        b_g = xbc_ref[:, SSD_D_INNER + g * SSD_STATE:SSD_D_INNER + (g + 1) * SSD_STATE]
        c_off = SSD_D_INNER + SSD_GROUPS * SSD_STATE
        c_g = xbc_ref[:, c_off + g * SSD_STATE:c_off + (g + 1) * SSD_STATE]
        cb = lax.dot_general(c_g, b_g, (((1,), (1,)), ((), ())), preferred_element_type=F32)
        cb = jnp.where(keep, cb, 0.0)
        s_g = s_ref[:, gs]
        y_off = jnp.dot(c_g, s_g.astype(BF16), preferred_element_type=F32)
        for mp in range(gw // LANES):
            m = g * (gw // LANES) + mp
            ls = slice(m * LANES, (m + 1) * LANES)
            x_pair = xbc_ref[:, ls]
            k0 = col0 + 2 * m
            ws = []
            for k in (k0, k0 + 1):
                seg = jnp.minimum(cs[:, k:k + 1] - cs_t[k:k + 1, :], 0.0)
                ws.append((jnp.exp2(seg) * cb * dt_t[k:k + 1, :]).astype(BF16))
            y_pair = (jnp.dot(jnp.concatenate(ws, axis=1),
                              jnp.concatenate([x_pair * keep_lo, x_pair * keep_hi], axis=0),
                              preferred_element_type=F32)
                      + y_off[:, mp * LANES:(mp + 1) * LANES] * exp_cs[:, ls])
            if not combine:
                y_pair = y_pair + drep_ref[:, ls] * x_pair.astype(F32)
            ybuf_ref[:, ls] = y_pair.astype(ybuf_ref.dtype)
        xs_g = (to_end[:, gs] * xbc_ref[:, gs].astype(F32)).astype(BF16)
        b_t = b_g.astype(F32).T.astype(BF16)
        s_ref[:, gs] = s_g * exp_edge[:, gs] + jnp.dot(b_t, xs_g, preferred_element_type=F32)

    if combine:
        z = z_ref[...].astype(F32)
        gated = (ybuf_ref[...] + yf_ref[...]) * (z * _sigmoid(z))
        ms = jnp.mean(gated * gated, axis=-1, keepdims=True)
        y_ref[...] = (gated * lax.rsqrt(ms + EPS) * ng_ref[...]).astype(y_ref.dtype)


def _ssd_kernel(*refs, reverse, has_h0, emit_state, combine):
    it = iter(refs)
    xbc_ref, dt_ref, dtb_ref, alog_ref, e_ref = (next(it) for _ in range(5))
    h0_ref = next(it) if has_h0 else None
    z_ref = yf_ref = ng_ref = drep_ref = None
    if combine:
        z_ref, yf_ref, ng_ref = next(it), next(it), next(it)
    else:
        drep_ref = next(it)
    y_ref = next(it)
    st_ref = next(it) if emit_state else None
    s_ref = next(it)
    ybuf_ref = next(it) if combine else y_ref
    q = SSD_CHUNK
    c = pl.program_id(1)
    seqs = range(SSD_SEQS_PER_STEP)

    @pl.when(c == 0)
    def _():
        if has_h0:
            for sq in seqs:
                for k in range(SSD_D_INNER // q):
                    s_ref[sq, :, k * q:(k + 1) * q] = h0_ref[sq, k * q:(k + 1) * q, :].T
        else:
            s_ref[...] = jnp.zeros_like(s_ref)

    decays = [_ssd_decays(dt_ref.at[sq], dtb_ref, alog_ref, reverse=reverse) for sq in seqs]
    spread = _spread_heads(jnp.concatenate([d["factors"] for d in decays], axis=0), e_ref)
    for sq in seqs:
        r0 = sq * 2 * q
        _ssd_chunk(xbc_ref.at[sq], decays[sq], spread[r0:r0 + q], spread[r0 + q:r0 + 2 * q],
                   z_ref.at[sq] if combine else None, yf_ref.at[sq] if combine else None,
                   ng_ref, drep_ref, y_ref.at[sq], s_ref.at[sq], ybuf_ref.at[sq],
                   reverse=reverse, combine=combine)

    if emit_state:
        @pl.when(c == pl.num_programs(1) - 1)
        def _():
            for sq in seqs:
                for k in range(SSD_D_INNER // q):
                    st_ref[sq, k * q:(k + 1) * q, :] = s_ref[sq, :, k * q:(k + 1) * q].T


def _ssd_scan(xbc, dt_raw, dt_bias, a_log, expand, *, nseq, seq_len, reverse, h0=None,
              layer=0, emit_state=False, zx=None, y_fwd=None, norm_g=None, d_rep=None, name):
    combine = zx is not None
    nchunk = seq_len // SSD_CHUNK
    q = SSD_CHUNK
    sp = SSD_SEQS_PER_STEP
    nb = nseq // sp

    def per_seq(a):
        return a.reshape(nb, sp, seq_len, a.shape[-1])

    def chunk_spec(width):
        return pl.BlockSpec((None, sp, q, width),
                            lambda b, c: (b, 0, (nchunk - 1 - c) if reverse else c, 0))

    def const_spec(shape):
        return pl.BlockSpec(shape, lambda b, c: (0,) * len(shape))

    in_specs = [chunk_spec(SSD_CONV_DIM), chunk_spec(LANES), const_spec((1, LANES)),
                const_spec((1, LANES)), const_spec((2 * LANES, SSD_D_INNER))]
    args = [per_seq(xbc), per_seq(dt_raw), dt_bias, a_log, expand]
    if h0 is not None:
        in_specs.append(pl.BlockSpec((None, sp, None, SSD_D_INNER, SSD_STATE),
                                     lambda b, c: (b, 0, layer, 0, 0)))
        args.append(h0.reshape((nb, sp) + h0.shape[1:]))
    if combine:
        in_specs += [chunk_spec(SSD_D_INNER), chunk_spec(SSD_D_INNER),
                     const_spec((1, SSD_D_INNER))]
        args += [per_seq(zx), per_seq(y_fwd), norm_g]
    else:
        in_specs.append(const_spec((1, SSD_D_INNER)))
        args.append(d_rep)
    out_shape = [jax.ShapeDtypeStruct((nb, sp, seq_len, SSD_D_INNER), BF16)]
    out_specs = [chunk_spec(SSD_D_INNER)]
    if emit_state:
        out_shape.append(jax.ShapeDtypeStruct((nb, sp, SSD_D_INNER, SSD_STATE), F32))
        out_specs.append(pl.BlockSpec((None, sp, SSD_D_INNER, SSD_STATE),
                                      lambda b, c: (b, 0, 0, 0)))
    scratch = [pltpu.VMEM((sp, SSD_STATE, SSD_D_INNER), F32)]
    if combine:
        scratch.append(pltpu.VMEM((sp, q, SSD_D_INNER), F32))
    outs = list(pl.pallas_call(
        functools.partial(_ssd_kernel, reverse=reverse, has_h0=h0 is not None,
                          emit_state=emit_state, combine=combine),
        out_shape=out_shape,
        grid=(nb, nchunk),
        in_specs=in_specs,
        out_specs=out_specs,
        scratch_shapes=scratch,
        compiler_params=_params(("parallel", "arbitrary")),
        name=name,
    )(*args))
    outs[0] = outs[0].reshape(nseq * seq_len, SSD_D_INNER)
    if emit_state:
        outs[1] = outs[1].reshape(nseq, SSD_D_INNER, SSD_STATE)
    return outs


def _head_rms(x, gain_row):
    rows, width = x.shape
    lane = lax.broadcasted_iota(jnp.int32, (rows, LANES), 1)
    lo = lane < HEAD_DIM
    outs = []
    for j in range(width // LANES):
        xb = x[:, j * LANES:(j + 1) * LANES]
        sq = xb * xb
        s_lo = jnp.sum(jnp.where(lo, sq, 0.0), axis=-1, keepdims=True)
        s_hi = jnp.sum(jnp.where(lo, 0.0, sq), axis=-1, keepdims=True)
        ms = jnp.where(lo, s_lo, s_hi) * (1.0 / HEAD_DIM)
        outs.append(xb * lax.rsqrt(ms + EPS))
    return jnp.concatenate(outs, axis=1) * gain_row


def _rope(x, cos, sin_signed):
    rows, width = x.shape
    lane = lax.broadcasted_iota(jnp.int32, (rows, LANES), 1)
    even = (lane % 2) == 0
    outs = []
    for j in range(width // LANES):
        xb = x[:, j * LANES:(j + 1) * LANES]
        partner = jnp.where(even, pltpu.roll(xb, LANES - 1, 1), pltpu.roll(xb, 1, 1))
        outs.append(xb * cos + partner * sin_signed)
    return jnp.concatenate(outs, axis=1)


def _qknorm_kernel(*refs, rope, emit_cache):
    it = iter(refs)
    q_ref, k_ref, v_ref, qg_ref, kg_ref = (next(it) for _ in range(5))
    if rope:
        cos_ref, sin_ref = next(it), next(it)
    qo_ref, ko_ref, vo_ref = next(it), next(it), next(it)
    q = _head_rms(q_ref[...], qg_ref[...])
    k = _head_rms(k_ref[...], kg_ref[...])
    if emit_cache:
        kc_ref, vc_ref = next(it), next(it)
        kc_ref[...] = k
        vc_ref[...] = v_ref[...]
    if rope:
        q = _rope(q, cos_ref[...], sin_ref[...])
        k = _rope(k, cos_ref[...], sin_ref[...])
    qo_ref[...] = (q * (HEAD_DIM ** -0.5 * LOG2E)).astype(BF16)
    ko_ref[...] = k.astype(BF16)
    v = v_ref[...]
    vo_ref[...] = (_v_with_ones(v) if rope else v).astype(BF16)


def _rope_tables(seq_len):
    n_pairs = HEAD_DIM // 4
    t = jnp.arange(seq_len)
    row = (t // GRID_W).astype(F32)
    col = (t % GRID_W).astype(F32)
    freqs = ROPE_BASE ** (-jnp.arange(n_pairs, dtype=F32) / n_pairs)
    ang = jnp.concatenate([row[:, None] * freqs, col[:, None] * freqs], axis=-1)
    cos = jnp.repeat(jnp.cos(ang), 2, axis=-1)
    sin = jnp.repeat(jnp.sin(ang), 2, axis=-1)
    sign = jnp.where(jnp.arange(HEAD_DIM) % 2 == 0, -1.0, 1.0).astype(F32)
    return jnp.tile(cos, (1, 2)), jnp.tile(sin * sign, (1, 2))


def _qk_norm(qkv, q_gain, k_gain, *, seq_len, rope, emit_cache):
    m = qkv.shape[0]
    tm = 256
    nq = ATTN_HEADS * HEAD_DIM
    nk = ATTN_KV_HEADS * HEAD_DIM
    qg = jnp.tile(q_gain, ATTN_HEADS).reshape(1, nq)
    kg = jnp.tile(k_gain, ATTN_KV_HEADS).reshape(1, nk)
    in_specs = [pl.BlockSpec((tm, nq), lambda i: (i, 0)),
                pl.BlockSpec((tm, nk), lambda i: (i, nq // nk)),
                pl.BlockSpec((tm, nk), lambda i: (i, nq // nk + 1)),
                pl.BlockSpec((1, nq), lambda i: (0, 0)),
                pl.BlockSpec((1, nk), lambda i: (0, 0))]
    args = [qkv, qkv, qkv, qg, kg]
    if rope:
        cos, sin = _rope_tables(seq_len)
        tps = seq_len // tm
        in_specs += [pl.BlockSpec((tm, LANES), lambda i: (i % tps, 0)),
                     pl.BlockSpec((tm, LANES), lambda i: (i % tps, 0))]
        args += [cos, sin]
    out_shape = [jax.ShapeDtypeStruct((m, nq), BF16), jax.ShapeDtypeStruct((m, nk), BF16),
                 jax.ShapeDtypeStruct((m, nk), BF16)]
    out_specs = [pl.BlockSpec((tm, nq), lambda i: (i, 0)), pl.BlockSpec((tm, nk), lambda i: (i, 0)),
                 pl.BlockSpec((tm, nk), lambda i: (i, 0))]
    if rope:
        out_shape[2] = jax.ShapeDtypeStruct((m, ATTN_KV_HEADS * LANES), BF16)
        out_specs[2] = pl.BlockSpec((tm, ATTN_KV_HEADS * LANES), lambda i: (i, 0))
    if emit_cache:
        out_shape += [jax.ShapeDtypeStruct((m, nk), F32)] * 2
        out_specs += [pl.BlockSpec((tm, nk), lambda i: (i, 0))] * 2
    return pl.pallas_call(
        functools.partial(_qknorm_kernel, rope=rope, emit_cache=emit_cache),
        out_shape=out_shape,
        grid=(m // tm,),
        in_specs=in_specs,
        out_specs=out_specs,
        compiler_params=_params(("parallel",)),
        name="gqa_qk_norm_rope" if rope else "gqa_qk_norm",
    )(*args)


def _place_half(x, lane, src_half, dst_half):
    if src_half != dst_half:
        x = pltpu.roll(x, HEAD_DIM, 1)
    keep = (lane >= HEAD_DIM) if dst_half else (lane < HEAD_DIM)
    return jnp.where(keep, x, 0.0)


def _v_with_ones(v):
    lane = lax.broadcasted_iota(jnp.int32, v.shape[:-1] + (LANES,), v.ndim - 1)
    tiles = []
    for g in range(v.shape[-1] // HEAD_DIM):
        pair = v[..., (g // 2) * LANES:(g // 2 + 1) * LANES]
        own = (lane >= HEAD_DIM) if g % 2 else (lane < HEAD_DIM)
        tiles.append(jnp.where(own, pair, 1.0))
    return jnp.concatenate(tiles, axis=-1)


def _attn_kernel(*refs, hq, hkv, tk, has_ctx, q_scale, v_ones):
    if has_ctx:
        q_ref, k_ref, v_ref, ck_ref, cv_ref, o_ref = refs
    else:
        q_ref, k_ref, v_ref, o_ref = refs
    grp = hq // hkv
    tq = q_ref.shape[0]
    rows = grp * tq
    lane = lax.broadcasted_iota(jnp.int32, (tq, LANES), 1)
    nt = (((1,), (1,)), ((), ()))

    qst = []
    for g in range(hkv):
        parts = []
        for u in range(grp):
            h = g * grp + u
            qb = q_ref[:, (h // 2) * LANES:(h // 2 + 1) * LANES].astype(F32)
            if q_scale != 1.0:
                qb = qb * q_scale
            parts.append(_place_half(qb, lane, h % 2, g % 2).astype(BF16))
        qst.append(parts[0] if grp == 1 else jnp.concatenate(parts, axis=0))

    def step(g, kblk, vblk, carry):
        m_i, l_i, acc = carry
        s = lax.dot_general(qst[g], kblk.astype(BF16), nt, preferred_element_type=F32)
        m_n = jnp.maximum(m_i, jnp.max(s, axis=-1, keepdims=True))
        alpha = jnp.exp2(m_i - m_n)
        p = jnp.exp2(s - m_n)
        acc = alpha * acc + jnp.dot(p.astype(BF16), vblk.astype(BF16),
                                    preferred_element_type=F32)
        if v_ones:
            return m_n, l_i, acc
        return m_n, alpha * l_i + jnp.sum(p, axis=-1, keepdims=True), acc

    def all_heads(kv_at, carries):
        out = []
        for g in range(hkv):
            kc = slice((g // 2) * LANES, (g // 2 + 1) * LANES)
            vc = slice(g * LANES, (g + 1) * LANES) if v_ones else kc
            out.append(step(g, *kv_at(kc, vc), carries[g]))
        return tuple(out)

    carries = tuple((jnp.full((rows, 1), -jnp.inf, F32), jnp.zeros((rows, 1), F32),
                     jnp.zeros((rows, LANES), F32)) for _ in range(hkv))
    n_self = k_ref.shape[0] // tk
    if n_self == 1:
        carries = all_heads(lambda kc, vc: (k_ref[:, kc], v_ref[:, vc]), carries)
    else:
        def body(ci, car):
            r0 = pl.multiple_of(ci * tk, tk)
            return all_heads(lambda kc, vc: (k_ref[pl.ds(r0, tk), kc], v_ref[pl.ds(r0, tk), vc]),
                             car)
        carries = lax.fori_loop(0, n_self, body, carries, unroll=2)
    if has_ctx:
        carries = all_heads(lambda kc, vc: (ck_ref[:, kc], cv_ref[:, vc]), carries)

    head_out = [None] * hq
    for g in range(hkv):
        _, l_f, acc = carries[g]
        if v_ones:
            ones_lane = HEAD_DIM * (1 - g % 2)
            l_f = acc[:, ones_lane:ones_lane + 1]
        o = acc / l_f
        for u in range(grp):
            head_out[g * grp + u] = o[u * tq:(u + 1) * tq]
    for mpair in range(hq // 2):
        oa, ob = head_out[2 * mpair], head_out[2 * mpair + 1]
        if ((2 * mpair) // grp) % 2 != 0:
            oa = pltpu.roll(oa, HEAD_DIM, 1)
        if ((2 * mpair + 1) // grp) % 2 != 1:
            ob = pltpu.roll(ob, HEAD_DIM, 1)
        o_ref[:, mpair * LANES:(mpair + 1) * LANES] = jnp.where(
            lane < HEAD_DIM, oa, ob).astype(o_ref.dtype)


def _attention(q_arr, q_col, k_arr, k_col, v_arr, v_col, ctx_k, ctx_v, *, nb, seq_len,
               hq, hkv, tq, tk, q_scale, name, v_ones=False):
    wq, wk = hq * HEAD_DIM, hkv * HEAD_DIM
    wv = hkv * LANES if v_ones else wk
    qt = seq_len // tq
    in_specs = [pl.BlockSpec((tq, wq), lambda b, i: (b * qt + i, q_col)),
                pl.BlockSpec((seq_len, wk), lambda b, i: (b, k_col)),
                pl.BlockSpec((seq_len, wv), lambda b, i: (b, v_col))]
    args = [q_arr, k_arr, v_arr]
    has_ctx = ctx_k is not None
    if has_ctx:
        lc = ctx_k.shape[1]
        in_specs += [pl.BlockSpec((None, lc, wk), lambda b, i: (b, 0, 0)),
                     pl.BlockSpec((None, lc, wv), lambda b, i: (b, 0, 0))]
        args += [ctx_k, ctx_v]
    return pl.pallas_call(
        functools.partial(_attn_kernel, hq=hq, hkv=hkv, tk=tk, has_ctx=has_ctx, q_scale=q_scale,
                          v_ones=v_ones),
        out_shape=jax.ShapeDtypeStruct((nb * seq_len, wq), BF16),
        grid=(nb, qt),
        in_specs=in_specs,
        out_specs=pl.BlockSpec((tq, wq), lambda b, i: (b * qt + i, 0)),
        compiler_params=_params(("parallel", "arbitrary")),
        name=name,
    )(*args)


NAT_ROWS_PER_STEP = 4


def _nat_kernel(q_ref, k_ref, v_ref, ck_ref, cv_ref, *rest, rows):
    bias_refs, o_ref = rest[:NAT_ROWS_PER_STEP], rest[NAT_ROWS_PER_STEP]
    kh, w = NAT_KH, GRID_W
    nloc = kh * w
    pair_rows = 2 * w
    lane = lax.broadcasted_iota(jnp.int32, (w, LANES), 1)
    nt = (((1,), (1,)), ((), ()))
    q_scale = (HEAD_DIM ** -0.5) * LOG2E
    for mpair in range(NAT_HEADS // 2):
        ls = slice(mpair * LANES, (mpair + 1) * LANES)
        k_ctx = ck_ref[:, ls]
        v_ctx = cv_ref[:, ls]
        qs = []
        for j in range(NAT_ROWS_PER_STEP):
            qb = q_ref[j * w:(j + 1) * w, ls].astype(F32) * q_scale
            qs.append(jnp.concatenate([_place_half(qb, lane, 0, 0), _place_half(qb, lane, 1, 1)],
                                      axis=0).astype(BF16))
        s_ctx_all = lax.dot_general(jnp.concatenate(qs, axis=0), k_ctx, nt,
                                    preferred_element_type=F32)
        p_ctx, o_loc, denom = [], [], []
        for j in range(NAT_ROWS_PER_STEP):
            r = pl.program_id(1) * NAT_ROWS_PER_STEP + j
            k0 = pl.multiple_of(jnp.clip(r - kh // 2, 0, rows - kh) * w, w)
            s_loc = (lax.dot_general(qs[j], k_ref[pl.ds(k0, nloc), ls], nt,
                                     preferred_element_type=F32) + bias_refs[j][mpair])
            s_ctx = s_ctx_all[j * pair_rows:(j + 1) * pair_rows]
            mx = jnp.maximum(jnp.max(s_loc, axis=-1, keepdims=True),
                             jnp.max(s_ctx, axis=-1, keepdims=True))
            p_loc = jnp.exp2(s_loc - mx)
            pc = jnp.exp2(s_ctx - mx)
            denom.append(jnp.sum(p_loc, axis=-1, keepdims=True)
                         + jnp.sum(pc, axis=-1, keepdims=True))
            o_loc.append(jnp.dot(p_loc.astype(BF16), v_ref[pl.ds(k0, nloc), ls],
                                 preferred_element_type=F32))
            p_ctx.append(pc.astype(BF16))
        o_ctx_all = jnp.dot(jnp.concatenate(p_ctx, axis=0), v_ctx, preferred_element_type=F32)
        for j in range(NAT_ROWS_PER_STEP):
            o = (o_loc[j] + o_ctx_all[j * pair_rows:(j + 1) * pair_rows]) / denom[j]
            o_ref[j * w:(j + 1) * w, ls] = jnp.where(lane < HEAD_DIM, o[:w], o[w:]).astype(o_ref.dtype)


def _nat_bias(rpb):
    col = jnp.arange(GRID_W)
    cstart = jnp.clip(col - NAT_KW // 2, 0, GRID_W - NAT_KW)
    col_ok = (col[None, :] >= cstart[:, None]) & (col[None, :] < cstart[:, None] + NAT_KW)
    dc_idx = jnp.clip(col[None, :] - col[:, None] + NAT_KW - 1, 0, 2 * NAT_KW - 2)
    rpb_c = jnp.where(col_ok[None, None], rpb[:, :, dc_idx] * LOG2E, NEG_INF)
    variants = []
    for d0 in range(NAT_KH):
        blk = rpb_c[:, d0:d0 + NAT_KH].transpose(0, 2, 1, 3)
        variants.append(blk.reshape(NAT_HEADS // 2, 2 * GRID_W, NAT_KH * GRID_W))
    return jnp.stack(variants, axis=0).astype(BF16)


def _nat_latent(qkv, ctx_k, ctx_v, rpb, *, nb, seq_len):
    rows = seq_len // GRID_W
    kh = NAT_KH
    rb = NAT_ROWS_PER_STEP
    bias = _nat_bias(rpb)
    lc = ctx_k.shape[1]
    wd = NAT_HEADS * HEAD_DIM
    steps = rows // rb

    def bias_spec(j):
        def idx(b, i):
            r = i * rb + j
            return (jnp.clip(r - kh // 2, 0, rows - kh) - r + kh - 1, 0, 0, 0)
        return pl.BlockSpec((None, NAT_HEADS // 2, 2 * GRID_W, kh * GRID_W), idx)

    return pl.pallas_call(
        functools.partial(_nat_kernel, rows=rows),
        out_shape=jax.ShapeDtypeStruct((nb * seq_len, wd), BF16),
        grid=(nb, steps),
        in_specs=[pl.BlockSpec((rb * GRID_W, wd), lambda b, i: (b * steps + i, 0)),
                  pl.BlockSpec((seq_len, wd), lambda b, i: (b, 1), pipeline_mode=pl.Buffered(1)),
                  pl.BlockSpec((seq_len, wd), lambda b, i: (b, 2), pipeline_mode=pl.Buffered(1)),
                  pl.BlockSpec((None, lc, wd), lambda b, i: (b, 0, 0)),
                  pl.BlockSpec((None, lc, wd), lambda b, i: (b, 0, 0))]
                 + [bias_spec(j) for j in range(rb)],
        out_specs=pl.BlockSpec((rb * GRID_W, wd), lambda b, i: (b * steps + i, 0)),
        compiler_params=_params(("parallel", "arbitrary"), NAT_VMEM_LIMIT_BYTES),
        name="nat_latent",
    )(qkv, qkv, qkv, ctx_k, ctx_v, *([bias] * rb))


def _ssd_layer(xp, xs, mods_p, mods_s, gain, j, np_, ns_, lp, ls, state_f, state_b,
               ssd_w_in, ssd_conv_w, ssd_conv_b, ssd_dt_bias, ssd_a_log, ssd_d, ssd_norm,
               ssd_w_out):
    w_zx = ssd_w_in[j, :, :SSD_ZX_DIM].astype(BF16)
    w_dt = jnp.pad(ssd_w_in[j, :, SSD_ZX_DIM:], ((0, 0), (0, LANES - 2 * SSD_HEADS))).astype(BF16)
    w_out = ssd_w_out[j].astype(BF16)
    pad = LANES - 2 * SSD_HEADS
    dt_bias = jnp.pad(ssd_dt_bias[j].reshape(-1), (0, pad)).reshape(1, LANES)
    a_log = jnp.pad(ssd_a_log[j].reshape(-1), (0, pad)).reshape(1, LANES)
    d_rep = jnp.repeat(ssd_d[j], HEAD_DIM).reshape(1, SSD_D_INNER)
    norm_g = ssd_norm[j].reshape(1, SSD_D_INNER)
    head_of_col = jnp.arange(SSD_D_INNER) // HEAD_DIM
    expand = [jnp.tile((jnp.arange(LANES)[:, None] == head_of_col[None, :] + off).astype(BF16), (2, 1))
              for off in (0, SSD_HEADS)]

    outs = []
    states = None
    for (x, mods, nseq, seq_len, is_prompt) in ((xp, mods_p, np_, lp, True),
                                                (xs, mods_s, ns_, ls, False)):
        tag = "p" if is_prompt else "s"
        zx, dt_raw = _mod_matmul(x, mods, gain, w_zx, w_side=w_dt,
                                 rows_per_group=seq_len if not is_prompt else x.shape[0],
                                 out_dtype=BF16, tm=1024, tn=1024, name="ssd_in_" + tag)
        xbc = _conv_silu(zx, ssd_conv_w[j], ssd_conv_b[j], seq_len=seq_len)
        common = dict(nseq=nseq, seq_len=seq_len, layer=j)
        fwd = _ssd_scan(xbc, dt_raw, dt_bias, a_log, expand[0], reverse=False,
                        h0=None if is_prompt else state_f, emit_state=is_prompt, d_rep=d_rep,
                        name="ssd_scan_fwd_" + tag, **common)
        bwd = _ssd_scan(xbc, dt_raw, dt_bias, a_log, expand[1], reverse=True,
                        h0=None if is_prompt else state_b, emit_state=is_prompt, zx=zx,
                        y_fwd=fwd[0], norm_g=norm_g, name="ssd_scan_bwd_" + tag, **common)
        if is_prompt:
            states = (fwd[1], bwd[1])
        outs.append(_proj_residual(bwd[0], x, mods,
                                   w_out, rows_per_group=seq_len if not is_prompt else x.shape[0],
                                   tm=512, name="ssd_out_" + tag))
    return outs[0], outs[1], states


def kernel(x_prompt, x_sample, c, state_ssd_fwd, state_ssd_bwd, cache_attn_k, cache_attn_v, cache_nat_k, cache_nat_v, c_ctx, ada_w, ada_b, norm_mix, norm_mlp, mlp_w1, mlp_w2, ssd_w_in, ssd_conv_w, ssd_conv_b, ssd_dt_bias, ssd_a_log, ssd_d, ssd_norm, ssd_w_out, attn_w_qkv, attn_q_norm, attn_k_norm, attn_w_out, nat_w_qkv, nat_rpb, nat_w_out, norm_final):
    np_, lp, d = x_prompt.shape
    ns_, ls, _ = x_sample.shape
    mp_rows, ms_rows = np_ * lp, ns_ * ls
    xp = x_prompt.reshape(mp_rows, d)
    xs = x_sample.reshape(ms_rows, d)

    cond_rows = 16
    cond = jnp.concatenate([c_ctx[None, :], c, jnp.zeros((cond_rows - 1 - ns_, d), F32)], axis=0)
    mods_all = _ada_mods(cond, ada_w, ada_b).reshape(DEPTH, cond_rows, 6, d)

    n_ssd = state_ssd_fwd.shape[1]
    st_f = state_ssd_fwd.reshape(ns_, n_ssd, SSD_D_INNER, SSD_STATE)
    st_b = state_ssd_bwd.reshape(ns_, n_ssd, SSD_D_INNER, SSD_STATE)

    sf_out, sb_out, ak_out, av_out, nk_out, nv_out = [], [], [], [], [], []
    for i in range(DEPTH):
        kind, j = i % N_MIXERS, i // N_MIXERS
        mods_p = mods_all[i, 0:1]
        mods_s = mods_all[i, 1:1 + ns_]
        rpg_p, rpg_s = mp_rows, ls
        if kind == 0:
            xp, xs, (hf, hb) = _ssd_layer(
                xp, xs, mods_p, mods_s, norm_mix[i], j, np_, ns_, lp, ls, st_f, st_b,
                ssd_w_in, ssd_conv_w, ssd_conv_b, ssd_dt_bias, ssd_a_log, ssd_d, ssd_norm,
                ssd_w_out)
            sf_out.append(hf.reshape(np_, SSD_HEADS, HEAD_DIM, SSD_STATE))
            sb_out.append(hb.reshape(np_, SSD_HEADS, HEAD_DIM, SSD_STATE))
        elif kind == 1:
            w_qkv = attn_w_qkv[j].astype(BF16)
            w_out = attn_w_out[j].astype(BF16)
            nk = ATTN_KV_HEADS * HEAD_DIM
            qkv = _mod_matmul(xp, mods_p, norm_mix[i], w_qkv, rows_per_group=rpg_p,
                              out_dtype=F32, tm=1024, tn=768, name="gqa_qkv_p")
            q, k, v, kc, vc = _qk_norm(qkv, attn_q_norm[j], attn_k_norm[j], seq_len=lp,
                                       rope=False, emit_cache=True)
            o = _attention(q, 0, k, 0, v, 0, None, None, nb=np_, seq_len=lp, hq=ATTN_HEADS,
                           hkv=ATTN_KV_HEADS, tq=128, tk=lp, q_scale=1.0, name="gqa_attn_p")
            xp = _proj_residual(o, xp, mods_p, w_out, rows_per_group=rpg_p, tm=512,
                                name="gqa_out_p")
            ak_out.append(kc.reshape(np_, lp, ATTN_KV_HEADS, HEAD_DIM))
            av_out.append(vc.reshape(np_, lp, ATTN_KV_HEADS, HEAD_DIM))
            qkv = _mod_matmul(xs, mods_s, norm_mix[i], w_qkv, rows_per_group=rpg_s,
                              out_dtype=F32, tm=1024, tn=768, name="gqa_qkv_s")
            q, k, v = _qk_norm(qkv, attn_q_norm[j], attn_k_norm[j], seq_len=ls, rope=True,
                               emit_cache=False)
            past = cache_attn_k.shape[2]
            ck = cache_attn_k[:, j].reshape(ns_, past, nk).astype(BF16)
            cv = _v_with_ones(cache_attn_v[:, j].reshape(ns_, past, nk)).astype(BF16)
            o = _attention(q, 0, k, 0, v, 0, ck, cv, nb=ns_, seq_len=ls, hq=ATTN_HEADS,
                           hkv=ATTN_KV_HEADS, tq=128, tk=2048, q_scale=1.0, name="gqa_attn_s",
                           v_ones=True)
            xs = _proj_residual(o, xs, mods_s, w_out, rows_per_group=rpg_s, tm=512,
                                name="gqa_out_s")
        else:
            w_qkv = nat_w_qkv[j].astype(BF16)
            w_out = nat_w_out[j].astype(BF16)
            wd = NAT_HEADS * HEAD_DIM
            scale = HEAD_DIM ** -0.5 * LOG2E
            qkv = _mod_matmul(xp, mods_p, norm_mix[i], w_qkv, rows_per_group=rpg_p,
                              out_dtype=F32, tm=1024, tn=1024, name="nat_qkv_p")
            o = _attention(qkv, 0, qkv, 1, qkv, 2, None, None, nb=np_, seq_len=lp, hq=NAT_HEADS,
                           hkv=NAT_HEADS, tq=lp, tk=lp, q_scale=scale, name="nat_attn_p")
            xp = _proj_residual(o, xp, mods_p, w_out, rows_per_group=rpg_p, tm=512,
                                name="nat_out_p")
            nk_out.append(qkv[:, wd:2 * wd].reshape(np_, lp, NAT_HEADS, HEAD_DIM))
            nv_out.append(qkv[:, 2 * wd:].reshape(np_, lp, NAT_HEADS, HEAD_DIM))
            qkv = _mod_matmul(xs, mods_s, norm_mix[i], w_qkv, rows_per_group=rpg_s,
                              out_dtype=BF16, tm=1024, tn=1024, name="nat_qkv_s")
            past = cache_nat_k.shape[2]
            ck = cache_nat_k[:, j].reshape(ns_, past, wd).astype(BF16)
            cv = cache_nat_v[:, j].reshape(ns_, past, wd).astype(BF16)
            o = _nat_latent(qkv, ck, cv, nat_rpb[j], nb=ns_, seq_len=ls)
            xs = _proj_residual(o, xs, mods_s, w_out, rows_per_group=rpg_s, tm=512,
                                name="nat_out_s")
        last = i == DEPTH - 1
        w1 = mlp_w1[i].astype(BF16)
        w2 = mlp_w2[i].astype(BF16)
        xp = _mlp(xp, mods_p, norm_mlp[i], w1, w2, norm_final, rows_per_group=rpg_p,
                  final_norm=last, name="mlp_p")
        xs = _mlp(xs, mods_s, norm_mlp[i], w1, w2, norm_final, rows_per_group=rpg_s,
                  final_norm=last, name="mlp_s")

    return (xp.reshape(np_, lp, d), xs.reshape(ns_, ls, d),
            jnp.stack(sf_out, axis=1), jnp.stack(sb_out, axis=1),
            jnp.stack(ak_out, axis=1), jnp.stack(av_out, axis=1),
            jnp.stack(nk_out, axis=1), jnp.stack(nv_out, axis=1))
```
